```python
import math, functools
import jax, jax.numpy as jnp
from jax import lax
import numpy as np

D_MODEL = 2048
BATCH = 4
SEQ = 4096
DEPTH = 1
DEC_BATCH = 128
DEC_SEQ = 1
PAST_LEN = 16384
PAGE_SIZE = 128

MIX_WIDTH = D_MODEL
MLA_WIDTH = MIX_WIDTH // 2
S5_WIDTH = MIX_WIDTH - MLA_WIDTH
MLA_HEADS = 8
QK_NOPE_DIM = 128
QK_ROPE_DIM = 64
V_HEAD_DIM = MLA_WIDTH // MLA_HEADS
Q_LORA_RANK = D_MODEL // 4
KV_LORA_RANK = D_MODEL // 8
ROPE_THETA = 10000.0
SOFTMAX_SCALE = (QK_NOPE_DIM + QK_ROPE_DIM) ** -0.5
Q_BLOCK = 128
S5_GROUP = 16
S5_GROUPS = S5_WIDTH // S5_GROUP
S5_STATE = 64
DT_MIN = 0.001
DT_MAX = 0.1
D_FF = ((8 * D_MODEL // 3 + 255) // 256) * 256
ADA_CHUNKS = 9
IN_COLS = Q_LORA_RANK + KV_LORA_RANK + QK_ROPE_DIM + S5_WIDTH
EPS = 1e-6

kernel_name = 'hymba_mla_s5_macaron_adaln_step'


def rms_norm(x, g):
    x32 = x.astype(jnp.float32)
    y = x32 * lax.rsqrt(jnp.mean(x32 * x32, axis=-1, keepdims=True) + EPS)
    return (y * g.astype(jnp.float32)).astype(x.dtype)


def modulate(h, shift, scale):
    return h * (1.0 + scale) + shift


def ada_modulation(c, w_ada, b_ada):
    mod = (jax.nn.silu(c) @ w_ada + b_ada)[:, None, :]
    return jnp.split(mod, ADA_CHUNKS, axis=-1)


def macaron_ffn(x, shift, scale, gate, g, w1, w3, w2):
    h = modulate(rms_norm(x, g), shift, scale)
    return x + 0.5 * gate * ((jax.nn.silu(h @ w1) * (h @ w3)) @ w2)


def apply_rope(x, pos):
    half = QK_ROPE_DIM // 2
    inv_freq = ROPE_THETA ** (-jnp.arange(half, dtype=jnp.float32) / half)
    ang = pos.astype(jnp.float32)[:, None] * inv_freq[None, :]
    cos = jnp.cos(ang)[None, :, None, :]
    sin = jnp.sin(ang)[None, :, None, :]
    x32 = x.astype(jnp.float32)
    x1, x2 = x32[..., :half], x32[..., half:]
    return jnp.concatenate([x1 * cos - x2 * sin, x1 * sin + x2 * cos], axis=-1).astype(x.dtype)


def latent_scores(q_lat, q_rope, ckv, krope):
    s = jnp.einsum('bthc,bkc->bhtk', q_lat, ckv, preferred_element_type=jnp.float32)
    s = s + jnp.einsum('bthr,bkr->bhtk', q_rope, krope, preferred_element_type=jnp.float32)
    return s * SOFTMAX_SCALE


def prompt_attention(q_lat, q_rope, ckv, krope):
    b, s = ckv.shape[:2]
    nb = s // Q_BLOCK
    ql = q_lat.reshape(b, nb, Q_BLOCK, MLA_HEADS, KV_LORA_RANK).swapaxes(0, 1)
    qr = q_rope.reshape(b, nb, Q_BLOCK, MLA_HEADS, QK_ROPE_DIM).swapaxes(0, 1)
    k_pos = jnp.arange(s)

    def block(args):
        i, ql_b, qr_b = args
        sc = latent_scores(ql_b, qr_b, ckv, krope)
        q_pos = i * Q_BLOCK + jnp.arange(Q_BLOCK)
        sc = jnp.where(k_pos[None, :] <= q_pos[:, None], sc, -jnp.inf)
        p = jax.nn.softmax(sc, axis=-1).astype(ckv.dtype)
        return jnp.einsum('bhtk,bkc->bthc', p, ckv)

    o = lax.map(block, (jnp.arange(nb), ql, qr))
    return o.swapaxes(0, 1).reshape(b, s, MLA_HEADS, KV_LORA_RANK)


def sample_attention(ckv_past, krope_past, q_lat, q_rope, ckv_new, krope_new):
    t = q_lat.shape[1]
    past = ckv_past.shape[1]
    s_past = latent_scores(q_lat, q_rope, ckv_past, krope_past)
    s_new = latent_scores(q_lat, q_rope, ckv_new, krope_new)
    causal = jnp.arange(t)[None, :] <= jnp.arange(t)[:, None]
    s_new = jnp.where(causal, s_new, -jnp.inf)
    p = jax.nn.softmax(jnp.concatenate([s_past, s_new], axis=-1), axis=-1).astype(ckv_new.dtype)
    return (jnp.einsum('bhtk,bkc->bthc', p[..., :past], ckv_past)
            + jnp.einsum('bhtk,bkc->bthc', p[..., past:], ckv_new))


def s5_scan(u, h0, a_re, a_im, log_dt, b_re, b_im, c_re, c_im, d_skip):
    f32 = jnp.float32
    b, t = u.shape[:2]
    lam = lax.complex(a_re.astype(f32), a_im.astype(f32))
    dt = jnp.exp(log_dt.astype(f32))[:, None]
    lam_bar = jnp.exp(lam * dt)
    b_bar = ((lam_bar - 1.0) / lam)[:, :, None] * lax.complex(b_re.astype(f32), b_im.astype(f32))
    c_mat = lax.complex(c_re.astype(f32), c_im.astype(f32))
    u_g = u.astype(f32).reshape(b, t, S5_GROUPS, S5_GROUP)
    bu = jnp.einsum('gpc,btgc->btgp', b_bar, u_g.astype(jnp.complex64))
    bu = bu.at[:, 0].add(lam_bar[None] * h0)
    a = jnp.broadcast_to(lam_bar, bu.shape)

    def combine(left, right):
        a_l, b_l = left
        a_r, b_r = right
        return a_l * a_r, a_r * b_l + b_r

    _, h = lax.associative_scan(combine, (a, bu), axis=1)
    y = jnp.real(jnp.einsum('gcp,btgp->btgc', c_mat, h)) + d_skip.astype(f32).reshape(S5_GROUPS, S5_GROUP) * u_g
    return y.reshape(b, t, S5_WIDTH).astype(u.dtype), h[:, -1]


def decoder_layer(x, c, pos, h0, attend, p):
    b, t = x.shape[:2]
    sh1, sc1, g1, sh2, sc2, g2, sh3, sc3, g3 = ada_modulation(c, p['w_ada'], p['b_ada'])
    x = macaron_ffn(x, sh1, sc1, g1, p['norm_ffn1'], p['ffn1_w1'], p['ffn1_w3'], p['ffn1_w2'])
    h = modulate(rms_norm(x, p['norm_mix']), sh2, sc2)
    proj = h @ p['w_in']
    q_a, kv_a, k_r, u = jnp.split(
        proj, [Q_LORA_RANK, Q_LORA_RANK + KV_LORA_RANK, Q_LORA_RANK + KV_LORA_RANK + QK_ROPE_DIM], axis=-1)
    q = (rms_norm(q_a, p['norm_q']) @ p['w_uq']).reshape(b, t, MLA_HEADS, QK_NOPE_DIM + QK_ROPE_DIM)
    q_rope = apply_rope(q[..., QK_NOPE_DIM:], pos)
    q_lat = jnp.einsum('bthd,chd->bthc', q[..., :QK_NOPE_DIM], p['w_uk'])
    ckv = rms_norm(kv_a, p['norm_kv'])
    krope = apply_rope(k_r[:, :, None, :], pos)[:, :, 0, :]
    o_lat = attend(q_lat, q_rope, ckv, krope)
    attn = jnp.einsum('bthc,chd->bthd', o_lat, p['w_uv']).reshape(b, t, MLA_WIDTH)
    attn = rms_norm(attn, p['norm_attn_out'])
    y, h_last = s5_scan(u, h0, p['s5_a_re'], p['s5_a_im'], p['s5_log_dt'], p['s5_b_re'], p['s5_b_im'],
                        p['s5_c_re'], p['s5_c_im'], p['s5_d'])
    y = jax.nn.gelu(y)
    y = y * jax.nn.sigmoid(y @ p['w_glu'] + p['b_glu'])
    y = rms_norm(y, p['norm_ssm_out'])
    x = x + g2 * (jnp.concatenate([attn, y], axis=-1) @ p['w_out'])
    x = macaron_ffn(x, sh3, sc3, g3, p['norm_ffn2'], p['ffn2_w1'], p['ffn2_w3'], p['ffn2_w2'])
    return x, ckv, krope, h_last


def setup_inputs(seed: int = 0) -> dict:
    key = jax.random.key(seed)
    keys = iter(jax.random.split(key, 64))
    f32 = jnp.float32
    L = DEPTH
    n_pages = PAST_LEN // PAGE_SIZE
    n_used = DEC_BATCH * n_pages
    n_pool = n_used + n_used // 4

    def nrm(shape, scale=1.0):
        return scale * jax.random.normal(next(keys), shape, f32)

    def gain(shape):
        return 1.0 + nrm(shape, 0.02)

    page_table = jax.random.permutation(next(keys), n_pool)[:n_used].reshape(DEC_BATCH, n_pages).astype(jnp.int32)
    log_dt = jax.random.uniform(next(keys), (L, S5_GROUPS), f32, math.log(DT_MIN), math.log(DT_MAX))
    a_re = -0.5 + nrm((L, S5_GROUPS, S5_STATE), 0.01)
    a_im = math.pi * jnp.arange(S5_STATE, dtype=f32)[None, None, :] + nrm((L, S5_GROUPS, S5_STATE), 0.01)
    return {
        'x_prompt': nrm((BATCH, SEQ, D_MODEL)),
        'x_sample': nrm((DEC_BATCH, DEC_SEQ, D_MODEL)),
        'c_prompt': nrm((BATCH, D_MODEL)),
        'c_sample': nrm((DEC_BATCH, D_MODEL)),
        'cache_ckv': nrm((L, n_pool, PAGE_SIZE, KV_LORA_RANK)),
        'cache_krope': nrm((L, n_pool, PAGE_SIZE, QK_ROPE_DIM)),
        'state_s5_re': nrm((L, DEC_BATCH, S5_GROUPS, S5_STATE), 0.3),
        'state_s5_im': nrm((L, DEC_BATCH, S5_GROUPS, S5_STATE), 0.3),
        'page_table': page_table,
        'w_ada': nrm((L, D_MODEL, ADA_CHUNKS * D_MODEL), D_MODEL ** -0.5),
        'b_ada': nrm((L, ADA_CHUNKS * D_MODEL), 0.01),
        'norm_ffn1': gain((L, D_MODEL)),
        'ffn1_w1': nrm((L, D_MODEL, D_FF), D_MODEL ** -0.5),
        'ffn1_w3': nrm((L, D_MODEL, D_FF), D_MODEL ** -0.5),
        'ffn1_w2': nrm((L, D_FF, D_MODEL), D_FF ** -0.5),
        'norm_mix': gain((L, D_MODEL)),
        'w_in': nrm((L, D_MODEL, IN_COLS), D_MODEL ** -0.5),
        'norm_q': gain((L, Q_LORA_RANK)),
        'w_uq': nrm((L, Q_LORA_RANK, MLA_HEADS * (QK_NOPE_DIM + QK_ROPE_DIM)), Q_LORA_RANK ** -0.5),
        'norm_kv': gain((L, KV_LORA_RANK)),
        'w_uk': nrm((L, KV_LORA_RANK, MLA_HEADS, QK_NOPE_DIM), KV_LORA_RANK ** -0.5),
        'w_uv': nrm((L, KV_LORA_RANK, MLA_HEADS, V_HEAD_DIM), KV_LORA_RANK ** -0.5),
        's5_a_re': a_re,
        's5_a_im': a_im,
        's5_log_dt': log_dt,
        's5_b_re': nrm((L, S5_GROUPS, S5_STATE, S5_GROUP), (2.0 * S5_GROUP) ** -0.5),
        's5_b_im': nrm((L, S5_GROUPS, S5_STATE, S5_GROUP), (2.0 * S5_GROUP) ** -0.5),
        's5_c_re': nrm((L, S5_GROUPS, S5_GROUP, S5_STATE), (2.0 * S5_STATE) ** -0.5),
        's5_c_im': nrm((L, S5_GROUPS, S5_GROUP, S5_STATE), (2.0 * S5_STATE) ** -0.5),
        's5_d': nrm((L, S5_WIDTH), 0.5),
        'w_glu': nrm((L, S5_WIDTH, S5_WIDTH), S5_WIDTH ** -0.5),
        'b_glu': nrm((L, S5_WIDTH), 0.01),
        'norm_attn_out': gain((L, MLA_WIDTH)),
        'norm_ssm_out': gain((L, S5_WIDTH)),
        'w_out': nrm((L, MIX_WIDTH, D_MODEL), MIX_WIDTH ** -0.5),
        'norm_ffn2': gain((L, D_MODEL)),
        'ffn2_w1': nrm((L, D_MODEL, D_FF), D_MODEL ** -0.5),
        'ffn2_w3': nrm((L, D_MODEL, D_FF), D_MODEL ** -0.5),
        'ffn2_w2': nrm((L, D_FF, D_MODEL), D_FF ** -0.5),
        'norm_final': gain((D_MODEL,)),
    }


def reference(x_prompt, x_sample, c_prompt, c_sample, cache_ckv, cache_krope, state_s5_re, state_s5_im,
              page_table, w_ada, b_ada, norm_ffn1, ffn1_w1, ffn1_w3, ffn1_w2, norm_mix, w_in, norm_q, w_uq,
              norm_kv, w_uk, w_uv, s5_a_re, s5_a_im, s5_log_dt, s5_b_re, s5_b_im, s5_c_re, s5_c_im, s5_d,
              w_glu, b_glu, norm_attn_out, norm_ssm_out, w_out, norm_ffn2, ffn2_w1, ffn2_w3, ffn2_w2,
              norm_final):
    bp, seq = x_prompt.shape[:2]
    bs, dec_seq = x_sample.shape[:2]
    n_pages = page_table.shape[1]
    past_len = n_pages * PAGE_SIZE
    pos_prompt = jnp.arange(seq)
    pos_sample = past_len + jnp.arange(dec_seq)
    h0_prompt = jnp.zeros((bp, S5_GROUPS, S5_STATE), jnp.complex64)
    xp, xs = x_prompt, x_sample
    ckv_p, kr_p, ckv_s, kr_s, hre_p, him_p, hre_s, him_s = [], [], [], [], [], [], [], []
    for l in range(DEPTH):
        p = {
            'w_ada': w_ada[l], 'b_ada': b_ada[l],
            'norm_ffn1': norm_ffn1[l], 'ffn1_w1': ffn1_w1[l], 'ffn1_w3': ffn1_w3[l], 'ffn1_w2': ffn1_w2[l],
            'norm_mix': norm_mix[l], 'w_in': w_in[l], 'norm_q': norm_q[l], 'w_uq': w_uq[l],
            'norm_kv': norm_kv[l], 'w_uk': w_uk[l], 'w_uv': w_uv[l],
            's5_a_re': s5_a_re[l], 's5_a_im': s5_a_im[l], 's5_log_dt': s5_log_dt[l],
            's5_b_re': s5_b_re[l], 's5_b_im': s5_b_im[l], 's5_c_re': s5_c_re[l], 's5_c_im': s5_c_im[l],
            's5_d': s5_d[l], 'w_glu': w_glu[l], 'b_glu': b_glu[l],
            'norm_attn_out': norm_attn_out[l], 'norm_ssm_out': norm_ssm_out[l], 'w_out': w_out[l],
            'norm_ffn2': norm_ffn2[l], 'ffn2_w1': ffn2_w1[l], 'ffn2_w3': ffn2_w3[l], 'ffn2_w2': ffn2_w2[l],
        }
        xp, ckv, kr, hl = decoder_layer(xp, c_prompt, pos_prompt, h0_prompt, prompt_attention, p)
        ckv_p.append(ckv)
        kr_p.append(kr)
        hre_p.append(jnp.real(hl).astype(x_prompt.dtype))
        him_p.append(jnp.imag(hl).astype(x_prompt.dtype))
        ckv_past = cache_ckv[l][page_table].reshape(bs, past_len, KV_LORA_RANK)
        krope_past = cache_krope[l][page_table].reshape(bs, past_len, QK_ROPE_DIM)
        h0_sample = lax.complex(state_s5_re[l].astype(jnp.float32), state_s5_im[l].astype(jnp.float32))
        attend_sample = functools.partial(sample_attention, ckv_past, krope_past)
        xs, ckv, kr, hl = decoder_layer(xs, c_sample, pos_sample, h0_sample, attend_sample, p)
        ckv_s.append(ckv)
        kr_s.append(kr)
        hre_s.append(jnp.real(hl).astype(state_s5_re.dtype))
        him_s.append(jnp.imag(hl).astype(state_s5_im.dtype))
    y_prompt = rms_norm(xp, norm_final)
    y_sample = rms_norm(xs, norm_final)
    new_ckv_prompt = jnp.stack(ckv_p)
    new_krope_prompt = jnp.stack(kr_p)
    new_ckv_sample = jnp.stack(ckv_s)
    new_krope_sample = jnp.stack(kr_s)
    s5_re_prompt = jnp.stack(hre_p)
    s5_im_prompt = jnp.stack(him_p)
    s5_re_sample = jnp.stack(hre_s)
    s5_im_sample = jnp.stack(him_s)
    return (y_prompt, y_sample, new_ckv_prompt, new_krope_prompt, new_ckv_sample, new_krope_sample,
            s5_re_prompt, s5_im_prompt, s5_re_sample, s5_im_sample)
```

```python
import functools
import math

import jax
import jax.numpy as jnp
from jax import lax
from jax.experimental import pallas as pl
from jax.experimental.pallas import tpu as pltpu

F32 = jnp.float32
BF16 = jnp.bfloat16

D_MODEL = 2048
D_FF = 5632
MLA_HEADS = 8
QK_NOPE = 128
QK_ROPE = 64
V_HEAD = 128
Q_LORA = 512
KV_LORA = 256
MLA_WIDTH = 1024
S5_WIDTH = 1024
S5_GROUP = 16
S5_GROUPS = 64
S5_STATE = 64
ADA_CHUNKS = 9
PAGE = 128
ROPE_THETA = 10000.0
SOFTMAX_SCALE = (QK_NOPE + QK_ROPE) ** -0.5
EPS = 1e-6

LANES = 128
S5_CHUNK = 16
GROUPS_PER_SLAB = LANES // S5_GROUP
N_SLABS = S5_WIDTH // LANES
SLAB_STATE = GROUPS_PER_SLAB * S5_STATE
VMEM_LIMIT = 56 * 1024 * 1024


def _cparams(sem):
    return pltpu.CompilerParams(dimension_semantics=sem, vmem_limit_bytes=VMEM_LIMIT)


def _rms(x, g):
    return x * lax.rsqrt(jnp.mean(x * x, axis=-1, keepdims=True) + EPS) * g


def _dot(a, b):
    return jnp.dot(a, b, preferred_element_type=F32)


def _dot_nt(a, b):
    return lax.dot_general(a, b, (((1,), (1,)), ((), ())), preferred_element_type=F32)


def _ada_kernel(c_ref, w_ref, b_ref, o_ref):
    c = c_ref[...]
    a = (c * jax.nn.sigmoid(c)).astype(BF16)
    o_ref[...] = _dot(a, w_ref[...].astype(BF16)) + b_ref[...]


def _ada(c, w_ada, b_ada):
    rows = c.shape[0]
    n = w_ada.shape[1]
    tn = 1024
    return pl.pallas_call(
        _ada_kernel,
        grid=(n // tn,),
        in_specs=[
            pl.BlockSpec((rows, D_MODEL), lambda j: (0, 0)),
            pl.BlockSpec((D_MODEL, tn), lambda j: (0, j)),
            pl.BlockSpec((1, tn), lambda j: (0, j)),
        ],
        out_specs=pl.BlockSpec((rows, tn), lambda j: (0, j)),
        out_shape=jax.ShapeDtypeStruct((rows, n), F32),
        compiler_params=_cparams(("arbitrary",)),
        name="ada_modulation",
    )(c, w_ada, b_ada.reshape(1, n))


def _ffn_kernel(x_ref, sh_ref, sc_ref, gt_ref, g_ref, w1_ref, w3_ref, w2_ref, *rest, final_norm):
    if final_norm:
        gf_ref, o_ref, h_ref = rest
    else:
        o_ref, h_ref = rest
    j = pl.program_id(1)

    @pl.when(j == 0)
    def _():
        h = _rms(x_ref[...], g_ref[...]) * (1.0 + sc_ref[0]) + sh_ref[0]
        h_ref[...] = h.astype(BF16)
        o_ref[...] = jnp.zeros_like(o_ref)

    h = h_ref[...]
    a = _dot(h, w1_ref[...])
    b = _dot(h, w3_ref[...])
    act = (a * jax.nn.sigmoid(a) * b).astype(BF16)
    o_ref[...] += _dot(act, w2_ref[...])

    @pl.when(j == pl.num_programs(1) - 1)
    def _():
        y = x_ref[...] + 0.5 * gt_ref[0] * o_ref[...]
        if final_norm:
            y = _rms(y, gf_ref[...])
        o_ref[...] = y


def _ffn(x, mod, chunk0, tiles_per_b, g, w1, w3, w2, gf=None, *, tm, tf=512):
    n = x.shape[0]
    r = mod.shape[1]

    def mod_spec(k):
        return pl.BlockSpec((1, r, D_MODEL), lambda i, j: (i // tiles_per_b, 0, k))

    in_specs = [
        pl.BlockSpec((tm, D_MODEL), lambda i, j: (i, 0)),
        mod_spec(chunk0), mod_spec(chunk0 + 1), mod_spec(chunk0 + 2),
        pl.BlockSpec((1, D_MODEL), lambda i, j: (0, 0)),
        pl.BlockSpec((D_MODEL, tf), lambda i, j: (0, j)),
        pl.BlockSpec((D_MODEL, tf), lambda i, j: (0, j)),
        pl.BlockSpec((tf, D_MODEL), lambda i, j: (j, 0)),
    ]
    args = [x, mod, mod, mod, g.reshape(1, D_MODEL), w1, w3, w2]
    if gf is not None:
        in_specs.append(pl.BlockSpec((1, D_MODEL), lambda i, j: (0, 0)))
        args.append(gf.reshape(1, D_MODEL))
    return pl.pallas_call(
        functools.partial(_ffn_kernel, final_norm=gf is not None),
        grid=(n // tm, D_FF // tf),
        in_specs=in_specs,
        out_specs=pl.BlockSpec((tm, D_MODEL), lambda i, j: (i, 0)),
        out_shape=jax.ShapeDtypeStruct((n, D_MODEL), F32),
        scratch_shapes=[pltpu.VMEM((tm, D_MODEL), BF16)],
        compiler_params=_cparams(("parallel", "arbitrary")),
        name="macaron_ffn",
    )(*args)


def _rope_slab(x, cos, sin_signed):
    lane = lax.broadcasted_iota(jnp.int32, x.shape, 1)
    first_half = (lane % QK_ROPE) < (QK_ROPE // 2)
    partner = jnp.where(first_half, pltpu.roll(x, LANES - QK_ROPE // 2, 1), pltpu.roll(x, QK_ROPE // 2, 1))
    return x * cos + partner * sin_signed


def _mixin_kernel(x_ref, sh_ref, sc_ref, g_ref, wq_ref, wkv_ref, wkr_ref, wu_ref, gq_ref, wqn_ref, wqr_ref,
                  wuk_ref, gkv_ref, cos_ref, sin_ref,
                  qlat_ref, qrope_ref, ckv_ref, kc_ref, krope_ref, kr_ref, u_ref):
    h = (_rms(x_ref[...], g_ref[...]) * (1.0 + sc_ref[0]) + sh_ref[0]).astype(BF16)
    cos = cos_ref[...]
    sin = sin_ref[...]

    u_ref[0] = _dot(h, wu_ref[...])

    ckv = _rms(_dot(h, wkv_ref[...]), gkv_ref[...])
    ckv_ref[0] = ckv
    kc_ref[0] = ckv.astype(BF16)

    kr = _rope_slab(_dot(h, wkr_ref[...]), cos, sin)[:, :QK_ROPE]
    krope_ref[0] = kr
    kr_ref[0] = kr.astype(BF16)

    qn = _rms(_dot(h, wq_ref[...]), gq_ref[...]).astype(BF16)
    q_nope = (_dot(qn, wqn_ref[...]) * SOFTMAX_SCALE).astype(BF16)
    q_rope = _dot(qn, wqr_ref[...]) * SOFTMAX_SCALE
    for hd in range(MLA_HEADS):
        qlat_ref[0, hd] = _dot(q_nope[:, hd * QK_NOPE:(hd + 1) * QK_NOPE], wuk_ref[hd]).astype(BF16)
    for s in range(MLA_HEADS * QK_ROPE // LANES):
        slab = _rope_slab(q_rope[:, s * LANES:(s + 1) * LANES], cos, sin).astype(BF16)
        qrope_ref[0, 2 * s] = slab[:, :QK_ROPE]
        qrope_ref[0, 2 * s + 1] = slab[:, QK_ROPE:]


def _mixin(x, mod, tiles_per_b, t_len, p, cos_t, sin_t, *, tm):
    n = x.shape[0]
    nb = n // t_len
    r = mod.shape[1]
    full = lambda shape: pl.BlockSpec(shape, lambda i: (0,) * len(shape))
    bt = lambda i: (i // tiles_per_b, i % tiles_per_b, 0)
    in_specs = [
        pl.BlockSpec((tm, D_MODEL), lambda i: (i, 0)),
        pl.BlockSpec((1, r, D_MODEL), lambda i: (i // tiles_per_b, 0, 3)),
        pl.BlockSpec((1, r, D_MODEL), lambda i: (i // tiles_per_b, 0, 4)),
        full((1, D_MODEL)),
        full((D_MODEL, Q_LORA)), full((D_MODEL, KV_LORA)), full((D_MODEL, LANES)), full((D_MODEL, S5_WIDTH)),
        full((1, Q_LORA)), full((Q_LORA, MLA_HEADS * QK_NOPE)), full((Q_LORA, MLA_HEADS * QK_ROPE)),
        full((MLA_HEADS, QK_NOPE, KV_LORA)), full((1, KV_LORA)),
        pl.BlockSpec((tm, LANES), lambda i: (i % tiles_per_b, 0)),
        pl.BlockSpec((tm, LANES), lambda i: (i % tiles_per_b, 0)),
    ]
    out_specs = [
        pl.BlockSpec((1, MLA_HEADS, tm, KV_LORA), lambda i: (i // tiles_per_b, 0, i % tiles_per_b, 0)),
        pl.BlockSpec((1, MLA_HEADS, tm, QK_ROPE), lambda i: (i // tiles_per_b, 0, i % tiles_per_b, 0)),
        pl.BlockSpec((1, tm, KV_LORA), bt),
        pl.BlockSpec((1, tm, KV_LORA), bt),
        pl.BlockSpec((1, tm, QK_ROPE), bt),
        pl.BlockSpec((1, tm, QK_ROPE), bt),
        pl.BlockSpec((1, tm, S5_WIDTH), bt),
    ]
    out_shape = [
        jax.ShapeDtypeStruct((nb, MLA_HEADS, t_len, KV_LORA), BF16),
        jax.ShapeDtypeStruct((nb, MLA_HEADS, t_len, QK_ROPE), BF16),
        jax.ShapeDtypeStruct((nb, t_len, KV_LORA), F32),
        jax.ShapeDtypeStruct((nb, t_len, KV_LORA), BF16),
        jax.ShapeDtypeStruct((nb, t_len, QK_ROPE), F32),
        jax.ShapeDtypeStruct((nb, t_len, QK_ROPE), BF16),
        jax.ShapeDtypeStruct((nb, t_len, S5_WIDTH), F32),
    ]
    return pl.pallas_call(
        _mixin_kernel,
        grid=(n // tm,),
        in_specs=in_specs,
        out_specs=out_specs,
        out_shape=out_shape,
        compiler_params=_cparams(("parallel",)),
        name="mixer_in",
    )(x, mod, mod, p["norm_mix"], p["w_q"], p["w_kv"], p["w_kr2"], p["w_u"], p["norm_q"], p["w_uq_nope"],
      p["w_uq_rope"], p["w_uk_t"], p["norm_kv"], cos_t, sin_t)


def _attn_out(o_lat, wuv_ref, g_ref, tq):
    parts = [_dot(o_lat[hd * tq:(hd + 1) * tq].astype(BF16), wuv_ref[hd]) for hd in range(MLA_HEADS)]
    return _rms(jnp.concatenate(parts, axis=1), g_ref[...])


def _prompt_attn_kernel(ql_ref, qr_ref, kc_ref, kr_ref, wuv_ref, g_ref, o_ref, m_ref, l_ref, acc_ref, *, tq):
    qi = pl.program_id(1)
    rows = MLA_HEADS * tq
    ql = ql_ref[0].reshape(rows, KV_LORA)
    qr = qr_ref[0].reshape(rows, QK_ROPE)
    m_ref[...] = jnp.full_like(m_ref, -jnp.inf)
    l_ref[...] = jnp.zeros_like(l_ref)
    acc_ref[...] = jnp.zeros_like(acc_ref)

    def step(ki, masked):
        start = pl.multiple_of(ki * tq, tq)
        kc = kc_ref[0, pl.ds(start, tq), :]
        kr = kr_ref[0, pl.ds(start, tq), :]
        s = _dot_nt(ql, kc) + _dot_nt(qr, kr)
        if masked:
            t_local = lax.broadcasted_iota(jnp.int32, s.shape, 0) & (tq - 1)
            k_local = lax.broadcasted_iota(jnp.int32, s.shape, 1)
            s = jnp.where(k_local <= t_local, s, -jnp.inf)
        m_old = m_ref[...]
        m_new = jnp.maximum(m_old, jnp.max(s, axis=1, keepdims=True))
        alpha = jnp.exp(m_old - m_new)
        pexp = jnp.exp(s - m_new)
        l_ref[...] = alpha * l_ref[...] + jnp.sum(pexp, axis=1, keepdims=True)
        acc_ref[...] = alpha * acc_ref[...] + _dot(pexp.astype(BF16), kc)
        m_ref[...] = m_new

    def body(ki, carry):
        step(ki, False)
        return carry

    lax.fori_loop(0, qi, body, 0)
    step(qi, True)
    o_lat = acc_ref[...] / l_ref[...]
    o_ref[0] = _attn_out(o_lat, wuv_ref, g_ref, tq).astype(BF16)


def _prompt_attn(q_lat, q_rope, kc, kr, w_uv_t, g, *, tq=256):
    nb, _, t_len, _ = q_lat.shape
    rows = MLA_HEADS * tq
    return pl.pallas_call(
        functools.partial(_prompt_attn_kernel, tq=tq),
        grid=(nb, t_len // tq),
        in_specs=[
            pl.BlockSpec((1, MLA_HEADS, tq, KV_LORA), lambda b, i: (b, 0, i, 0)),
            pl.BlockSpec((1, MLA_HEADS, tq, QK_ROPE), lambda b, i: (b, 0, i, 0)),
            pl.BlockSpec((1, t_len, KV_LORA), lambda b, i: (b, 0, 0)),
            pl.BlockSpec((1, t_len, QK_ROPE), lambda b, i: (b, 0, 0)),
            pl.BlockSpec((MLA_HEADS, KV_LORA, V_HEAD), lambda b, i: (0, 0, 0)),
            pl.BlockSpec((1, MLA_WIDTH), lambda b, i: (0, 0)),
        ],
        out_specs=pl.BlockSpec((1, tq, MLA_WIDTH), lambda b, i: (b, i, 0)),
        out_shape=jax.ShapeDtypeStruct((nb, t_len, MLA_WIDTH), BF16),
        scratch_shapes=[pltpu.VMEM((rows, 1), F32), pltpu.VMEM((rows, 1), F32), pltpu.VMEM((rows, KV_LORA), F32)],
        compiler_params=_cparams(("parallel", "parallel")),
        name="prompt_attention",
    )(q_lat, q_rope, kc, kr, w_uv_t, g)


def _sample_attn_kernel(pt_ref, ql_ref, qr_ref, kcn_ref, krn_ref, *rest, pages):
    del pt_ref
    kc_refs = rest[:pages]
    kr_refs = rest[pages:2 * pages]
    o_ref, m_ref, l_ref, acc_ref = rest[2 * pages:]
    s_id = pl.program_id(1)

    @pl.when(s_id == 0)
    def _():
        m_ref[...] = jnp.full_like(m_ref, -jnp.inf)
        l_ref[...] = jnp.zeros_like(l_ref)
        acc_ref[...] = jnp.zeros_like(acc_ref)

    ql = ql_ref[0]
    qr = qr_ref[0]
    kc = jnp.concatenate([r[0, 0].astype(BF16) for r in kc_refs], axis=0)
    kr = jnp.concatenate([r[0, 0].astype(BF16) for r in kr_refs], axis=0)
    s = _dot_nt(ql, kc) + _dot_nt(qr, kr)
    m_old = m_ref[...]
    m_new = jnp.maximum(m_old, jnp.max(s, axis=1, keepdims=True))
    alpha = jnp.exp(m_old - m_new)
    pexp = jnp.exp(s - m_new)
    l_new = alpha * l_ref[...] + jnp.sum(pexp, axis=1, keepdims=True)
    acc_new = alpha * acc_ref[...] + _dot(pexp.astype(BF16), kc)
    m_ref[...] = m_new
    l_ref[...] = l_new
    acc_ref[...] = acc_new

    @pl.when(s_id == pl.num_programs(1) - 1)
    def _():
        kcn = kcn_ref[0].astype(BF16).astype(F32)
        krn = krn_ref[0].astype(BF16).astype(F32)
        s_n = (jnp.sum(ql.astype(F32) * kcn, axis=1, keepdims=True)
               + jnp.sum(qr.astype(F32) * krn, axis=1, keepdims=True))
        m_f = jnp.maximum(m_new, s_n)
        a_f = jnp.exp(m_new - m_f)
        p_n = jnp.exp(s_n - m_f)
        l_f = a_f * l_new + p_n
        acc_f = a_f * acc_new + p_n.astype(BF16).astype(F32) * kcn
        o_ref[0] = acc_f / l_f


def _sample_attn(page_table, q_lat, q_rope, ckv_new, krope_new, cache_ckv, cache_krope, *, pages=16):
    nb, n_pages = page_table.shape

    def cache_spec(width, i):
        return pl.BlockSpec((1, 1, PAGE, width), lambda b, s, pt: (0, pt[b, s * pages + i], 0, 0))

    in_specs = [
        pl.BlockSpec((1, MLA_HEADS, KV_LORA), lambda b, s, pt: (b, 0, 0)),
        pl.BlockSpec((1, MLA_HEADS, QK_ROPE), lambda b, s, pt: (b, 0, 0)),
        pl.BlockSpec((1, 1, KV_LORA), lambda b, s, pt: (b, 0, 0)),
        pl.BlockSpec((1, 1, QK_ROPE), lambda b, s, pt: (b, 0, 0)),
    ]
    in_specs += [cache_spec(KV_LORA, i) for i in range(pages)]
    in_specs += [cache_spec(QK_ROPE, i) for i in range(pages)]
    grid_spec = pltpu.PrefetchScalarGridSpec(
        num_scalar_prefetch=1,
        grid=(nb, n_pages // pages),
        in_specs=in_specs,
        out_specs=pl.BlockSpec((1, MLA_HEADS, KV_LORA), lambda b, s, pt: (b, 0, 0)),
        scratch_shapes=[pltpu.VMEM((MLA_HEADS, 1), F32), pltpu.VMEM((MLA_HEADS, 1), F32),
                        pltpu.VMEM((MLA_HEADS, KV_LORA), F32)],
    )
    return pl.pallas_call(
        functools.partial(_sample_attn_kernel, pages=pages),
        grid_spec=grid_spec,
        out_shape=jax.ShapeDtypeStruct((nb, MLA_HEADS, KV_LORA), F32),
        compiler_params=_cparams(("parallel", "arbitrary")),
        name="sample_paged_attention",
    )(page_table, q_lat, q_rope, ckv_new, krope_new, *([cache_ckv] * pages), *([cache_krope] * pages))


def _sample_attn_out_kernel(o_ref, wuv_ref, g_ref, out_ref):
    nb = o_ref.shape[1]
    out_ref[...] = _attn_out(o_ref[...].reshape(MLA_HEADS * nb, KV_LORA), wuv_ref, g_ref, nb).astype(BF16)


def _sample_attn_out(o_lat_t, w_uv_t, g):
    nb = o_lat_t.shape[1]
    return pl.pallas_call(
        _sample_attn_out_kernel,
        out_shape=jax.ShapeDtypeStruct((nb, MLA_WIDTH), BF16),
        name="sample_attention_out",
    )(o_lat_t, w_uv_t, g)


def _s5_prompt_kernel(u_ref, r_ref, e_ref, f_ref, are_ref, aim_ref, d_ref, y_ref, hre_ref, him_ref,
                      uc_ref, s_ref, hp_ref, yc_ref):
    n_chunks = uc_ref.shape[0]
    blk = 2 * LANES
    n_blk = S5_CHUNK * LANES // blk
    for s in range(S5_CHUNK):
        uc_ref[:, s * LANES:(s + 1) * LANES] = u_ref[0, pl.ds(s, n_chunks, stride=S5_CHUNK), :]
    uc = uc_ref[...]
    ub = uc.astype(BF16)

    s_ref[...] = _dot(ub, e_ref[0])

    a_re = are_ref[0]
    a_im = aim_ref[0]

    def scan(k, carry):
        h_re, h_im = carry
        hp_ref[pl.ds(k, 1), :] = jnp.concatenate([h_re, h_im], axis=1)
        row = s_ref[pl.ds(k, 1), :]
        n_re = a_re * h_re - a_im * h_im + row[:, :SLAB_STATE]
        n_im = a_re * h_im + a_im * h_re + row[:, SLAB_STATE:]
        return n_re, n_im

    zero = jnp.zeros((1, SLAB_STATE), F32)
    h_re, h_im = lax.fori_loop(0, n_chunks, scan, (zero, zero))
    hre_ref[0, 0] = h_re
    him_ref[0, 0] = h_im

    yc_ref[...] = _dot(hp_ref[...].astype(BF16), f_ref[0]) + uc * d_ref[0]
    for tb in range(n_blk):
        yc_ref[:, tb * blk:(tb + 1) * blk] += _dot(ub[:, :(tb + 1) * blk], r_ref[0, (n_blk - 1 - tb) * blk:, :])
    y = jax.nn.gelu(yc_ref[...])
    for t in range(S5_CHUNK):
        y_ref[0, pl.ds(t, n_chunks, stride=S5_CHUNK), :] = y[:, t * LANES:(t + 1) * LANES]


def _s5_prompt(u, ops):
    nb, t_len, _ = u.shape
    n_chunks = t_len // S5_CHUNK
    cw = S5_CHUNK * LANES
    slab = lambda shape: pl.BlockSpec((1,) + shape, lambda j, b: (j,) + (0,) * len(shape))
    y, hre, him = pl.pallas_call(
        _s5_prompt_kernel,
        grid=(N_SLABS, nb),
        in_specs=[
            pl.BlockSpec((1, t_len, LANES), lambda j, b: (b, 0, j)),
            slab((cw, 2 * LANES)), slab((cw, 2 * SLAB_STATE)), slab((2 * SLAB_STATE, cw)),
            slab((1, SLAB_STATE)), slab((1, SLAB_STATE)), slab((1, cw)),
        ],
        out_specs=[
            pl.BlockSpec((1, t_len, LANES), lambda j, b: (b, 0, j)),
            pl.BlockSpec((1, 1, 1, SLAB_STATE), lambda j, b: (j, b, 0, 0)),
            pl.BlockSpec((1, 1, 1, SLAB_STATE), lambda j, b: (j, b, 0, 0)),
        ],
        out_shape=[
            jax.ShapeDtypeStruct((nb, t_len, S5_WIDTH), F32),
            jax.ShapeDtypeStruct((N_SLABS, nb, 1, SLAB_STATE), F32),
            jax.ShapeDtypeStruct((N_SLABS, nb, 1, SLAB_STATE), F32),
        ],
        scratch_shapes=[
            pltpu.VMEM((n_chunks, cw), F32),
            pltpu.VMEM((n_chunks, 2 * SLAB_STATE), F32),
            pltpu.VMEM((n_chunks, 2 * SLAB_STATE), F32),
            pltpu.VMEM((n_chunks, cw), F32),
        ],
        compiler_params=_cparams(("parallel", "parallel")),
        name="s5_prompt",
    )(u, ops["r"], ops["e"], ops["f"], ops["a_re"], ops["a_im"], ops["d_chunk"])

    def to_state(h):
        return h.reshape(N_SLABS, nb, GROUPS_PER_SLAB, S5_STATE).transpose(1, 0, 2, 3).reshape(nb, S5_GROUPS, S5_STATE)

    return y, to_state(hre), to_state(him)


def _split_bf16(x):
    hi = x.astype(BF16)
    return hi, (x - hi.astype(F32)).astype(BF16)


def _s5_sample_kernel(u_ref, h0re_ref, h0im_ref, bd_hi_ref, bd_lo_ref, cd_ref, lre_ref, lim_ref, d_ref,
                      y_ref, hre_ref, him_ref):
    u = u_ref[...]
    u_hi, u_lo = _split_bf16(u)
    bu = _dot(u_hi, bd_hi_ref[0]) + (_dot(u_hi, bd_lo_ref[0]) + _dot(u_lo, bd_hi_ref[0]))
    l_re = lre_ref[0]
    l_im = lim_ref[0]
    h0_re = h0re_ref[...]
    h0_im = h0im_ref[...]
    h_re = l_re * h0_re - l_im * h0_im + bu[:, :SLAB_STATE]
    h_im = l_re * h0_im + l_im * h0_re + bu[:, SLAB_STATE:]
    hre_ref[...] = h_re
    him_ref[...] = h_im
    h = jnp.concatenate([h_re, h_im], axis=1).astype(BF16)
    y_ref[...] = jax.nn.gelu(_dot(h, cd_ref[0]) + u * d_ref[0])


def _s5_sample(u, h0_re, h0_im, ops):
    nb = u.shape[0]
    n_state = S5_GROUPS * S5_STATE
    slab = lambda shape: pl.BlockSpec((1,) + shape, lambda j: (j,) + (0,) * len(shape))
    col = lambda width: pl.BlockSpec((nb, width), lambda j: (0, j))
    y, hre, him = pl.pallas_call(
        _s5_sample_kernel,
        grid=(N_SLABS,),
        in_specs=[
            col(LANES), col(SLAB_STATE), col(SLAB_STATE),
            slab((LANES, 2 * SLAB_STATE)), slab((LANES, 2 * SLAB_STATE)), slab((2 * SLAB_STATE, LANES)),
            slab((1, SLAB_STATE)), slab((1, SLAB_STATE)), slab((1, LANES)),
        ],
        out_specs=[col(LANES), col(SLAB_STATE), col(SLAB_STATE)],
        out_shape=[
            jax.ShapeDtypeStruct((nb, S5_WIDTH), F32),
            jax.ShapeDtypeStruct((nb, n_state), F32),
            jax.ShapeDtypeStruct((nb, n_state), F32),
        ],
        compiler_params=_cparams(("parallel",)),
        name="s5_sample",
    )(u, h0_re.reshape(nb, n_state), h0_im.reshape(nb, n_state), ops["bd_hi"], ops["bd_lo"], ops["cd"],
      ops["l_re"], ops["l_im"], ops["d_slab"])
    return y, hre.reshape(nb, S5_GROUPS, S5_STATE), him.reshape(nb, S5_GROUPS, S5_STATE)


def _mixout_kernel(x_ref, attn_ref, y_ref, gt_ref, wglu_ref, bglu_ref, gs_ref, woa_ref, woy_ref, o_ref):
    y = y_ref[...]
    z = _dot(y.astype(BF16), wglu_ref[...]) + bglu_ref[...]
    yn = _rms(y * jax.nn.sigmoid(z), gs_ref[...]).astype(BF16)
    mix = _dot(attn_ref[...], woa_ref[...]) + _dot(yn, woy_ref[...])
    o_ref[...] = x_ref[...] + gt_ref[0] * mix


def _mixout(x, attn, y, mod, tiles_per_b, p, *, tm):
    n = x.shape[0]
    r = mod.shape[1]
    full = lambda shape: pl.BlockSpec(shape, lambda i: (0,) * len(shape))
    return pl.pallas_call(
        _mixout_kernel,
        grid=(n // tm,),
        in_specs=[
            pl.BlockSpec((tm, D_MODEL), lambda i: (i, 0)),
            pl.BlockSpec((tm, MLA_WIDTH), lambda i: (i, 0)),
            pl.BlockSpec((tm, S5_WIDTH), lambda i: (i, 0)),
            pl.BlockSpec((1, r, D_MODEL), lambda i: (i // tiles_per_b, 0, 5)),
            full((S5_WIDTH, S5_WIDTH)), full((1, S5_WIDTH)), full((1, S5_WIDTH)),
            full((MLA_WIDTH, D_MODEL)), full((S5_WIDTH, D_MODEL)),
        ],
        out_specs=pl.BlockSpec((tm, D_MODEL), lambda i: (i, 0)),
        out_shape=jax.ShapeDtypeStruct((n, D_MODEL), F32),
        compiler_params=_cparams(("parallel",)),
        name="mixer_out",
    )(x, attn, y, mod, p["w_glu"], p["b_glu"], p["norm_ssm_out"], p["w_out_attn"], p["w_out_ssm"])


def _rope_tables(pos):
    half = QK_ROPE // 2
    inv_freq = ROPE_THETA ** (-jnp.arange(half, dtype=F32) / half)
    ang = pos.astype(F32)[:, None] * inv_freq[None, :]
    cos = jnp.cos(ang)
    sin = jnp.sin(ang)
    cos_t = jnp.tile(jnp.concatenate([cos, cos], axis=1), (1, LANES // QK_ROPE))
    sin_t = jnp.tile(jnp.concatenate([-sin, sin], axis=1), (1, LANES // QK_ROPE))
    return cos_t, sin_t


def _group_diag(x, g_axis, new_axis):
    x = jnp.expand_dims(x, new_axis)
    shape = [1] * x.ndim
    shape[g_axis if g_axis < new_axis else g_axis + 1] = GROUPS_PER_SLAB
    shape[new_axis] = GROUPS_PER_SLAB
    return x * jnp.eye(GROUPS_PER_SLAB, dtype=x.dtype).reshape(shape)


def _by_slab(x, g_axis):
    return x.reshape(x.shape[:g_axis] + (N_SLABS, GROUPS_PER_SLAB) + x.shape[g_axis + 1:])


def _s5_operators(a_re, a_im, log_dt, b_re, b_im, c_re, c_im, d_skip):
    hp = lax.Precision.HIGHEST
    lam = lax.complex(a_re.astype(F32), a_im.astype(F32))
    dt = jnp.exp(log_dt.astype(F32))[:, None]
    z = lam * dt
    lam_bar = jnp.exp(z)
    b_bar = ((lam_bar - 1.0) / lam)[:, :, None] * lax.complex(b_re.astype(F32), b_im.astype(F32))
    c_mat = lax.complex(c_re.astype(F32), c_im.astype(F32))
    steps = jnp.arange(S5_CHUNK + 1, dtype=F32)
    pw = jnp.exp(z[None] * steps[:, None, None])

    cp = c_mat[None] * pw[:S5_CHUNK, :, None, :]
    m = (jnp.einsum("ngcp,gpd->ngcd", jnp.real(cp), jnp.real(b_bar), precision=hp)
         - jnp.einsum("ngcp,gpd->ngcd", jnp.imag(cp), jnp.imag(b_bar), precision=hp))
    m_ext = jnp.concatenate([m, jnp.zeros_like(m[:1])], axis=0)
    n_blk = S5_CHUNK // 2
    d_i = (n_blk - 1 - jnp.arange(n_blk))[:, None, None]
    s_i = jnp.arange(2)[None, :, None]
    t_i = jnp.arange(2)[None, None, :]
    lag = 2 * d_i + t_i - s_i
    mg = _by_slab(m_ext[lag], 3)
    mg = mg.transpose(3, 0, 1, 4, 6, 2, 5)
    tb = _group_diag(mg, 3, 6)
    r_op = tb.reshape(N_SLABS, n_blk * 2 * LANES, 2 * LANES)

    pw_rev = jnp.exp(z[None] * (S5_CHUNK - 1 - steps[:S5_CHUNK])[:, None, None])
    w = pw_rev[:, :, :, None] * b_bar[None]
    w = _by_slab(w, 1).transpose(1, 0, 2, 4, 3)

    def e_half(v):
        return _group_diag(v, 2, 4).reshape(N_SLABS, S5_CHUNK * LANES, SLAB_STATE)

    e_op = jnp.concatenate([e_half(jnp.real(w)), e_half(jnp.imag(w))], axis=2)

    g_op = c_mat[None] * pw[1:, :, None, :]
    g_op = _by_slab(g_op, 1).transpose(1, 2, 4, 0, 3)

    def f_half(v):
        return _group_diag(v, 1, 4).reshape(N_SLABS, SLAB_STATE, S5_CHUNK * LANES)

    f_op = jnp.concatenate([f_half(jnp.real(g_op)), f_half(-jnp.imag(g_op))], axis=1)

    def slab_vec(v):
        return v.reshape(N_SLABS, 1, SLAB_STATE)

    a_chunk = pw[S5_CHUNK]
    d_slab = d_skip.astype(F32).reshape(N_SLABS, 1, LANES)

    bt = _by_slab(b_bar, 0).transpose(0, 1, 3, 2)

    def bd_half(v):
        return _group_diag(v, 1, 3).reshape(N_SLABS, LANES, SLAB_STATE)

    bd = jnp.concatenate([bd_half(jnp.real(bt)), bd_half(jnp.imag(bt))], axis=2)
    ct = _by_slab(c_mat, 0).transpose(0, 1, 3, 2)

    def cd_half(v):
        return _group_diag(v, 1, 3).reshape(N_SLABS, SLAB_STATE, LANES)

    cd = jnp.concatenate([cd_half(jnp.real(ct)), cd_half(-jnp.imag(ct))], axis=1)
    bd_hi = bd.astype(BF16)
    bd_lo = (bd - bd_hi.astype(F32)).astype(BF16)
    return {
        "r": r_op.astype(BF16), "e": e_op.astype(BF16), "f": f_op.astype(BF16),
        "a_re": slab_vec(jnp.real(a_chunk)), "a_im": slab_vec(jnp.imag(a_chunk)),
        "d_chunk": jnp.tile(d_slab, (1, 1, S5_CHUNK)), "d_slab": d_slab,
        "bd_hi": bd_hi, "bd_lo": bd_lo, "cd": cd.astype(BF16),
        "l_re": slab_vec(jnp.real(lam_bar)), "l_im": slab_vec(jnp.imag(lam_bar)),
    }


def _layer_params(w_in, w_uq, w_uk, w_uv, w_glu, w_out, norm_mix, norm_q, norm_kv, norm_attn_out, norm_ssm_out,
                  b_glu):
    c0, c1, c2 = Q_LORA, Q_LORA + KV_LORA, Q_LORA + KV_LORA + QK_ROPE
    w_kr = w_in[:, c1:c2]
    w_uq_h = w_uq.reshape(Q_LORA, MLA_HEADS, QK_NOPE + QK_ROPE)
    return {
        "w_q": w_in[:, :c0].astype(BF16),
        "w_kv": w_in[:, c0:c1].astype(BF16),
        "w_kr2": jnp.concatenate([w_kr, w_kr], axis=1).astype(BF16),
        "w_u": w_in[:, c2:].astype(BF16),
        "w_uq_nope": w_uq_h[:, :, :QK_NOPE].reshape(Q_LORA, MLA_HEADS * QK_NOPE).astype(BF16),
        "w_uq_rope": w_uq_h[:, :, QK_NOPE:].reshape(Q_LORA, MLA_HEADS * QK_ROPE).astype(BF16),
        "w_uk_t": w_uk.transpose(1, 2, 0).astype(BF16),
        "w_uv_t": w_uv.transpose(1, 0, 2).astype(BF16),
        "w_glu": w_glu.astype(BF16),
        "w_out_attn": w_out[:MLA_WIDTH].astype(BF16),
        "w_out_ssm": w_out[MLA_WIDTH:].astype(BF16),
        "norm_mix": norm_mix.reshape(1, D_MODEL),
        "norm_q": norm_q.reshape(1, Q_LORA),
        "norm_kv": norm_kv.reshape(1, KV_LORA),
        "norm_attn_out": norm_attn_out.reshape(1, MLA_WIDTH),
        "norm_ssm_out": norm_ssm_out.reshape(1, S5_WIDTH),
        "b_glu": b_glu.reshape(1, S5_WIDTH),
    }


def kernel(x_prompt, x_sample, c_prompt, c_sample, cache_ckv, cache_krope, state_s5_re, state_s5_im, page_table, w_ada, b_ada, norm_ffn1, ffn1_w1, ffn1_w3, ffn1_w2, norm_mix, w_in, norm_q, w_uq, norm_kv, w_uk, w_uv, s5_a_re, s5_a_im, s5_log_dt, s5_b_re, s5_b_im, s5_c_re, s5_c_im, s5_d, w_glu, b_glu, norm_attn_out, norm_ssm_out, w_out, norm_ffn2, ffn2_w1, ffn2_w3, ffn2_w2, norm_final):
    bp, seq, _ = x_prompt.shape
    bs = x_sample.shape[0]
    depth = w_ada.shape[0]
    assert depth == 1 and x_sample.shape[1] == 1
    n_pages = page_table.shape[1]
    past_len = n_pages * PAGE
    l = 0

    pad = (-(bs + bp)) % 8
    c_all = jnp.concatenate([c_sample, c_prompt, jnp.zeros((pad, D_MODEL), F32)], axis=0)
    mod = _ada(c_all, w_ada[l], b_ada[l])
    mod_s = mod[:bs].reshape(1, bs, ADA_CHUNKS * D_MODEL)
    mod_p = mod[bs:bs + bp].reshape(bp, 1, ADA_CHUNKS * D_MODEL)

    p = _layer_params(w_in[l], w_uq[l], w_uk[l], w_uv[l], w_glu[l], w_out[l], norm_mix[l], norm_q[l], norm_kv[l],
                      norm_attn_out[l], norm_ssm_out[l], b_glu[l])
    ops = _s5_operators(s5_a_re[l], s5_a_im[l], s5_log_dt[l], s5_b_re[l], s5_b_im[l], s5_c_re[l], s5_c_im[l],
                        s5_d[l])
    f1 = (ffn1_w1[l].astype(BF16), ffn1_w3[l].astype(BF16), ffn1_w2[l].astype(BF16))
    f2 = (ffn2_w1[l].astype(BF16), ffn2_w3[l].astype(BF16), ffn2_w2[l].astype(BF16))
    cos_p, sin_p = _rope_tables(jnp.arange(seq))
    cos_s, sin_s = _rope_tables(jnp.full((bs,), past_len))

    tm_p = 512
    xp = x_prompt.reshape(bp * seq, D_MODEL)
    xp = _ffn(xp, mod_p, 0, seq // tm_p, norm_ffn1[l], *f1, tm=tm_p)
    tm_mix = 512
    q_lat, q_rope, ckv_p, kc_p, krope_p, kr_p, u_p = _mixin(xp, mod_p, seq // tm_mix, seq, p, cos_p, sin_p, tm=tm_mix)
    attn_p = _prompt_attn(q_lat, q_rope, kc_p, kr_p, p["w_uv_t"], p["norm_attn_out"])
    y_p, hre_p, him_p = _s5_prompt(u_p, ops)
    xp = _mixout(xp, attn_p.reshape(bp * seq, MLA_WIDTH), y_p.reshape(bp * seq, S5_WIDTH), mod_p, seq // tm_mix, p,
                 tm=tm_mix)
    y_prompt = _ffn(xp, mod_p, 6, seq // tm_p, norm_ffn2[l], *f2, norm_final, tm=tm_p).reshape(bp, seq, D_MODEL)

    xs = x_sample.reshape(bs, D_MODEL)
    xs = _ffn(xs, mod_s, 0, 1, norm_ffn1[l], *f1, tm=bs)
    q_lat_s, q_rope_s, ckv_s, _, krope_s, _, u_s = _mixin(xs, mod_s, 1, bs, p, cos_s, sin_s, tm=bs)
    o_lat = _sample_attn(page_table, q_lat_s[0].transpose(1, 0, 2), q_rope_s[0].transpose(1, 0, 2),
                         ckv_s.reshape(bs, 1, KV_LORA), krope_s.reshape(bs, 1, QK_ROPE), cache_ckv[l:l + 1],
                         cache_krope[l:l + 1])
    attn_s = _sample_attn_out(o_lat.transpose(1, 0, 2), p["w_uv_t"], p["norm_attn_out"])
    y_s, hre_s, him_s = _s5_sample(u_s.reshape(bs, S5_WIDTH), state_s5_re[l], state_s5_im[l], ops)
    xs = _mixout(xs, attn_s, y_s, mod_s, 1, p, tm=bs)
    y_sample = _ffn(xs, mod_s, 6, 1, norm_ffn2[l], *f2, norm_final, tm=bs).reshape(bs, 1, D_MODEL)

    return (y_prompt, y_sample,
            ckv_p[None], krope_p[None],
            ckv_s.reshape(1, bs, 1, KV_LORA), krope_s.reshape(1, bs, 1, QK_ROPE),
            hre_p[None], him_p[None], hre_s[None], him_s[None])
```

```python
import functools

import jax
import jax.numpy as jnp
from jax import lax
from jax.experimental import pallas as pl
from jax.experimental.pallas import tpu as pltpu

F32 = jnp.float32
BF16 = jnp.bfloat16

D_MODEL = 2048
D_FF = 5632
MLA_HEADS = 8
QK_NOPE = 128
QK_ROPE = 64
V_HEAD = 128
Q_LORA = 512
KV_LORA = 256
MLA_WIDTH = 1024
S5_WIDTH = 1024
S5_GROUP = 16
S5_GROUPS = 64
S5_STATE = 64
ADA_CHUNKS = 9
PAGE = 128
ROPE_THETA = 10000.0
SOFTMAX_SCALE = (QK_NOPE + QK_ROPE) ** -0.5
EPS = 1e-6

LANES = 128
S5_CHUNK = 16
GROUPS_PER_SLAB = LANES // S5_GROUP
N_SLABS = S5_WIDTH // LANES
SLAB_STATE = GROUPS_PER_SLAB * S5_STATE
CHUNK_COLS = S5_CHUNK * LANES
MXU_TILE = 2 * LANES
VMEM_LIMIT = 56 * 1024 * 1024


def _cparams(sem):
    return pltpu.CompilerParams(dimension_semantics=sem, vmem_limit_bytes=VMEM_LIMIT)


def _rms(x, g):
    return x * lax.rsqrt(jnp.mean(x * x, axis=-1, keepdims=True) + EPS) * g


def _rms_rows(x, g):
    return x * lax.rsqrt(jnp.mean(x * x, axis=0, keepdims=True) + EPS) * g


def _dot(a, b):
    return jnp.dot(a, b, preferred_element_type=F32)


def _dot_nt(a, b):
    return lax.dot_general(a, b, (((1,), (1,)), ((), ())), preferred_element_type=F32)


def _ada_kernel(c_ref, w_ref, b_ref, o_ref):
    c = c_ref[...]
    a = (c * jax.nn.sigmoid(c)).astype(BF16)
    o_ref[...] = _dot(a, w_ref[...].astype(BF16)) + b_ref[...]


def _ada(c, w_ada, b_ada):
    rows = c.shape[0]
    n = w_ada.shape[1]
    tn = 1024
    return pl.pallas_call(
        _ada_kernel,
        grid=(n // tn,),
        in_specs=[
            pl.BlockSpec((rows, D_MODEL), lambda j: (0, 0)),
            pl.BlockSpec((D_MODEL, tn), lambda j: (0, j)),
            pl.BlockSpec((1, tn), lambda j: (0, j)),
        ],
        out_specs=pl.BlockSpec((rows, tn), lambda j: (0, j)),
        out_shape=jax.ShapeDtypeStruct((rows, n), F32),
        compiler_params=_cparams(("arbitrary",)),
        name="ada_modulation",
    )(c, w_ada, b_ada.reshape(1, n))


def _ffn_kernel(x_ref, sh_ref, sc_ref, gt_ref, g_ref, w1_ref, w3_ref, w2_ref, *rest, final_norm):
    if final_norm:
        gf_ref, o_ref, h_ref = rest
    else:
        o_ref, h_ref = rest
    j = pl.program_id(1)

    @pl.when(j == 0)
    def _():
        h = _rms(x_ref[...], g_ref[...]) * (1.0 + sc_ref[0]) + sh_ref[0]
        h_ref[...] = h.astype(BF16)
        o_ref[...] = jnp.zeros_like(o_ref)

    h = h_ref[...]
    a = _dot(h, w1_ref[...])
    b = _dot(h, w3_ref[...])
    act = (a * jax.nn.sigmoid(a) * b).astype(BF16)
    o_ref[...] += _dot(act, w2_ref[...])

    @pl.when(j == pl.num_programs(1) - 1)
    def _():
        y = x_ref[...] + 0.5 * gt_ref[0] * o_ref[...]
        if final_norm:
            y = _rms(y, gf_ref[...])
        o_ref[...] = y


def _ffn(x, mod, chunk0, tiles_per_b, g, w1, w3, w2, gf=None, *, tm, tf=512):
    n = x.shape[0]
    r = mod.shape[1]

    def mod_spec(k):
        return pl.BlockSpec((1, r, D_MODEL), lambda i, j: (i // tiles_per_b, 0, k))

    in_specs = [
        pl.BlockSpec((tm, D_MODEL), lambda i, j: (i, 0)),
        mod_spec(chunk0), mod_spec(chunk0 + 1), mod_spec(chunk0 + 2),
        pl.BlockSpec((1, D_MODEL), lambda i, j: (0, 0)),
        pl.BlockSpec((D_MODEL, tf), lambda i, j: (0, j)),
        pl.BlockSpec((D_MODEL, tf), lambda i, j: (0, j)),
        pl.BlockSpec((tf, D_MODEL), lambda i, j: (j, 0)),
    ]
    args = [x, mod, mod, mod, g.reshape(1, D_MODEL), w1, w3, w2]
    if gf is not None:
        in_specs.append(pl.BlockSpec((1, D_MODEL), lambda i, j: (0, 0)))
        args.append(gf.reshape(1, D_MODEL))
    return pl.pallas_call(
        functools.partial(_ffn_kernel, final_norm=gf is not None),
        grid=(n // tm, D_FF // tf),
        in_specs=in_specs,
        out_specs=pl.BlockSpec((tm, D_MODEL), lambda i, j: (i, 0)),
        out_shape=jax.ShapeDtypeStruct((n, D_MODEL), F32),
        scratch_shapes=[pltpu.VMEM((tm, D_MODEL), BF16)],
        compiler_params=_cparams(("parallel", "arbitrary")),
        name="macaron_ffn",
    )(*args)


def _rope_rows(x, cos, sin):
    half = QK_ROPE // 2
    x1 = x[:half]
    x2 = x[half:]
    return x1 * cos - x2 * sin, x1 * sin + x2 * cos


def _mixin_kernel(x_ref, sh_ref, sc_ref, g_ref, wq_ref, wkv_ref, wkr_ref, wu_ref, gq_ref, wqn_ref, wqr_ref,
                  wuk_ref, gkv_ref, cos_ref, sin_ref,
                  qlat_ref, qrope_ref, ckv_ref, kc_ref, kct_ref, krope_ref, kr_ref, u_ref, *, tq):
    half = QK_ROPE // 2
    n_q = x_ref.shape[0] // tq
    h = (_rms(x_ref[...], g_ref[...]) * (1.0 + sc_ref[0]) + sh_ref[0]).astype(BF16)
    cos = cos_ref[...]
    sin = sin_ref[...]

    u_ref[0] = _dot(h, wu_ref[...])

    ckv = _rms(_dot(h, wkv_ref[...]), gkv_ref[...])
    ckv_ref[0] = ckv
    kc_ref[0] = ckv.astype(BF16)
    kct_ref[0] = ckv.T.astype(BF16)

    k1, k2 = _rope_rows(_dot_nt(wkr_ref[...], h), cos, sin)
    kr_t = jnp.concatenate([k1, k2], axis=0)
    krope_ref[0] = kr_t
    kr_ref[0] = kr_t.T.astype(BF16)

    qn = _rms(_dot(h, wq_ref[...]), gq_ref[...]).astype(BF16)
    q_nope = (_dot(qn, wqn_ref[...]) * SOFTMAX_SCALE).astype(BF16)
    q_rope_t = _dot_nt(wqr_ref[...], qn) * SOFTMAX_SCALE
    for hd in range(MLA_HEADS):
        ql_t = _dot_nt(wuk_ref[hd], q_nope[:, hd * QK_NOPE:(hd + 1) * QK_NOPE]).astype(BF16)
        r1, r2 = _rope_rows(q_rope_t[hd * QK_ROPE:(hd + 1) * QK_ROPE], cos, sin)
        r1 = r1.astype(BF16)
        r2 = r2.astype(BF16)
        for qq in range(n_q):
            src = slice(qq * tq, (qq + 1) * tq)
            dst = slice(hd * tq, (hd + 1) * tq)
            qlat_ref[0, qq, :, dst] = ql_t[:, src]
            qrope_ref[0, qq, :half, dst] = r1[:, src]
            qrope_ref[0, qq, half:, dst] = r2[:, src]


def _mixin(x, mod, tiles_per_b, t_len, p, cos_t, sin_t, *, tm, tq):
    n = x.shape[0]
    nb = n // t_len
    r = mod.shape[1]
    half = QK_ROPE // 2
    full = lambda shape: pl.BlockSpec(shape, lambda i: (0,) * len(shape))
    rows = lambda width: pl.BlockSpec((1, tm, width), lambda i: (i // tiles_per_b, i % tiles_per_b, 0))
    cols = lambda height: pl.BlockSpec((1, height, tm), lambda i: (i // tiles_per_b, 0, i % tiles_per_b))
    head_cols = lambda height: pl.BlockSpec((1, tm // tq, height, MLA_HEADS * tq),
                                            lambda i: (i // tiles_per_b, i % tiles_per_b, 0, 0))
    in_specs = [
        pl.BlockSpec((tm, D_MODEL), lambda i: (i, 0)),
        pl.BlockSpec((1, r, D_MODEL), lambda i: (i // tiles_per_b, 0, 3)),
        pl.BlockSpec((1, r, D_MODEL), lambda i: (i // tiles_per_b, 0, 4)),
        full((1, D_MODEL)),
        full((D_MODEL, Q_LORA)), full((D_MODEL, KV_LORA)), full((QK_ROPE, D_MODEL)), full((D_MODEL, S5_WIDTH)),
        full((1, Q_LORA)), full((Q_LORA, MLA_HEADS * QK_NOPE)), full((MLA_HEADS * QK_ROPE, Q_LORA)),
        full((MLA_HEADS, KV_LORA, QK_NOPE)), full((1, KV_LORA)),
        pl.BlockSpec((half, tm), lambda i: (0, i % tiles_per_b)),
        pl.BlockSpec((half, tm), lambda i: (0, i % tiles_per_b)),
    ]
    out_specs = [head_cols(KV_LORA), head_cols(QK_ROPE), rows(KV_LORA), rows(KV_LORA), cols(KV_LORA),
                 cols(QK_ROPE), rows(QK_ROPE), rows(S5_WIDTH)]
    out_shape = [
        jax.ShapeDtypeStruct((nb, t_len // tq, KV_LORA, MLA_HEADS * tq), BF16),
        jax.ShapeDtypeStruct((nb, t_len // tq, QK_ROPE, MLA_HEADS * tq), BF16),
        jax.ShapeDtypeStruct((nb, t_len, KV_LORA), F32),
        jax.ShapeDtypeStruct((nb, t_len, KV_LORA), BF16),
        jax.ShapeDtypeStruct((nb, KV_LORA, t_len), BF16),
        jax.ShapeDtypeStruct((nb, QK_ROPE, t_len), F32),
        jax.ShapeDtypeStruct((nb, t_len, QK_ROPE), BF16),
        jax.ShapeDtypeStruct((nb, t_len, S5_WIDTH), F32),
    ]
    return pl.pallas_call(
        functools.partial(_mixin_kernel, tq=tq),
        grid=(n // tm,),
        in_specs=in_specs,
        out_specs=out_specs,
        out_shape=out_shape,
        compiler_params=_cparams(("parallel",)),
        name="mixer_in",
    )(x, mod, mod, p["norm_mix"], p["w_q"], p["w_kv"], p["w_kr_t"], p["w_u"], p["norm_q"], p["w_uq_nope"],
      p["w_uq_rope_t"], p["w_uk_h"], p["norm_kv"], cos_t, sin_t)


def _prompt_attn_kernel(ql_ref, qr_ref, kc_ref, kr_ref, kct_ref, wuv_ref, g_ref, o_ref, m_ref, l_ref, acc_ref, *,
                        tq):
    qi = pl.program_id(1)
    width = m_ref.shape[2]
    heads_per_half = width // tq
    n_half = m_ref.shape[0]
    m_ref[...] = jnp.full_like(m_ref, -jnp.inf)
    l_ref[...] = jnp.zeros_like(l_ref)
    acc_ref[...] = jnp.zeros_like(acc_ref)

    def step(ki, masked):
        start = pl.multiple_of(ki * tq, tq)
        kc = kc_ref[0, pl.ds(start, tq), :]
        kr = kr_ref[0, pl.ds(start, tq), :]
        kct = kct_ref[0, :, pl.ds(start, tq)]
        if masked:
            k_local = lax.broadcasted_iota(jnp.int32, (tq, width), 0)
            t_local = lax.broadcasted_iota(jnp.int32, (tq, width), 1) & (tq - 1)
            keep = k_local <= t_local
        for hf in range(n_half):
            lanes = slice(hf * width, (hf + 1) * width)
            s = _dot(kc, ql_ref[0, 0, :, lanes]) + _dot(kr, qr_ref[0, 0, :, lanes])
            if masked:
                s = jnp.where(keep, s, -jnp.inf)
            m_old = m_ref[hf]
            m_new = jnp.maximum(m_old, jnp.max(s, axis=0, keepdims=True))
            alpha = jnp.exp(m_old - m_new)
            pexp = jnp.exp(s - m_new)
            l_ref[hf] = alpha * l_ref[hf] + jnp.sum(pexp, axis=0, keepdims=True)
            acc_ref[hf] = alpha * acc_ref[hf] + _dot(kct, pexp.astype(BF16))
            m_ref[hf] = m_new

    def body(ki, carry):
        step(ki, False)
        return carry

    lax.fori_loop(0, qi, body, 0)
    step(qi, True)
    parts = []
    for hd in range(MLA_HEADS):
        hf = hd // heads_per_half
        lanes = slice((hd % heads_per_half) * tq, (hd % heads_per_half + 1) * tq)
        o_t = acc_ref[hf, :, lanes] / l_ref[hf, :, lanes]
        parts.append(_dot(wuv_ref[hd], o_t.astype(BF16)))
    attn_t = _rms_rows(jnp.concatenate(parts, axis=0), g_ref[...])
    o_ref[0] = attn_t.T.astype(BF16)


def _prompt_attn(q_lat_t, q_rope_t, kc, kr, kc_t, w_uv_h, g_col, *, tq=256, n_half=2):
    nb, n_q, _, _ = q_lat_t.shape
    t_len = n_q * tq
    width = MLA_HEADS * tq // n_half
    return pl.pallas_call(
        functools.partial(_prompt_attn_kernel, tq=tq),
        grid=(nb, n_q),
        in_specs=[
            pl.BlockSpec((1, 1, KV_LORA, MLA_HEADS * tq), lambda b, i: (b, i, 0, 0)),
            pl.BlockSpec((1, 1, QK_ROPE, MLA_HEADS * tq), lambda b, i: (b, i, 0, 0)),
            pl.BlockSpec((1, t_len, KV_LORA), lambda b, i: (b, 0, 0)),
            pl.BlockSpec((1, t_len, QK_ROPE), lambda b, i: (b, 0, 0)),
            pl.BlockSpec((1, KV_LORA, t_len), lambda b, i: (b, 0, 0)),
            pl.BlockSpec((MLA_HEADS, V_HEAD, KV_LORA), lambda b, i: (0, 0, 0)),
            pl.BlockSpec((MLA_WIDTH, 1), lambda b, i: (0, 0)),
        ],
        out_specs=pl.BlockSpec((1, tq, MLA_WIDTH), lambda b, i: (b, i, 0)),
        out_shape=jax.ShapeDtypeStruct((nb, t_len, MLA_WIDTH), BF16),
        scratch_shapes=[pltpu.VMEM((n_half, 1, width), F32), pltpu.VMEM((n_half, 1, width), F32),
                        pltpu.VMEM((n_half, KV_LORA, width), F32)],
        compiler_params=_cparams(("parallel", "parallel")),
        name="prompt_attention",
    )(q_lat_t, q_rope_t, kc, kr, kc_t, w_uv_h, g_col)


def _sample_attn_kernel(pt_ref, ql_ref, qr_ref, kcn_ref, krn_ref, *rest, pages):
    del pt_ref
    kc_refs = rest[:pages]
    kr_refs = rest[pages:2 * pages]
    o_ref, m_ref, l_ref, acc_ref = rest[2 * pages:]
    s_id = pl.program_id(1)

    @pl.when(s_id == 0)
    def _():
        m_ref[...] = jnp.full_like(m_ref, -jnp.inf)
        l_ref[...] = jnp.zeros_like(l_ref)
        acc_ref[...] = jnp.zeros_like(acc_ref)

    ql = ql_ref[0]
    qr = qr_ref[0]
    kc = jnp.concatenate([r[0, 0].astype(BF16) for r in kc_refs], axis=0)
    kr_t = jnp.concatenate([r[0, 0].astype(BF16) for r in kr_refs], axis=1)
    s = _dot_nt(ql, kc) + _dot(qr, kr_t)
    m_old = m_ref[...]
    m_new = jnp.maximum(m_old, jnp.max(s, axis=1, keepdims=True))
    alpha = jnp.exp(m_old - m_new)
    pexp = jnp.exp(s - m_new)
    l_new = alpha * l_ref[...] + jnp.sum(pexp, axis=1, keepdims=True)
    acc_new = alpha * acc_ref[...] + _dot(pexp.astype(BF16), kc)
    m_ref[...] = m_new
    l_ref[...] = l_new
    acc_ref[...] = acc_new

    @pl.when(s_id == pl.num_programs(1) - 1)
    def _():
        kcn = kcn_ref[0].astype(BF16).astype(F32)
        krn = krn_ref[0].astype(BF16).astype(F32)
        s_n = (jnp.sum(ql.astype(F32) * kcn, axis=1, keepdims=True)
               + jnp.sum(qr.astype(F32) * krn, axis=1, keepdims=True))
        m_f = jnp.maximum(m_new, s_n)
        a_f = jnp.exp(m_new - m_f)
        p_n = jnp.exp(s_n - m_f)
        l_f = a_f * l_new + p_n
        acc_f = a_f * acc_new + p_n.astype(BF16).astype(F32) * kcn
        o_ref[0] = acc_f / l_f


def _sample_attn(page_table, q_lat, q_rope, ckv_new, krope_new, cache_ckv, cache_krope_t, *, pages=16):
    nb, n_pages = page_table.shape

    def cache_spec(shape, i):
        return pl.BlockSpec((1, 1) + shape, lambda b, s, pt: (0, pt[b, s * pages + i], 0, 0))

    in_specs = [
        pl.BlockSpec((1, MLA_HEADS, KV_LORA), lambda b, s, pt: (b, 0, 0)),
        pl.BlockSpec((1, MLA_HEADS, QK_ROPE), lambda b, s, pt: (b, 0, 0)),
        pl.BlockSpec((1, 1, KV_LORA), lambda b, s, pt: (b, 0, 0)),
        pl.BlockSpec((1, 1, QK_ROPE), lambda b, s, pt: (b, 0, 0)),
    ]
    in_specs += [cache_spec((PAGE, KV_LORA), i) for i in range(pages)]
    in_specs += [cache_spec((QK_ROPE, PAGE), i) for i in range(pages)]
    grid_spec = pltpu.PrefetchScalarGridSpec(
        num_scalar_prefetch=1,
        grid=(nb, n_pages // pages),
        in_specs=in_specs,
        out_specs=pl.BlockSpec((1, MLA_HEADS, KV_LORA), lambda b, s, pt: (b, 0, 0)),
        scratch_shapes=[pltpu.VMEM((MLA_HEADS, 1), F32), pltpu.VMEM((MLA_HEADS, 1), F32),
                        pltpu.VMEM((MLA_HEADS, KV_LORA), F32)],
    )
    return pl.pallas_call(
        functools.partial(_sample_attn_kernel, pages=pages),
        grid_spec=grid_spec,
        out_shape=jax.ShapeDtypeStruct((nb, MLA_HEADS, KV_LORA), F32),
        compiler_params=_cparams(("parallel", "arbitrary")),
        name="sample_paged_attention",
    )(page_table, q_lat, q_rope, ckv_new, krope_new, *([cache_ckv] * pages), *([cache_krope_t] * pages))


def _sample_attn_out_kernel(o_ref, wuv_ref, g_ref, out_ref):
    parts = [_dot_nt(o_ref[hd].astype(BF16), wuv_ref[hd]) for hd in range(MLA_HEADS)]
    out_ref[...] = _rms(jnp.concatenate(parts, axis=1), g_ref[...]).astype(BF16)


def _sample_attn_out(o_lat_h, w_uv_h, g):
    nb = o_lat_h.shape[1]
    return pl.pallas_call(
        _sample_attn_out_kernel,
        out_shape=jax.ShapeDtypeStruct((nb, MLA_WIDTH), BF16),
        name="sample_attention_out",
    )(o_lat_h, w_uv_h, g)


def _same_group(shape, row_shift, col_shift):
    rows = lax.broadcasted_iota(jnp.int32, shape, 0)
    cols = lax.broadcasted_iota(jnp.int32, shape, 1)
    mask = GROUPS_PER_SLAB - 1
    return ((rows >> row_shift) & mask) == ((cols >> col_shift) & mask)


def _s5_prompt_kernel(u_ref, wr_ref, we_ref, wf_ref, tr_ref, te_ref, tf_ref, are_ref, aim_ref, d_ref,
                      y_ref, hre_ref, him_ref, r_ref, e_ref, f_ref, uc_ref, s_ref, hp_ref, yc_ref):
    n_chunks = uc_ref.shape[0]
    n_blk = CHUNK_COLS // MXU_TILE
    ch_bits = S5_GROUP.bit_length() - 1
    st_bits = S5_STATE.bit_length() - 1

    @pl.when(pl.program_id(1) == 0)
    def _():
        zero = jnp.zeros((), F32)
        r_ref[...] = jnp.where(_same_group(r_ref.shape, ch_bits, ch_bits),
                               _dot(wr_ref[0], tr_ref[...]), zero).astype(BF16)
        e_ref[...] = jnp.where(_same_group(e_ref.shape, ch_bits, st_bits),
                               _dot(we_ref[0], te_ref[...]), zero).astype(BF16)
        f_ref[...] = jnp.where(_same_group(f_ref.shape, st_bits, ch_bits),
                               _dot(tf_ref[...], wf_ref[0]), zero).astype(BF16)

    for s in range(S5_CHUNK):
        uc_ref[:, s * LANES:(s + 1) * LANES] = u_ref[0, pl.ds(s, n_chunks, stride=S5_CHUNK), :]
    uc = uc_ref[...]
    ub = uc.astype(BF16)

    s_ref[...] = _dot(ub, e_ref[...])

    a_re = are_ref[0]
    a_im = aim_ref[0]

    def scan(k, carry):
        h_re, h_im = carry
        hp_ref[pl.ds(k, 1), :] = jnp.concatenate([h_re, h_im], axis=1)
        row = s_ref[pl.ds(k, 1), :]
        n_re = a_re * h_re - a_im * h_im + row[:, :SLAB_STATE]
        n_im = a_re * h_im + a_im * h_re + row[:, SLAB_STATE:]
        return n_re, n_im

    zero_state = jnp.zeros((1, SLAB_STATE), F32)
    h_re, h_im = lax.fori_loop(0, n_chunks, scan, (zero_state, zero_state), unroll=4)
    hre_ref[0, 0] = h_re
    him_ref[0, 0] = h_im

    yc_ref[...] = _dot(hp_ref[...].astype(BF16), f_ref[...]) + uc * d_ref[0]
    for tb in range(n_blk):
        yc_ref[:, tb * MXU_TILE:(tb + 1) * MXU_TILE] += _dot(ub[:, :(tb + 1) * MXU_TILE],
                                                             r_ref[(n_blk - 1 - tb) * MXU_TILE:, :])
    y = jax.nn.gelu(yc_ref[...])
    for t in range(S5_CHUNK):
        y_ref[0, pl.ds(t, n_chunks, stride=S5_CHUNK), :] = y[:, t * LANES:(t + 1) * LANES]


def _s5_prompt(u, ops):
    nb, t_len, _ = u.shape
    n_chunks = t_len // S5_CHUNK
    slab = lambda shape: pl.BlockSpec((1,) + shape, lambda j, b: (j,) + (0,) * len(shape))
    full = lambda shape: pl.BlockSpec(shape, lambda j, b: (0,) * len(shape))
    y, hre, him = pl.pallas_call(
        _s5_prompt_kernel,
        grid=(N_SLABS, nb),
        in_specs=[
            pl.BlockSpec((1, t_len, LANES), lambda j, b: (b, 0, j)),
            slab((CHUNK_COLS, 2 * S5_GROUP)), slab((CHUNK_COLS, 2 * S5_STATE)), slab((2 * S5_STATE, CHUNK_COLS)),
            full((2 * S5_GROUP, MXU_TILE)), full((2 * S5_STATE, 2 * SLAB_STATE)), full((2 * SLAB_STATE, 2 * S5_STATE)),
            slab((1, SLAB_STATE)), slab((1, SLAB_STATE)), slab((1, CHUNK_COLS)),
        ],
        out_specs=[
            pl.BlockSpec((1, t_len, LANES), lambda j, b: (b, 0, j)),
            pl.BlockSpec((1, 1, 1, SLAB_STATE), lambda j, b: (j, b, 0, 0)),
            pl.BlockSpec((1, 1, 1, SLAB_STATE), lambda j, b: (j, b, 0, 0)),
        ],
        out_shape=[
            jax.ShapeDtypeStruct((nb, t_len, S5_WIDTH), F32),
            jax.ShapeDtypeStruct((N_SLABS, nb, 1, SLAB_STATE), F32),
            jax.ShapeDtypeStruct((N_SLABS, nb, 1, SLAB_STATE), F32),
        ],
        scratch_shapes=[
            pltpu.VMEM((CHUNK_COLS, MXU_TILE), BF16),
            pltpu.VMEM((CHUNK_COLS, 2 * SLAB_STATE), BF16),
            pltpu.VMEM((2 * SLAB_STATE, CHUNK_COLS), BF16),
            pltpu.VMEM((n_chunks, CHUNK_COLS), F32),
            pltpu.VMEM((n_chunks, 2 * SLAB_STATE), F32),
            pltpu.VMEM((n_chunks, 2 * SLAB_STATE), F32),
            pltpu.VMEM((n_chunks, CHUNK_COLS), F32),
        ],
        compiler_params=_cparams(("arbitrary", "arbitrary")),
        name="s5_prompt",
    )(u, ops["wr"], ops["we"], ops["wf"], ops["tile_r"], ops["tile_e"], ops["tile_f"], ops["a_re"], ops["a_im"],
      ops["d_chunk"])

    def to_state(h):
        return h.reshape(N_SLABS, nb, GROUPS_PER_SLAB, S5_STATE).transpose(1, 0, 2, 3).reshape(nb, S5_GROUPS, S5_STATE)

    return y, to_state(hre), to_state(him)


def _split_bf16(x):
    hi = x.astype(BF16)
    return hi, (x - hi.astype(F32)).astype(BF16)


def _s5_sample_kernel(u_ref, h0re_ref, h0im_ref, bd_hi_ref, bd_lo_ref, cd_ref, lre_ref, lim_ref, d_ref,
                      y_ref, hre_ref, him_ref):
    u = u_ref[...]
    u_hi, u_lo = _split_bf16(u)
    bu = _dot(u_hi, bd_hi_ref[0]) + (_dot(u_hi, bd_lo_ref[0]) + _dot(u_lo, bd_hi_ref[0]))
    l_re = lre_ref[0]
    l_im = lim_ref[0]
    h0_re = h0re_ref[...]
    h0_im = h0im_ref[...]
    h_re = l_re * h0_re - l_im * h0_im + bu[:, :SLAB_STATE]
    h_im = l_re * h0_im + l_im * h0_re + bu[:, SLAB_STATE:]
    hre_ref[...] = h_re
    him_ref[...] = h_im
    h = jnp.concatenate([h_re, h_im], axis=1).astype(BF16)
    y_ref[...] = jax.nn.gelu(_dot(h, cd_ref[0]) + u * d_ref[0])


def _s5_sample(u, h0_re, h0_im, ops):
    nb = u.shape[0]
    n_state = S5_GROUPS * S5_STATE
    slab = lambda shape: pl.BlockSpec((1,) + shape, lambda j: (j,) + (0,) * len(shape))
    col = lambda width: pl.BlockSpec((nb, width), lambda j: (0, j))
    y, hre, him = pl.pallas_call(
        _s5_sample_kernel,
        grid=(N_SLABS,),
        in_specs=[
            col(LANES), col(SLAB_STATE), col(SLAB_STATE),
            slab((LANES, 2 * SLAB_STATE)), slab((LANES, 2 * SLAB_STATE)), slab((2 * SLAB_STATE, LANES)),
            slab((1, SLAB_STATE)), slab((1, SLAB_STATE)), slab((1, LANES)),
        ],
        out_specs=[col(LANES), col(SLAB_STATE), col(SLAB_STATE)],
        out_shape=[
            jax.ShapeDtypeStruct((nb, S5_WIDTH), F32),
            jax.ShapeDtypeStruct((nb, n_state), F32),
            jax.ShapeDtypeStruct((nb, n_state), F32),
        ],
        compiler_params=_cparams(("parallel",)),
        name="s5_sample",
    )(u, h0_re.reshape(nb, n_state), h0_im.reshape(nb, n_state), ops["bd_hi"], ops["bd_lo"], ops["cd"],
      ops["l_re"], ops["l_im"], ops["d_slab"])
    return y, hre.reshape(nb, S5_GROUPS, S5_STATE), him.reshape(nb, S5_GROUPS, S5_STATE)


def _mixout_kernel(x_ref, attn_ref, y_ref, gt_ref, wglu_ref, bglu_ref, gs_ref, woa_ref, woy_ref, o_ref):
    y = y_ref[...]
    z = _dot(y.astype(BF16), wglu_ref[...]) + bglu_ref[...]
    yn = _rms(y * jax.nn.sigmoid(z), gs_ref[...]).astype(BF16)
    mix = _dot(attn_ref[...], woa_ref[...]) + _dot(yn, woy_ref[...])
    o_ref[...] = x_ref[...] + gt_ref[0] * mix


def _mixout(x, attn, y, mod, tiles_per_b, p, *, tm):
    n = x.shape[0]
    r = mod.shape[1]
    full = lambda shape: pl.BlockSpec(shape, lambda i: (0,) * len(shape))
    return pl.pallas_call(
        _mixout_kernel,
        grid=(n // tm,),
        in_specs=[
            pl.BlockSpec((tm, D_MODEL), lambda i: (i, 0)),
            pl.BlockSpec((tm, MLA_WIDTH), lambda i: (i, 0)),
            pl.BlockSpec((tm, S5_WIDTH), lambda i: (i, 0)),
            pl.BlockSpec((1, r, D_MODEL), lambda i: (i // tiles_per_b, 0, 5)),
            full((S5_WIDTH, S5_WIDTH)), full((1, S5_WIDTH)), full((1, S5_WIDTH)),
            full((MLA_WIDTH, D_MODEL)), full((S5_WIDTH, D_MODEL)),
        ],
        out_specs=pl.BlockSpec((tm, D_MODEL), lambda i: (i, 0)),
        out_shape=jax.ShapeDtypeStruct((n, D_MODEL), F32),
        compiler_params=_cparams(("parallel",)),
        name="mixer_out",
    )(x, attn, y, mod, p["w_glu"], p["b_glu"], p["norm_ssm_out"], p["w_out_attn"], p["w_out_ssm"])


def _rope_tables(pos):
    half = QK_ROPE // 2
    inv_freq = ROPE_THETA ** (-jnp.arange(half, dtype=F32) / half)
    ang = pos.astype(F32)[:, None] * inv_freq[None, :]
    return jnp.cos(ang).T, jnp.sin(ang).T


def _group_diag(x, g_axis, new_axis):
    x = jnp.expand_dims(x, new_axis)
    shape = [1] * x.ndim
    shape[g_axis if g_axis < new_axis else g_axis + 1] = GROUPS_PER_SLAB
    shape[new_axis] = GROUPS_PER_SLAB
    return x * jnp.eye(GROUPS_PER_SLAB, dtype=x.dtype).reshape(shape)


def _by_slab(x, g_axis):
    return x.reshape(x.shape[:g_axis] + (N_SLABS, GROUPS_PER_SLAB) + x.shape[g_axis + 1:])


def _replicate_over_groups(n_outer, n_inner):
    eo = jnp.eye(n_outer, dtype=F32)[:, None, :, None, None]
    ei = jnp.eye(n_inner, dtype=F32)[None, :, None, None, :]
    ones = jnp.ones((1, 1, 1, GROUPS_PER_SLAB, 1), F32)
    return (eo * ei * ones).reshape(n_outer * n_inner, n_outer * GROUPS_PER_SLAB * n_inner)


def _s5_operators(a_re, a_im, log_dt, b_re, b_im, c_re, c_im, d_skip):
    hp = lax.Precision.HIGHEST
    a_re, a_im, b_re, b_im, c_re, c_im = (v.astype(F32) for v in (a_re, a_im, b_re, b_im, c_re, c_im))
    dt = jnp.exp(log_dt.astype(F32))[:, None]
    z_re = a_re * dt
    z_im = a_im * dt

    def lam_pow(n):
        mag = jnp.exp(z_re[None] * n[:, None, None])
        ang = z_im[None] * n[:, None, None]
        return mag * jnp.cos(ang), mag * jnp.sin(ang)

    steps = jnp.arange(S5_CHUNK + 1, dtype=F32)
    pw_re, pw_im = lam_pow(steps)
    lb_re, lb_im = pw_re[1], pw_im[1]
    den = a_re * a_re + a_im * a_im
    q_re = ((lb_re - 1.0) * a_re + lb_im * a_im) / den
    q_im = (lb_im * a_re - (lb_re - 1.0) * a_im) / den
    bb_re = q_re[:, :, None] * b_re - q_im[:, :, None] * b_im
    bb_im = q_re[:, :, None] * b_im + q_im[:, :, None] * b_re

    cp_re = c_re[None] * pw_re[:S5_CHUNK, :, None, :] - c_im[None] * pw_im[:S5_CHUNK, :, None, :]
    cp_im = c_re[None] * pw_im[:S5_CHUNK, :, None, :] + c_im[None] * pw_re[:S5_CHUNK, :, None, :]
    m = (jnp.einsum("ngcp,gpd->ngcd", cp_re, bb_re, precision=hp)
         - jnp.einsum("ngcp,gpd->ngcd", cp_im, bb_im, precision=hp))
    m_ext = jnp.concatenate([m, jnp.zeros_like(m[:1])], axis=0)
    n_blk = S5_CHUNK // 2
    d_i = (n_blk - 1 - jnp.arange(n_blk))[:, None, None]
    s_i = jnp.arange(2)[None, :, None]
    t_i = jnp.arange(2)[None, None, :]
    lag = 2 * d_i + t_i - s_i
    mg = _by_slab(m_ext[lag], 3)
    wr = mg.transpose(3, 0, 1, 4, 6, 2, 5).reshape(N_SLABS, CHUNK_COLS, 2 * S5_GROUP)

    pr_re, pr_im = lam_pow(S5_CHUNK - 1 - steps[:S5_CHUNK])
    w_re = pr_re[:, :, :, None] * bb_re[None] - pr_im[:, :, :, None] * bb_im[None]
    w_im = pr_re[:, :, :, None] * bb_im[None] + pr_im[:, :, :, None] * bb_re[None]

    def e_half(v):
        return _by_slab(v, 1).transpose(1, 0, 2, 4, 3).reshape(N_SLABS, CHUNK_COLS, S5_STATE)

    we = jnp.concatenate([e_half(w_re), e_half(w_im)], axis=2)

    g_re = c_re[None] * pw_re[1:, :, None, :] - c_im[None] * pw_im[1:, :, None, :]
    g_im = c_re[None] * pw_im[1:, :, None, :] + c_im[None] * pw_re[1:, :, None, :]

    def f_half(v):
        return _by_slab(v, 1).transpose(1, 4, 0, 2, 3).reshape(N_SLABS, S5_STATE, CHUNK_COLS)

    wf = jnp.concatenate([f_half(g_re), f_half(-g_im)], axis=1)

    def slab_vec(v):
        return v.reshape(N_SLABS, 1, SLAB_STATE)

    d_slab = d_skip.astype(F32).reshape(N_SLABS, 1, LANES)
    tile_e = _replicate_over_groups(2, S5_STATE)

    def bd_half(v):
        return _group_diag(_by_slab(v, 0).transpose(0, 1, 3, 2), 1, 3).reshape(N_SLABS, LANES, SLAB_STATE)

    def cd_half(v):
        return _group_diag(_by_slab(v, 0).transpose(0, 1, 3, 2), 1, 3).reshape(N_SLABS, SLAB_STATE, LANES)

    bd = jnp.concatenate([bd_half(bb_re), bd_half(bb_im)], axis=2)
    cd = jnp.concatenate([cd_half(c_re), cd_half(-c_im)], axis=1)
    bd_hi = bd.astype(BF16)
    bd_lo = (bd - bd_hi.astype(F32)).astype(BF16)
    return {
        "wr": wr.astype(BF16), "we": we.astype(BF16), "wf": wf.astype(BF16),
        "tile_r": _replicate_over_groups(2, S5_GROUP).astype(BF16),
        "tile_e": tile_e.astype(BF16), "tile_f": tile_e.T.astype(BF16),
        "a_re": slab_vec(pw_re[S5_CHUNK]), "a_im": slab_vec(pw_im[S5_CHUNK]),
        "d_chunk": jnp.tile(d_slab, (1, 1, S5_CHUNK)), "d_slab": d_slab,
        "bd_hi": bd_hi, "bd_lo": bd_lo, "cd": cd.astype(BF16),
        "l_re": slab_vec(lb_re), "l_im": slab_vec(lb_im),
    }


def _layer_params(w_in, w_uq, w_uk, w_uv, w_glu, w_out, norm_mix, norm_q, norm_kv, norm_attn_out, norm_ssm_out,
                  b_glu):
    c0, c1, c2 = Q_LORA, Q_LORA + KV_LORA, Q_LORA + KV_LORA + QK_ROPE
    w_uq_h = w_uq.reshape(Q_LORA, MLA_HEADS, QK_NOPE + QK_ROPE)
    return {
        "w_q": w_in[:, :c0].astype(BF16),
        "w_kv": w_in[:, c0:c1].astype(BF16),
        "w_kr_t": w_in[:, c1:c2].T.astype(BF16),
        "w_u": w_in[:, c2:].astype(BF16),
        "w_uq_nope": w_uq_h[:, :, :QK_NOPE].reshape(Q_LORA, MLA_HEADS * QK_NOPE).astype(BF16),
        "w_uq_rope_t": w_uq_h[:, :, QK_NOPE:].reshape(Q_LORA, MLA_HEADS * QK_ROPE).T.astype(BF16),
        "w_uk_h": w_uk.transpose(1, 0, 2).astype(BF16),
        "w_uv_h": w_uv.transpose(1, 2, 0).astype(BF16),
        "w_glu": w_glu.astype(BF16),
        "w_out_attn": w_out[:MLA_WIDTH].astype(BF16),
        "w_out_ssm": w_out[MLA_WIDTH:].astype(BF16),
        "norm_mix": norm_mix.reshape(1, D_MODEL),
        "norm_q": norm_q.reshape(1, Q_LORA),
        "norm_kv": norm_kv.reshape(1, KV_LORA),
        "norm_attn_out": norm_attn_out.reshape(1, MLA_WIDTH),
        "norm_attn_out_col": norm_attn_out.reshape(MLA_WIDTH, 1),
        "norm_ssm_out": norm_ssm_out.reshape(1, S5_WIDTH),
        "b_glu": b_glu.reshape(1, S5_WIDTH),
    }


def kernel(x_prompt, x_sample, c_prompt, c_sample, cache_ckv, cache_krope, state_s5_re, state_s5_im, page_table, w_ada, b_ada, norm_ffn1, ffn1_w1, ffn1_w3, ffn1_w2, norm_mix, w_in, norm_q, w_uq, norm_kv, w_uk, w_uv, s5_a_re, s5_a_im, s5_log_dt, s5_b_re, s5_b_im, s5_c_re, s5_c_im, s5_d, w_glu, b_glu, norm_attn_out, norm_ssm_out, w_out, norm_ffn2, ffn2_w1, ffn2_w3, ffn2_w2, norm_final):
    bp, seq, _ = x_prompt.shape
    bs = x_sample.shape[0]
    depth = w_ada.shape[0]
    assert depth == 1 and x_sample.shape[1] == 1
    n_pages = page_table.shape[1]
    past_len = n_pages * PAGE
    l = 0

    pad = (-(bs + bp)) % 8
    c_all = jnp.concatenate([c_sample, c_prompt, jnp.zeros((pad, D_MODEL), F32)], axis=0)
    mod = _ada(c_all, w_ada[l], b_ada[l])
    mod_s = mod[:bs].reshape(1, bs, ADA_CHUNKS * D_MODEL)
    mod_p = mod[bs:bs + bp].reshape(bp, 1, ADA_CHUNKS * D_MODEL)

    p = _layer_params(w_in[l], w_uq[l], w_uk[l], w_uv[l], w_glu[l], w_out[l], norm_mix[l], norm_q[l], norm_kv[l],
                      norm_attn_out[l], norm_ssm_out[l], b_glu[l])
    ops = _s5_operators(s5_a_re[l], s5_a_im[l], s5_log_dt[l], s5_b_re[l], s5_b_im[l], s5_c_re[l], s5_c_im[l],
                        s5_d[l])
    f1 = (ffn1_w1[l].astype(BF16), ffn1_w3[l].astype(BF16), ffn1_w2[l].astype(BF16))
    f2 = (ffn2_w1[l].astype(BF16), ffn2_w3[l].astype(BF16), ffn2_w2[l].astype(BF16))
    cos_p, sin_p = _rope_tables(jnp.arange(seq))
    cos_s, sin_s = _rope_tables(jnp.full((bs,), past_len))

    tm_p = 512
    xp = x_prompt.reshape(bp * seq, D_MODEL)
    xp = _ffn(xp, mod_p, 0, seq // tm_p, norm_ffn1[l], *f1, tm=tm_p)
    tm_mix = 512
    q_lat_t, q_rope_t, ckv_p, kc_p, kct_p, krope_t_p, kr_p, u_p = _mixin(xp, mod_p, seq // tm_mix, seq, p, cos_p,
                                                                         sin_p, tm=tm_mix, tq=256)
    attn_p = _prompt_attn(q_lat_t, q_rope_t, kc_p, kr_p, kct_p, p["w_uv_h"], p["norm_attn_out_col"])
    y_p, hre_p, him_p = _s5_prompt(u_p, ops)
    xp = _mixout(xp, attn_p.reshape(bp * seq, MLA_WIDTH), y_p.reshape(bp * seq, S5_WIDTH), mod_p, seq // tm_mix, p,
                 tm=tm_mix)
    y_prompt = _ffn(xp, mod_p, 6, seq // tm_p, norm_ffn2[l], *f2, norm_final, tm=tm_p).reshape(bp, seq, D_MODEL)

    xs = x_sample.reshape(bs, D_MODEL)
    xs = _ffn(xs, mod_s, 0, 1, norm_ffn1[l], *f1, tm=bs)
    q_lat_ts, q_rope_ts, ckv_s, _, _, krope_t_s, _, u_s = _mixin(xs, mod_s, 1, bs, p, cos_s, sin_s, tm=bs, tq=bs)
    krope_s = krope_t_s[0].T
    q_lat_s = q_lat_ts.reshape(KV_LORA, MLA_HEADS, bs).transpose(2, 1, 0)
    q_rope_s = q_rope_ts.reshape(QK_ROPE, MLA_HEADS, bs).transpose(2, 1, 0)
    o_lat = _sample_attn(page_table, q_lat_s, q_rope_s,
                         ckv_s.reshape(bs, 1, KV_LORA), krope_s.reshape(bs, 1, QK_ROPE), cache_ckv[l:l + 1],
                         cache_krope[l:l + 1].transpose(0, 1, 3, 2))
    attn_s = _sample_attn_out(o_lat.transpose(1, 0, 2), p["w_uv_h"], p["norm_attn_out"])
    y_s, hre_s, him_s = _s5_sample(u_s.reshape(bs, S5_WIDTH), state_s5_re[l], state_s5_im[l], ops)
    xs = _mixout(xs, attn_s, y_s, mod_s, 1, p, tm=bs)
    y_sample = _ffn(xs, mod_s, 6, 1, norm_ffn2[l], *f2, norm_final, tm=bs).reshape(bs, 1, D_MODEL)

    return (y_prompt, y_sample,
            ckv_p[None], krope_t_p.transpose(0, 2, 1)[None],
            ckv_s.reshape(1, bs, 1, KV_LORA), krope_s.reshape(1, bs, 1, QK_ROPE),
            hre_p[None], him_p[None], hre_s[None], him_s[None])
```

```python
import functools

import jax
import jax.numpy as jnp
from jax import lax
from jax.experimental import pallas as pl
from jax.experimental.pallas import tpu as pltpu

F32 = jnp.float32
BF16 = jnp.bfloat16

D_MODEL = 2048
D_FF = 5632
MLA_HEADS = 8
QK_NOPE = 128
QK_ROPE = 64
V_HEAD = 128
Q_LORA = 512
KV_LORA = 256
MLA_WIDTH = 1024
S5_WIDTH = 1024
S5_GROUP = 16
S5_GROUPS = 64
S5_STATE = 64
ADA_CHUNKS = 9
PAGE = 128
ROPE_THETA = 10000.0
SOFTMAX_SCALE = (QK_NOPE + QK_ROPE) ** -0.5
EPS = 1e-6

LANES = 128
S5_CHUNK = 16
GROUPS_PER_SLAB = LANES // S5_GROUP
N_SLABS = S5_WIDTH // LANES
SLAB_STATE = GROUPS_PER_SLAB * S5_STATE
CHUNK_COLS = S5_CHUNK * LANES
MXU_TILE = 2 * LANES
VMEM_LIMIT = 56 * 1024 * 1024


def _cparams(sem):
    return pltpu.CompilerParams(dimension_semantics=sem, vmem_limit_bytes=VMEM_LIMIT)


def _rms(x, g):
    return x * lax.rsqrt(jnp.mean(x * x, axis=-1, keepdims=True) + EPS) * g


def _rms_rows(x, g):
    return x * lax.rsqrt(jnp.mean(x * x, axis=0, keepdims=True) + EPS) * g


def _dot(a, b):
    return jnp.dot(a, b, preferred_element_type=F32)


def _dot_nt(a, b):
    return lax.dot_general(a, b, (((1,), (1,)), ((), ())), preferred_element_type=F32)


def _ada_kernel(c_ref, w_ref, b_ref, o_ref):
    c = c_ref[...]
    a = (c * jax.nn.sigmoid(c)).astype(BF16)
    o_ref[...] = _dot(a, w_ref[...].astype(BF16)) + b_ref[...]


def _ada(c, w_ada, b_ada):
    rows = c.shape[0]
    n = w_ada.shape[1]
    tn = 1024
    return pl.pallas_call(
        _ada_kernel,
        grid=(n // tn,),
        in_specs=[
            pl.BlockSpec((rows, D_MODEL), lambda j: (0, 0)),
            pl.BlockSpec((D_MODEL, tn), lambda j: (0, j)),
            pl.BlockSpec((1, tn), lambda j: (0, j)),
        ],
        out_specs=pl.BlockSpec((rows, tn), lambda j: (0, j)),
        out_shape=jax.ShapeDtypeStruct((rows, n), F32),
        compiler_params=_cparams(("arbitrary",)),
        name="ada_modulation",
    )(c, w_ada, b_ada.reshape(1, n))


def _ffn_kernel(x_ref, sh_ref, sc_ref, gt_ref, g_ref, w1_ref, w3_ref, w2_ref, *rest, final_norm):
    if final_norm:
        gf_ref, o_ref, h_ref = rest
    else:
        o_ref, h_ref = rest
    j = pl.program_id(1)

    @pl.when(j == 0)
    def _():
        h = _rms(x_ref[...], g_ref[...]) * (1.0 + sc_ref[0]) + sh_ref[0]
        h_ref[...] = h.astype(BF16)
        o_ref[...] = jnp.zeros_like(o_ref)

    h = h_ref[...]
    a = _dot(h, w1_ref[...])
    b = _dot(h, w3_ref[...])
    act = (a * jax.nn.sigmoid(a) * b).astype(BF16)
    o_ref[...] += _dot(act, w2_ref[...])

    @pl.when(j == pl.num_programs(1) - 1)
    def _():
        y = x_ref[...] + 0.5 * gt_ref[0] * o_ref[...]
        if final_norm:
            y = _rms(y, gf_ref[...])
        o_ref[...] = y


def _ffn(x, mod, chunk0, tiles_per_b, g, w1, w3, w2, gf=None, *, tm, tf=512):
    n = x.shape[0]
    r = mod.shape[1]

    def mod_spec(k):
        return pl.BlockSpec((1, r, D_MODEL), lambda i, j: (i // tiles_per_b, 0, k))

    in_specs = [
        pl.BlockSpec((tm, D_MODEL), lambda i, j: (i, 0)),
        mod_spec(chunk0), mod_spec(chunk0 + 1), mod_spec(chunk0 + 2),
        pl.BlockSpec((1, D_MODEL), lambda i, j: (0, 0)),
        pl.BlockSpec((D_MODEL, tf), lambda i, j: (0, j)),
        pl.BlockSpec((D_MODEL, tf), lambda i, j: (0, j)),
        pl.BlockSpec((tf, D_MODEL), lambda i, j: (j, 0)),
    ]
    args = [x, mod, mod, mod, g.reshape(1, D_MODEL), w1, w3, w2]
    if gf is not None:
        in_specs.append(pl.BlockSpec((1, D_MODEL), lambda i, j: (0, 0)))
        args.append(gf.reshape(1, D_MODEL))
    return pl.pallas_call(
        functools.partial(_ffn_kernel, final_norm=gf is not None),
        grid=(n // tm, D_FF // tf),
        in_specs=in_specs,
        out_specs=pl.BlockSpec((tm, D_MODEL), lambda i, j: (i, 0)),
        out_shape=jax.ShapeDtypeStruct((n, D_MODEL), F32),
        scratch_shapes=[pltpu.VMEM((tm, D_MODEL), BF16)],
        compiler_params=_cparams(("parallel", "arbitrary")),
        name="macaron_ffn",
    )(*args)


def _rope_rows(x, cos, sin):
    half = QK_ROPE // 2
    x1 = x[:half]
    x2 = x[half:]
    return x1 * cos - x2 * sin, x1 * sin + x2 * cos


def _mixin_kernel(x_ref, sh_ref, sc_ref, g_ref, wq_ref, wkv_ref, wkr_ref, wu_ref, gq_ref, wqn_ref, wqr_ref,
                  wuk_ref, gkv_ref, cos_ref, sin_ref,
                  qlat_ref, qrope_ref, ckv_ref, kc_ref, kct_ref, krope_ref, kr_ref, u_ref, *, tq):
    half = QK_ROPE // 2
    n_q = x_ref.shape[0] // tq
    h = (_rms(x_ref[...], g_ref[...]) * (1.0 + sc_ref[0]) + sh_ref[0]).astype(BF16)
    cos = cos_ref[...]
    sin = sin_ref[...]

    u_ref[0] = _dot(h, wu_ref[...])

    ckv = _rms(_dot(h, wkv_ref[...]), gkv_ref[...])
    ckv_ref[0] = ckv
    kc_ref[0] = ckv.astype(BF16)
    kct_ref[0] = ckv.T.astype(BF16)

    k1, k2 = _rope_rows(_dot_nt(wkr_ref[...], h), cos, sin)
    kr_t = jnp.concatenate([k1, k2], axis=0)
    krope_ref[0] = kr_t
    kr_ref[0] = kr_t.T.astype(BF16)

    qn = _rms(_dot(h, wq_ref[...]), gq_ref[...]).astype(BF16)
    q_nope = (_dot(qn, wqn_ref[...]) * SOFTMAX_SCALE).astype(BF16)
    q_rope_t = _dot_nt(wqr_ref[...], qn) * SOFTMAX_SCALE
    for hd in range(MLA_HEADS):
        ql_t = _dot_nt(wuk_ref[hd], q_nope[:, hd * QK_NOPE:(hd + 1) * QK_NOPE]).astype(BF16)
        r1, r2 = _rope_rows(q_rope_t[hd * QK_ROPE:(hd + 1) * QK_ROPE], cos, sin)
        r1 = r1.astype(BF16)
        r2 = r2.astype(BF16)
        for qq in range(n_q):
            src = slice(qq * tq, (qq + 1) * tq)
            dst = slice(hd * tq, (hd + 1) * tq)
            qlat_ref[0, qq, :, dst] = ql_t[:, src]
            qrope_ref[0, qq, :half, dst] = r1[:, src]
            qrope_ref[0, qq, half:, dst] = r2[:, src]


def _mixin(x, mod, tiles_per_b, t_len, p, cos_t, sin_t, *, tm, tq):
    n = x.shape[0]
    nb = n // t_len
    r = mod.shape[1]
    half = QK_ROPE // 2
    full = lambda shape: pl.BlockSpec(shape, lambda i: (0,) * len(shape))
    rows = lambda width: pl.BlockSpec((1, tm, width), lambda i: (i // tiles_per_b, i % tiles_per_b, 0))
    cols = lambda height: pl.BlockSpec((1, height, tm), lambda i: (i // tiles_per_b, 0, i % tiles_per_b))
    head_cols = lambda height: pl.BlockSpec((1, tm // tq, height, MLA_HEADS * tq),
                                            lambda i: (i // tiles_per_b, i % tiles_per_b, 0, 0))
    in_specs = [
        pl.BlockSpec((tm, D_MODEL), lambda i: (i, 0)),
        pl.BlockSpec((1, r, D_MODEL), lambda i: (i // tiles_per_b, 0, 3)),
        pl.BlockSpec((1, r, D_MODEL), lambda i: (i // tiles_per_b, 0, 4)),
        full((1, D_MODEL)),
        full((D_MODEL, Q_LORA)), full((D_MODEL, KV_LORA)), full((QK_ROPE, D_MODEL)), full((D_MODEL, S5_WIDTH)),
        full((1, Q_LORA)), full((Q_LORA, MLA_HEADS * QK_NOPE)), full((MLA_HEADS * QK_ROPE, Q_LORA)),
        full((MLA_HEADS, KV_LORA, QK_NOPE)), full((1, KV_LORA)),
        pl.BlockSpec((half, tm), lambda i: (0, i % tiles_per_b)),
        pl.BlockSpec((half, tm), lambda i: (0, i % tiles_per_b)),
    ]
    out_specs = [head_cols(KV_LORA), head_cols(QK_ROPE), rows(KV_LORA), rows(KV_LORA), cols(KV_LORA),
                 cols(QK_ROPE), rows(QK_ROPE), rows(S5_WIDTH)]
    out_shape = [
        jax.ShapeDtypeStruct((nb, t_len // tq, KV_LORA, MLA_HEADS * tq), BF16),
        jax.ShapeDtypeStruct((nb, t_len // tq, QK_ROPE, MLA_HEADS * tq), BF16),
        jax.ShapeDtypeStruct((nb, t_len, KV_LORA), F32),
        jax.ShapeDtypeStruct((nb, t_len, KV_LORA), BF16),
        jax.ShapeDtypeStruct((nb, KV_LORA, t_len), BF16),
        jax.ShapeDtypeStruct((nb, QK_ROPE, t_len), F32),
        jax.ShapeDtypeStruct((nb, t_len, QK_ROPE), BF16),
        jax.ShapeDtypeStruct((nb, t_len, S5_WIDTH), F32),
    ]
    return pl.pallas_call(
        functools.partial(_mixin_kernel, tq=tq),
        grid=(n // tm,),
        in_specs=in_specs,
        out_specs=out_specs,
        out_shape=out_shape,
        compiler_params=_cparams(("parallel",)),
        name="mixer_in",
    )(x, mod, mod, p["norm_mix"], p["w_q"], p["w_kv"], p["w_kr_t"], p["w_u"], p["norm_q"], p["w_uq_nope"],
      p["w_uq_rope_t"], p["w_uk_h"], p["norm_kv"], cos_t, sin_t)


def _prompt_attn_kernel(ql_ref, qr_ref, kc_ref, kr_ref, kct_ref, wuv_ref, g_ref, o_ref, m_ref, l_ref, acc_ref, *,
                        tq):
    qi = pl.program_id(1)
    width = m_ref.shape[2]
    heads_per_half = width // tq
    n_half = m_ref.shape[0]
    m_ref[...] = jnp.full_like(m_ref, -jnp.inf)
    l_ref[...] = jnp.zeros_like(l_ref)
    acc_ref[...] = jnp.zeros_like(acc_ref)

    def step(ki, masked):
        start = pl.multiple_of(ki * tq, tq)
        kc = kc_ref[0, pl.ds(start, tq), :]
        kr = kr_ref[0, pl.ds(start, tq), :]
        kct = kct_ref[0, :, pl.ds(start, tq)]
        if masked:
            k_local = lax.broadcasted_iota(jnp.int32, (tq, width), 0)
            t_local = lax.broadcasted_iota(jnp.int32, (tq, width), 1) & (tq - 1)
            keep = k_local <= t_local
        for hf in range(n_half):
            lanes = slice(hf * width, (hf + 1) * width)
            s = _dot(kc, ql_ref[0, 0, :, lanes]) + _dot(kr, qr_ref[0, 0, :, lanes])
            if masked:
                s = jnp.where(keep, s, -jnp.inf)
            m_old = m_ref[hf]
            m_new = jnp.maximum(m_old, jnp.max(s, axis=0, keepdims=True))
            alpha = jnp.exp(m_old - m_new)
            pexp = jnp.exp(s - m_new)
            l_ref[hf] = alpha * l_ref[hf] + jnp.sum(pexp, axis=0, keepdims=True)
            acc_ref[hf] = alpha * acc_ref[hf] + _dot(kct, pexp.astype(BF16))
            m_ref[hf] = m_new

    def body(ki, carry):
        step(ki, False)
        return carry

    lax.fori_loop(0, qi, body, 0)
    step(qi, True)
    parts = []
    for hd in range(MLA_HEADS):
        hf = hd // heads_per_half
        lanes = slice((hd % heads_per_half) * tq, (hd % heads_per_half + 1) * tq)
        o_t = acc_ref[hf, :, lanes] / l_ref[hf, :, lanes]
        parts.append(_dot(wuv_ref[hd], o_t.astype(BF16)))
    attn_t = _rms_rows(jnp.concatenate(parts, axis=0), g_ref[...])
    o_ref[0] = attn_t.T.astype(BF16)


def _prompt_attn(q_lat_t, q_rope_t, kc, kr, kc_t, w_uv_h, g_col, *, tq=256, n_half=1):
    nb, n_q, _, _ = q_lat_t.shape
    t_len = n_q * tq
    width = MLA_HEADS * tq // n_half
    return pl.pallas_call(
        functools.partial(_prompt_attn_kernel, tq=tq),
        grid=(nb, n_q),
        in_specs=[
            pl.BlockSpec((1, 1, KV_LORA, MLA_HEADS * tq), lambda b, i: (b, i, 0, 0)),
            pl.BlockSpec((1, 1, QK_ROPE, MLA_HEADS * tq), lambda b, i: (b, i, 0, 0)),
            pl.BlockSpec((1, t_len, KV_LORA), lambda b, i: (b, 0, 0)),
            pl.BlockSpec((1, t_len, QK_ROPE), lambda b, i: (b, 0, 0)),
            pl.BlockSpec((1, KV_LORA, t_len), lambda b, i: (b, 0, 0)),
            pl.BlockSpec((MLA_HEADS, V_HEAD, KV_LORA), lambda b, i: (0, 0, 0)),
            pl.BlockSpec((MLA_WIDTH, 1), lambda b, i: (0, 0)),
        ],
        out_specs=pl.BlockSpec((1, tq, MLA_WIDTH), lambda b, i: (b, i, 0)),
        out_shape=jax.ShapeDtypeStruct((nb, t_len, MLA_WIDTH), BF16),
        scratch_shapes=[pltpu.VMEM((n_half, 1, width), F32), pltpu.VMEM((n_half, 1, width), F32),
                        pltpu.VMEM((n_half, KV_LORA, width), F32)],
        compiler_params=_cparams(("parallel", "parallel")),
        name="prompt_attention",
    )(q_lat_t, q_rope_t, kc, kr, kc_t, w_uv_h, g_col)


def _sample_attn_kernel(pt_ref, ql_ref, qr_ref, kcn_ref, krn_ref, ckv_hbm, kr_hbm, o_ref, kbuf, rbuf, sem_k, sem_r,
                        *, pages, streams):
    nb, n_pages = pt_ref.shape
    n_chunks = n_pages // pages
    n_groups = nb // streams
    total = n_groups * n_chunks

    def chunk_copies(g, slot):
        grp = g // n_chunks
        c = g % n_chunks
        copies = []
        for st in range(streams):
            b = grp * streams + st
            for i in range(pages):
                page = pt_ref[b, c * pages + i]
                copies.append(pltpu.make_async_copy(
                    ckv_hbm.at[0, page], kbuf.at[slot, st, pl.ds(i * PAGE, PAGE), :], sem_k.at[slot, st]))
                copies.append(pltpu.make_async_copy(
                    kr_hbm.at[0, page], rbuf.at[slot, st, :, pl.ds(i * PAGE, PAGE)], sem_r.at[slot, st]))
        return copies

    grp = pl.program_id(0)

    @pl.when(grp == 0)
    def _():
        for cp in chunk_copies(0, 0):
            cp.start()

    qs = [(ql_ref[st], qr_ref[st]) for st in range(streams)]

    def chunk_body(c, carry):
        g = grp * n_chunks + c
        slot = g % 2

        @pl.when(g + 1 < total)
        def _():
            for cp in chunk_copies(g + 1, 1 - slot):
                cp.start()

        for cp in chunk_copies(g, slot):
            cp.wait()
        out = []
        for st in range(streams):
            m_old, l_old, acc_old = carry[st]
            ql, qr = qs[st]
            kc = kbuf[slot, st].astype(BF16)
            kr_t = rbuf[slot, st].astype(BF16)
            s = _dot_nt(ql, kc) + _dot(qr, kr_t)
            m_new = jnp.maximum(m_old, jnp.max(s, axis=1, keepdims=True))
            alpha = jnp.exp(m_old - m_new)
            pexp = jnp.exp(s - m_new)
            l_new = alpha * l_old + jnp.sum(pexp, axis=1, keepdims=True)
            acc_new = alpha * acc_old + _dot(pexp.astype(BF16), kc)
            out.append((m_new, l_new, acc_new))
        return tuple(out)

    init = tuple((jnp.full((MLA_HEADS, 1), -jnp.inf, F32), jnp.zeros((MLA_HEADS, 1), F32),
                  jnp.zeros((MLA_HEADS, KV_LORA), F32)) for _ in range(streams))
    final = lax.fori_loop(0, n_chunks, chunk_body, init)
    for st in range(streams):
        m_old, l_old, acc_old = final[st]
        ql, qr = qs[st]
        kcn = kcn_ref[st].astype(BF16).astype(F32)
        krn = krn_ref[st].astype(BF16).astype(F32)
        s_n = (jnp.sum(ql.astype(F32) * kcn, axis=1, keepdims=True)
               + jnp.sum(qr.astype(F32) * krn, axis=1, keepdims=True))
        m_f = jnp.maximum(m_old, s_n)
        a_f = jnp.exp(m_old - m_f)
        p_n = jnp.exp(s_n - m_f)
        l_f = a_f * l_old + p_n
        acc_f = a_f * acc_old + p_n.astype(BF16).astype(F32) * kcn
        o_ref[st] = acc_f / l_f


def _sample_attn(page_table, q_lat, q_rope, ckv_new, krope_new, cache_ckv, cache_krope_t, *, pages=16, streams=2):
    nb, n_pages = page_table.shape
    assert n_pages % pages == 0 and nb % streams == 0
    per_group = lambda rows, width: pl.BlockSpec((streams, rows, width), lambda g, pt: (g, 0, 0))
    grid_spec = pltpu.PrefetchScalarGridSpec(
        num_scalar_prefetch=1,
        grid=(nb // streams,),
        in_specs=[
            per_group(MLA_HEADS, KV_LORA), per_group(MLA_HEADS, QK_ROPE),
            per_group(1, KV_LORA), per_group(1, QK_ROPE),
            pl.BlockSpec(memory_space=pl.ANY), pl.BlockSpec(memory_space=pl.ANY),
        ],
        out_specs=per_group(MLA_HEADS, KV_LORA),
        scratch_shapes=[
            pltpu.VMEM((2, streams, pages * PAGE, KV_LORA), F32),
            pltpu.VMEM((2, streams, QK_ROPE, pages * PAGE), F32),
            pltpu.SemaphoreType.DMA((2, streams)),
            pltpu.SemaphoreType.DMA((2, streams)),
        ],
    )
    return pl.pallas_call(
        functools.partial(_sample_attn_kernel, pages=pages, streams=streams),
        grid_spec=grid_spec,
        out_shape=jax.ShapeDtypeStruct((nb, MLA_HEADS, KV_LORA), F32),
        compiler_params=_cparams(("arbitrary",)),
        name="sample_paged_attention",
    )(page_table, q_lat, q_rope, ckv_new, krope_new, cache_ckv, cache_krope_t)


def _sample_attn_out_kernel(o_ref, wuv_ref, g_ref, out_ref):
    parts = [_dot_nt(o_ref[hd].astype(BF16), wuv_ref[hd]) for hd in range(MLA_HEADS)]
    out_ref[...] = _rms(jnp.concatenate(parts, axis=1), g_ref[...]).astype(BF16)


def _sample_attn_out(o_lat_h, w_uv_h, g):
    nb = o_lat_h.shape[1]
    return pl.pallas_call(
        _sample_attn_out_kernel,
        out_shape=jax.ShapeDtypeStruct((nb, MLA_WIDTH), BF16),
        name="sample_attention_out",
    )(o_lat_h, w_uv_h, g)


def _same_group(shape, row_shift, col_shift):
    rows = lax.broadcasted_iota(jnp.int32, shape, 0)
    cols = lax.broadcasted_iota(jnp.int32, shape, 1)
    mask = GROUPS_PER_SLAB - 1
    return ((rows >> row_shift) & mask) == ((cols >> col_shift) & mask)


def _s5_prompt_kernel(u_ref, wr_ref, we_ref, wf_ref, tr_ref, te_ref, tf_ref, are_ref, aim_ref, d_ref,
                      y_ref, hre_ref, him_ref, r_ref, e_ref, f_ref, uc_ref, s_ref, hp_ref, yc_ref):
    n_chunks = uc_ref.shape[0]
    n_blk = CHUNK_COLS // MXU_TILE
    ch_bits = S5_GROUP.bit_length() - 1
    st_bits = S5_STATE.bit_length() - 1

    @pl.when(pl.program_id(1) == 0)
    def _():
        zero = jnp.zeros((), F32)
        r_ref[...] = jnp.where(_same_group(r_ref.shape, ch_bits, ch_bits),
                               _dot(wr_ref[0], tr_ref[...]), zero).astype(BF16)
        e_ref[...] = jnp.where(_same_group(e_ref.shape, ch_bits, st_bits),
                               _dot(we_ref[0], te_ref[...]), zero).astype(BF16)
        f_ref[...] = jnp.where(_same_group(f_ref.shape, st_bits, ch_bits),
                               _dot(tf_ref[...], wf_ref[0]), zero).astype(BF16)

    for s in range(S5_CHUNK):
        uc_ref[:, s * LANES:(s + 1) * LANES] = u_ref[0, pl.ds(s, n_chunks, stride=S5_CHUNK), :]
    uc = uc_ref[...]
    ub = uc.astype(BF16)

    s_ref[...] = _dot(ub, e_ref[...])

    a_re = are_ref[0]
    a_im = aim_ref[0]

    def scan(k, carry):
        h_re, h_im = carry
        hp_ref[pl.ds(k, 1), :] = jnp.concatenate([h_re, h_im], axis=1)
        row = s_ref[pl.ds(k, 1), :]
        n_re = a_re * h_re - a_im * h_im + row[:, :SLAB_STATE]
        n_im = a_re * h_im + a_im * h_re + row[:, SLAB_STATE:]
        return n_re, n_im

    zero_state = jnp.zeros((1, SLAB_STATE), F32)
    h_re, h_im = lax.fori_loop(0, n_chunks, scan, (zero_state, zero_state), unroll=4)
    hre_ref[0, 0] = h_re
    him_ref[0, 0] = h_im

    yc_ref[...] = _dot(hp_ref[...].astype(BF16), f_ref[...]) + uc * d_ref[0]
    for tb in range(n_blk):
        yc_ref[:, tb * MXU_TILE:(tb + 1) * MXU_TILE] += _dot(ub[:, :(tb + 1) * MXU_TILE],
                                                             r_ref[(n_blk - 1 - tb) * MXU_TILE:, :])
    y = jax.nn.gelu(yc_ref[...])
    for t in range(S5_CHUNK):
        y_ref[0, pl.ds(t, n_chunks, stride=S5_CHUNK), :] = y[:, t * LANES:(t + 1) * LANES]


def _s5_prompt(u, ops):
    nb, t_len, _ = u.shape
    n_chunks = t_len // S5_CHUNK
    slab = lambda shape: pl.BlockSpec((1,) + shape, lambda j, b: (j,) + (0,) * len(shape))
    full = lambda shape: pl.BlockSpec(shape, lambda j, b: (0,) * len(shape))
    y, hre, him = pl.pallas_call(
        _s5_prompt_kernel,
        grid=(N_SLABS, nb),
        in_specs=[
            pl.BlockSpec((1, t_len, LANES), lambda j, b: (b, 0, j)),
            slab((CHUNK_COLS, 2 * S5_GROUP)), slab((CHUNK_COLS, 2 * S5_STATE)), slab((2 * S5_STATE, CHUNK_COLS)),
            full((2 * S5_GROUP, MXU_TILE)), full((2 * S5_STATE, 2 * SLAB_STATE)), full((2 * SLAB_STATE, 2 * S5_STATE)),
            slab((1, SLAB_STATE)), slab((1, SLAB_STATE)), slab((1, CHUNK_COLS)),
        ],
        out_specs=[
            pl.BlockSpec((1, t_len, LANES), lambda j, b: (b, 0, j)),
            pl.BlockSpec((1, 1, 1, SLAB_STATE), lambda j, b: (j, b, 0, 0)),
            pl.BlockSpec((1, 1, 1, SLAB_STATE), lambda j, b: (j, b, 0, 0)),
        ],
        out_shape=[
            jax.ShapeDtypeStruct((nb, t_len, S5_WIDTH), F32),
            jax.ShapeDtypeStruct((N_SLABS, nb, 1, SLAB_STATE), F32),
            jax.ShapeDtypeStruct((N_SLABS, nb, 1, SLAB_STATE), F32),
        ],
        scratch_shapes=[
            pltpu.VMEM((CHUNK_COLS, MXU_TILE), BF16),
            pltpu.VMEM((CHUNK_COLS, 2 * SLAB_STATE), BF16),
            pltpu.VMEM((2 * SLAB_STATE, CHUNK_COLS), BF16),
            pltpu.VMEM((n_chunks, CHUNK_COLS), F32),
            pltpu.VMEM((n_chunks, 2 * SLAB_STATE), F32),
            pltpu.VMEM((n_chunks, 2 * SLAB_STATE), F32),
            pltpu.VMEM((n_chunks, CHUNK_COLS), F32),
        ],
        compiler_params=_cparams(("arbitrary", "arbitrary")),
        name="s5_prompt",
    )(u, ops["wr"], ops["we"], ops["wf"], ops["tile_r"], ops["tile_e"], ops["tile_f"], ops["a_re"], ops["a_im"],
      ops["d_chunk"])

    def to_state(h):
        return h.reshape(N_SLABS, nb, GROUPS_PER_SLAB, S5_STATE).transpose(1, 0, 2, 3).reshape(nb, S5_GROUPS, S5_STATE)

    return y, to_state(hre), to_state(him)


def _split_bf16(x):
    hi = x.astype(BF16)
    return hi, (x - hi.astype(F32)).astype(BF16)


def _s5_sample_kernel(u_ref, h0re_ref, h0im_ref, bd_hi_ref, bd_lo_ref, cd_ref, lre_ref, lim_ref, d_ref,
                      y_ref, hre_ref, him_ref):
    u = u_ref[...]
    u_hi, u_lo = _split_bf16(u)
    bu = _dot(u_hi, bd_hi_ref[0]) + (_dot(u_hi, bd_lo_ref[0]) + _dot(u_lo, bd_hi_ref[0]))
    l_re = lre_ref[0]
    l_im = lim_ref[0]
    h0_re = h0re_ref[...]
    h0_im = h0im_ref[...]
    h_re = l_re * h0_re - l_im * h0_im + bu[:, :SLAB_STATE]
    h_im = l_re * h0_im + l_im * h0_re + bu[:, SLAB_STATE:]
    hre_ref[...] = h_re
    him_ref[...] = h_im
    h = jnp.concatenate([h_re, h_im], axis=1).astype(BF16)
    y_ref[...] = jax.nn.gelu(_dot(h, cd_ref[0]) + u * d_ref[0])


def _s5_sample(u, h0_re, h0_im, ops):
    nb = u.shape[0]
    n_state = S5_GROUPS * S5_STATE
    slab = lambda shape: pl.BlockSpec((1,) + shape, lambda j: (j,) + (0,) * len(shape))
    col = lambda width: pl.BlockSpec((nb, width), lambda j: (0, j))
    y, hre, him = pl.pallas_call(
        _s5_sample_kernel,
        grid=(N_SLABS,),
        in_specs=[
            col(LANES), col(SLAB_STATE), col(SLAB_STATE),
            slab((LANES, 2 * SLAB_STATE)), slab((LANES, 2 * SLAB_STATE)), slab((2 * SLAB_STATE, LANES)),
            slab((1, SLAB_STATE)), slab((1, SLAB_STATE)), slab((1, LANES)),
        ],
        out_specs=[col(LANES), col(SLAB_STATE), col(SLAB_STATE)],
        out_shape=[
            jax.ShapeDtypeStruct((nb, S5_WIDTH), F32),
            jax.ShapeDtypeStruct((nb, n_state), F32),
            jax.ShapeDtypeStruct((nb, n_state), F32),
        ],
        compiler_params=_cparams(("parallel",)),
        name="s5_sample",
    )(u, h0_re.reshape(nb, n_state), h0_im.reshape(nb, n_state), ops["bd_hi"], ops["bd_lo"], ops["cd"],
      ops["l_re"], ops["l_im"], ops["d_slab"])
    return y, hre.reshape(nb, S5_GROUPS, S5_STATE), him.reshape(nb, S5_GROUPS, S5_STATE)


def _mixout_kernel(x_ref, attn_ref, y_ref, gt_ref, wglu_ref, bglu_ref, gs_ref, woa_ref, woy_ref, o_ref):
    y = y_ref[...]
    z = _dot(y.astype(BF16), wglu_ref[...]) + bglu_ref[...]
    yn = _rms(y * jax.nn.sigmoid(z), gs_ref[...]).astype(BF16)
    mix = _dot(attn_ref[...], woa_ref[...]) + _dot(yn, woy_ref[...])
    o_ref[...] = x_ref[...] + gt_ref[0] * mix


def _mixout(x, attn, y, mod, tiles_per_b, p, *, tm):
    n = x.shape[0]
    r = mod.shape[1]
    full = lambda shape: pl.BlockSpec(shape, lambda i: (0,) * len(shape))
    return pl.pallas_call(
        _mixout_kernel,
        grid=(n // tm,),
        in_specs=[
            pl.BlockSpec((tm, D_MODEL), lambda i: (i, 0)),
            pl.BlockSpec((tm, MLA_WIDTH), lambda i: (i, 0)),
            pl.BlockSpec((tm, S5_WIDTH), lambda i: (i, 0)),
            pl.BlockSpec((1, r, D_MODEL), lambda i: (i // tiles_per_b, 0, 5)),
            full((S5_WIDTH, S5_WIDTH)), full((1, S5_WIDTH)), full((1, S5_WIDTH)),
            full((MLA_WIDTH, D_MODEL)), full((S5_WIDTH, D_MODEL)),
        ],
        out_specs=pl.BlockSpec((tm, D_MODEL), lambda i: (i, 0)),
        out_shape=jax.ShapeDtypeStruct((n, D_MODEL), F32),
        compiler_params=_cparams(("parallel",)),
        name="mixer_out",
    )(x, attn, y, mod, p["w_glu"], p["b_glu"], p["norm_ssm_out"], p["w_out_attn"], p["w_out_ssm"])


def _rope_tables(pos):
    half = QK_ROPE // 2
    inv_freq = ROPE_THETA ** (-jnp.arange(half, dtype=F32) / half)
    ang = pos.astype(F32)[:, None] * inv_freq[None, :]
    return jnp.cos(ang).T, jnp.sin(ang).T


def _group_diag(x, g_axis, new_axis):
    x = jnp.expand_dims(x, new_axis)
    shape = [1] * x.ndim
    shape[g_axis if g_axis < new_axis else g_axis + 1] = GROUPS_PER_SLAB
    shape[new_axis] = GROUPS_PER_SLAB
    return x * jnp.eye(GROUPS_PER_SLAB, dtype=x.dtype).reshape(shape)


def _by_slab(x, g_axis):
    return x.reshape(x.shape[:g_axis] + (N_SLABS, GROUPS_PER_SLAB) + x.shape[g_axis + 1:])


def _replicate_over_groups(n_outer, n_inner):
    eo = jnp.eye(n_outer, dtype=F32)[:, None, :, None, None]
    ei = jnp.eye(n_inner, dtype=F32)[None, :, None, None, :]
    ones = jnp.ones((1, 1, 1, GROUPS_PER_SLAB, 1), F32)
    return (eo * ei * ones).reshape(n_outer * n_inner, n_outer * GROUPS_PER_SLAB * n_inner)


def _s5_operators(a_re, a_im, log_dt, b_re, b_im, c_re, c_im, d_skip):
    hp = lax.Precision.HIGHEST
    a_re, a_im, b_re, b_im, c_re, c_im = (v.astype(F32) for v in (a_re, a_im, b_re, b_im, c_re, c_im))
    dt = jnp.exp(log_dt.astype(F32))[:, None]
    z_re = a_re * dt
    z_im = a_im * dt

    def lam_pow(n):
        mag = jnp.exp(z_re[None] * n[:, None, None])
        ang = z_im[None] * n[:, None, None]
        return mag * jnp.cos(ang), mag * jnp.sin(ang)

    steps = jnp.arange(S5_CHUNK + 1, dtype=F32)
    pw_re, pw_im = lam_pow(steps)
    lb_re, lb_im = pw_re[1], pw_im[1]
    den = a_re * a_re + a_im * a_im
    q_re = ((lb_re - 1.0) * a_re + lb_im * a_im) / den
    q_im = (lb_im * a_re - (lb_re - 1.0) * a_im) / den
    bb_re = q_re[:, :, None] * b_re - q_im[:, :, None] * b_im
    bb_im = q_re[:, :, None] * b_im + q_im[:, :, None] * b_re

    cp_re = c_re[None] * pw_re[:S5_CHUNK, :, None, :] - c_im[None] * pw_im[:S5_CHUNK, :, None, :]
    cp_im = c_re[None] * pw_im[:S5_CHUNK, :, None, :] + c_im[None] * pw_re[:S5_CHUNK, :, None, :]
    m = jnp.einsum("ngcq,gqd->ngcd", jnp.concatenate([cp_re, -cp_im], axis=3),
                   jnp.concatenate([bb_re, bb_im], axis=1), precision=hp)
    m_ext = jnp.concatenate([m, jnp.zeros_like(m[:1])], axis=0)
    n_blk = S5_CHUNK // 2
    d_i = (n_blk - 1 - jnp.arange(n_blk))[:, None, None]
    s_i = jnp.arange(2)[None, :, None]
    t_i = jnp.arange(2)[None, None, :]
    lag = 2 * d_i + t_i - s_i
    mg = _by_slab(m_ext[lag], 3)
    wr = mg.transpose(3, 0, 1, 4, 6, 2, 5).reshape(N_SLABS, CHUNK_COLS, 2 * S5_GROUP)

    pr_re, pr_im = lam_pow(S5_CHUNK - 1 - steps[:S5_CHUNK])
    w_re = pr_re[:, :, :, None] * bb_re[None] - pr_im[:, :, :, None] * bb_im[None]
    w_im = pr_re[:, :, :, None] * bb_im[None] + pr_im[:, :, :, None] * bb_re[None]

    def e_half(v):
        return _by_slab(v, 1).transpose(1, 0, 2, 4, 3).reshape(N_SLABS, CHUNK_COLS, S5_STATE)

    we = jnp.concatenate([e_half(w_re), e_half(w_im)], axis=2)

    g_re = c_re[None] * pw_re[1:, :, None, :] - c_im[None] * pw_im[1:, :, None, :]
    g_im = c_re[None] * pw_im[1:, :, None, :] + c_im[None] * pw_re[1:, :, None, :]

    def f_half(v):
        return _by_slab(v, 1).transpose(1, 4, 0, 2, 3).reshape(N_SLABS, S5_STATE, CHUNK_COLS)

    wf = jnp.concatenate([f_half(g_re), f_half(-g_im)], axis=1)

    def slab_vec(v):
        return v.reshape(N_SLABS, 1, SLAB_STATE)

    d_slab = d_skip.astype(F32).reshape(N_SLABS, 1, LANES)
    tile_e = _replicate_over_groups(2, S5_STATE)

    def bd_half(v):
        return _group_diag(_by_slab(v, 0).transpose(0, 1, 3, 2), 1, 3).reshape(N_SLABS, LANES, SLAB_STATE)

    def cd_half(v):
        return _group_diag(_by_slab(v, 0).transpose(0, 1, 3, 2), 1, 3).reshape(N_SLABS, SLAB_STATE, LANES)

    bd = jnp.concatenate([bd_half(bb_re), bd_half(bb_im)], axis=2)
    cd = jnp.concatenate([cd_half(c_re), cd_half(-c_im)], axis=1)
    bd_hi = bd.astype(BF16)
    bd_lo = (bd - bd_hi.astype(F32)).astype(BF16)
    return {
        "wr": wr.astype(BF16), "we": we.astype(BF16), "wf": wf.astype(BF16),
        "tile_r": _replicate_over_groups(2, S5_GROUP).astype(BF16),
        "tile_e": tile_e.astype(BF16), "tile_f": tile_e.T.astype(BF16),
        "a_re": slab_vec(pw_re[S5_CHUNK]), "a_im": slab_vec(pw_im[S5_CHUNK]),
        "d_chunk": jnp.tile(d_slab, (1, 1, S5_CHUNK)), "d_slab": d_slab,
        "bd_hi": bd_hi, "bd_lo": bd_lo, "cd": cd.astype(BF16),
        "l_re": slab_vec(lb_re), "l_im": slab_vec(lb_im),
    }


def _layer_params(w_in, w_uq, w_uk, w_uv, w_glu, w_out, norm_mix, norm_q, norm_kv, norm_attn_out, norm_ssm_out,
                  b_glu):
    c0, c1, c2 = Q_LORA, Q_LORA + KV_LORA, Q_LORA + KV_LORA + QK_ROPE
    w_uq_h = w_uq.reshape(Q_LORA, MLA_HEADS, QK_NOPE + QK_ROPE)
    return {
        "w_q": w_in[:, :c0].astype(BF16),
        "w_kv": w_in[:, c0:c1].astype(BF16),
        "w_kr_t": w_in[:, c1:c2].T.astype(BF16),
        "w_u": w_in[:, c2:].astype(BF16),
        "w_uq_nope": w_uq_h[:, :, :QK_NOPE].reshape(Q_LORA, MLA_HEADS * QK_NOPE).astype(BF16),
        "w_uq_rope_t": w_uq_h[:, :, QK_NOPE:].reshape(Q_LORA, MLA_HEADS * QK_ROPE).T.astype(BF16),
        "w_uk_h": w_uk.transpose(1, 0, 2).astype(BF16),
        "w_uv_h": w_uv.transpose(1, 2, 0).astype(BF16),
        "w_glu": w_glu.astype(BF16),
        "w_out_attn": w_out[:MLA_WIDTH].astype(BF16),
        "w_out_ssm": w_out[MLA_WIDTH:].astype(BF16),
        "norm_mix": norm_mix.reshape(1, D_MODEL),
        "norm_q": norm_q.reshape(1, Q_LORA),
        "norm_kv": norm_kv.reshape(1, KV_LORA),
        "norm_attn_out": norm_attn_out.reshape(1, MLA_WIDTH),
        "norm_attn_out_col": norm_attn_out.reshape(MLA_WIDTH, 1),
        "norm_ssm_out": norm_ssm_out.reshape(1, S5_WIDTH),
        "b_glu": b_glu.reshape(1, S5_WIDTH),
    }


def kernel(x_prompt, x_sample, c_prompt, c_sample, cache_ckv, cache_krope, state_s5_re, state_s5_im, page_table, w_ada, b_ada, norm_ffn1, ffn1_w1, ffn1_w3, ffn1_w2, norm_mix, w_in, norm_q, w_uq, norm_kv, w_uk, w_uv, s5_a_re, s5_a_im, s5_log_dt, s5_b_re, s5_b_im, s5_c_re, s5_c_im, s5_d, w_glu, b_glu, norm_attn_out, norm_ssm_out, w_out, norm_ffn2, ffn2_w1, ffn2_w3, ffn2_w2, norm_final):
    bp, seq, _ = x_prompt.shape
    bs = x_sample.shape[0]
    depth = w_ada.shape[0]
    assert depth == 1 and x_sample.shape[1] == 1
    n_pages = page_table.shape[1]
    past_len = n_pages * PAGE
    l = 0

    pad = (-(bs + bp)) % 8
    c_all = jnp.concatenate([c_sample, c_prompt, jnp.zeros((pad, D_MODEL), F32)], axis=0)
    mod = _ada(c_all, w_ada[l], b_ada[l])
    mod_s = mod[:bs].reshape(1, bs, ADA_CHUNKS * D_MODEL)
    mod_p = mod[bs:bs + bp].reshape(bp, 1, ADA_CHUNKS * D_MODEL)

    p = _layer_params(w_in[l], w_uq[l], w_uk[l], w_uv[l], w_glu[l], w_out[l], norm_mix[l], norm_q[l], norm_kv[l],
                      norm_attn_out[l], norm_ssm_out[l], b_glu[l])
    ops = _s5_operators(s5_a_re[l], s5_a_im[l], s5_log_dt[l], s5_b_re[l], s5_b_im[l], s5_c_re[l], s5_c_im[l],
                        s5_d[l])
    f1 = (ffn1_w1[l].astype(BF16), ffn1_w3[l].astype(BF16), ffn1_w2[l].astype(BF16))
    f2 = (ffn2_w1[l].astype(BF16), ffn2_w3[l].astype(BF16), ffn2_w2[l].astype(BF16))
    cos_p, sin_p = _rope_tables(jnp.arange(seq))
    cos_s, sin_s = _rope_tables(jnp.full((bs,), past_len))

    tm_p = 512
    xp = x_prompt.reshape(bp * seq, D_MODEL)
    xp = _ffn(xp, mod_p, 0, seq // tm_p, norm_ffn1[l], *f1, tm=tm_p)
    tm_mix = 512
    q_lat_t, q_rope_t, ckv_p, kc_p, kct_p, krope_t_p, kr_p, u_p = _mixin(xp, mod_p, seq // tm_mix, seq, p, cos_p,
                                                                         sin_p, tm=tm_mix, tq=256)
    attn_p = _prompt_attn(q_lat_t, q_rope_t, kc_p, kr_p, kct_p, p["w_uv_h"], p["norm_attn_out_col"])
    y_p, hre_p, him_p = _s5_prompt(u_p, ops)
    xp = _mixout(xp, attn_p.reshape(bp * seq, MLA_WIDTH), y_p.reshape(bp * seq, S5_WIDTH), mod_p, seq // tm_mix, p,
                 tm=tm_mix)
    y_prompt = _ffn(xp, mod_p, 6, seq // tm_p, norm_ffn2[l], *f2, norm_final, tm=tm_p).reshape(bp, seq, D_MODEL)

    xs = x_sample.reshape(bs, D_MODEL)
    xs = _ffn(xs, mod_s, 0, 1, norm_ffn1[l], *f1, tm=bs)
    q_lat_ts, q_rope_ts, ckv_s, _, _, krope_t_s, _, u_s = _mixin(xs, mod_s, 1, bs, p, cos_s, sin_s, tm=bs, tq=bs)
    krope_s = krope_t_s[0].T
    q_lat_s = q_lat_ts.reshape(KV_LORA, MLA_HEADS, bs).transpose(2, 1, 0)
    q_rope_s = q_rope_ts.reshape(QK_ROPE, MLA_HEADS, bs).transpose(2, 1, 0)
    o_lat = _sample_attn(page_table, q_lat_s, q_rope_s,
                         ckv_s.reshape(bs, 1, KV_LORA), krope_s.reshape(bs, 1, QK_ROPE), cache_ckv[l:l + 1],
                         cache_krope[l:l + 1].transpose(0, 1, 3, 2))
    attn_s = _sample_attn_out(o_lat.transpose(1, 0, 2), p["w_uv_h"], p["norm_attn_out"])
    y_s, hre_s, him_s = _s5_sample(u_s.reshape(bs, S5_WIDTH), state_s5_re[l], state_s5_im[l], ops)
    xs = _mixout(xs, attn_s, y_s, mod_s, 1, p, tm=bs)
    y_sample = _ffn(xs, mod_s, 6, 1, norm_ffn2[l], *f2, norm_final, tm=bs).reshape(bs, 1, D_MODEL)

    return (y_prompt, y_sample,
            ckv_p[None], krope_t_p.transpose(0, 2, 1)[None],
            ckv_s.reshape(1, bs, 1, KV_LORA), krope_s.reshape(1, bs, 1, QK_ROPE),
            hre_p[None], him_p[None], hre_s[None], him_s[None])
```

```python
import functools

import jax
import jax.numpy as jnp
from jax import lax
from jax.experimental import pallas as pl
from jax.experimental.pallas import tpu as pltpu

F32 = jnp.float32
BF16 = jnp.bfloat16

D_MODEL = 2048
D_FF = 5632
MLA_HEADS = 8
QK_NOPE = 128
QK_ROPE = 64
V_HEAD = 128
Q_LORA = 512
KV_LORA = 256
MLA_WIDTH = 1024
S5_WIDTH = 1024
S5_GROUP = 16
S5_GROUPS = 64
S5_STATE = 64
ADA_CHUNKS = 9
PAGE = 128
ROPE_THETA = 10000.0
SOFTMAX_SCALE = (QK_NOPE + QK_ROPE) ** -0.5
EPS = 1e-6

LANES = 128
S5_CHUNK = 16
GROUPS_PER_SLAB = LANES // S5_GROUP
N_SLABS = S5_WIDTH // LANES
SLAB_STATE = GROUPS_PER_SLAB * S5_STATE
CHUNK_COLS = S5_CHUNK * LANES
MXU_TILE = 2 * LANES
VMEM_LIMIT = 56 * 1024 * 1024


def _cparams(sem):
    return pltpu.CompilerParams(dimension_semantics=sem, vmem_limit_bytes=VMEM_LIMIT)


def _rms(x, g):
    return x * lax.rsqrt(jnp.mean(x * x, axis=-1, keepdims=True) + EPS) * g


def _rms_rows(x, g):
    return x * lax.rsqrt(jnp.mean(x * x, axis=0, keepdims=True) + EPS) * g


def _dot(a, b):
    return jnp.dot(a, b, preferred_element_type=F32)


def _dot_nt(a, b):
    return lax.dot_general(a, b, (((1,), (1,)), ((), ())), preferred_element_type=F32)


def _ada_kernel(c_ref, w_ref, b_ref, o_ref):
    c = c_ref[...]
    a = (c * jax.nn.sigmoid(c)).astype(BF16)
    o_ref[...] = _dot(a, w_ref[...].astype(BF16)) + b_ref[...]


def _ada(c, w_ada, b_ada):
    rows = c.shape[0]
    n = w_ada.shape[1]
    tn = 1024
    return pl.pallas_call(
        _ada_kernel,
        grid=(n // tn,),
        in_specs=[
            pl.BlockSpec((rows, D_MODEL), lambda j: (0, 0)),
            pl.BlockSpec((D_MODEL, tn), lambda j: (0, j)),
            pl.BlockSpec((1, tn), lambda j: (0, j)),
        ],
        out_specs=pl.BlockSpec((rows, tn), lambda j: (0, j)),
        out_shape=jax.ShapeDtypeStruct((rows, n), F32),
        compiler_params=_cparams(("arbitrary",)),
        name="ada_modulation",
    )(c, w_ada, b_ada.reshape(1, n))


def _ffn_kernel(x_ref, sh_ref, sc_ref, gt_ref, g_ref, w1_ref, w3_ref, w2_ref, *rest, final_norm):
    if final_norm:
        gf_ref, o_ref, h_ref = rest
    else:
        o_ref, h_ref = rest
    j = pl.program_id(1)

    @pl.when(j == 0)
    def _():
        h = _rms(x_ref[...], g_ref[...]) * (1.0 + sc_ref[0]) + sh_ref[0]
        h_ref[...] = h.astype(BF16)
        o_ref[...] = jnp.zeros_like(o_ref)

    h = h_ref[...]
    a = _dot(h, w1_ref[...])
    b = _dot(h, w3_ref[...])
    act = (a * jax.nn.sigmoid(a) * b).astype(BF16)
    o_ref[...] += _dot(act, w2_ref[...])

    @pl.when(j == pl.num_programs(1) - 1)
    def _():
        y = x_ref[...] + 0.5 * gt_ref[0] * o_ref[...]
        if final_norm:
            y = _rms(y, gf_ref[...])
        o_ref[...] = y


def _ffn(x, mod, chunk0, tiles_per_b, g, w1, w3, w2, gf=None, *, tm, tf=512):
    n = x.shape[0]
    r = mod.shape[1]

    def mod_spec(k):
        return pl.BlockSpec((1, r, D_MODEL), lambda i, j: (i // tiles_per_b, 0, k))

    in_specs = [
        pl.BlockSpec((tm, D_MODEL), lambda i, j: (i, 0)),
        mod_spec(chunk0), mod_spec(chunk0 + 1), mod_spec(chunk0 + 2),
        pl.BlockSpec((1, D_MODEL), lambda i, j: (0, 0)),
        pl.BlockSpec((D_MODEL, tf), lambda i, j: (0, j)),
        pl.BlockSpec((D_MODEL, tf), lambda i, j: (0, j)),
        pl.BlockSpec((tf, D_MODEL), lambda i, j: (j, 0)),
    ]
    args = [x, mod, mod, mod, g.reshape(1, D_MODEL), w1, w3, w2]
    if gf is not None:
        in_specs.append(pl.BlockSpec((1, D_MODEL), lambda i, j: (0, 0)))
        args.append(gf.reshape(1, D_MODEL))
    return pl.pallas_call(
        functools.partial(_ffn_kernel, final_norm=gf is not None),
        grid=(n // tm, D_FF // tf),
        in_specs=in_specs,
        out_specs=pl.BlockSpec((tm, D_MODEL), lambda i, j: (i, 0)),
        out_shape=jax.ShapeDtypeStruct((n, D_MODEL), F32),
        scratch_shapes=[pltpu.VMEM((tm, D_MODEL), BF16)],
        compiler_params=_cparams(("parallel", "arbitrary")),
        name="macaron_ffn",
    )(*args)


def _rope_rows(x, cos, sin):
    half = QK_ROPE // 2
    x1 = x[:half]
    x2 = x[half:]
    return x1 * cos - x2 * sin, x1 * sin + x2 * cos


def _mixin_kernel(x_ref, sh_ref, sc_ref, g_ref, wq_ref, wkv_ref, wkr_ref, wu_ref, gq_ref, wqn_ref, wqr_ref,
                  wuk_ref, gkv_ref, cos_ref, sin_ref,
                  qlat_ref, qrope_ref, ckv_ref, kc_ref, kct_ref, krope_ref, kr_ref, u_ref, *, tq):
    half = QK_ROPE // 2
    n_q = x_ref.shape[0] // tq
    h = (_rms(x_ref[...], g_ref[...]) * (1.0 + sc_ref[0]) + sh_ref[0]).astype(BF16)
    cos = cos_ref[...]
    sin = sin_ref[...]

    u_ref[0] = _dot(h, wu_ref[...])

    ckv = _rms(_dot(h, wkv_ref[...]), gkv_ref[...])
    ckv_ref[0] = ckv
    kc_ref[0] = ckv.astype(BF16)
    kct_ref[0] = ckv.T.astype(BF16)

    k1, k2 = _rope_rows(_dot_nt(wkr_ref[...], h), cos, sin)
    kr_t = jnp.concatenate([k1, k2], axis=0)
    krope_ref[0] = kr_t
    kr_ref[0] = kr_t.T.astype(BF16)

    qn = _rms(_dot(h, wq_ref[...]), gq_ref[...]).astype(BF16)
    q_nope = (_dot(qn, wqn_ref[...]) * SOFTMAX_SCALE).astype(BF16)
    q_rope_t = _dot_nt(wqr_ref[...], qn) * SOFTMAX_SCALE
    for hd in range(MLA_HEADS):
        ql_t = _dot_nt(wuk_ref[hd], q_nope[:, hd * QK_NOPE:(hd + 1) * QK_NOPE]).astype(BF16)
        r1, r2 = _rope_rows(q_rope_t[hd * QK_ROPE:(hd + 1) * QK_ROPE], cos, sin)
        r1 = r1.astype(BF16)
        r2 = r2.astype(BF16)
        for qq in range(n_q):
            src = slice(qq * tq, (qq + 1) * tq)
            dst = slice(hd * tq, (hd + 1) * tq)
            qlat_ref[0, qq, :, dst] = ql_t[:, src]
            qrope_ref[0, qq, :half, dst] = r1[:, src]
            qrope_ref[0, qq, half:, dst] = r2[:, src]


def _mixin(x, mod, tiles_per_b, t_len, p, cos_t, sin_t, *, tm, tq):
    n = x.shape[0]
    nb = n // t_len
    r = mod.shape[1]
    half = QK_ROPE // 2
    full = lambda shape: pl.BlockSpec(shape, lambda i: (0,) * len(shape))
    rows = lambda width: pl.BlockSpec((1, tm, width), lambda i: (i // tiles_per_b, i % tiles_per_b, 0))
    cols = lambda height: pl.BlockSpec((1, height, tm), lambda i: (i // tiles_per_b, 0, i % tiles_per_b))
    head_cols = lambda height: pl.BlockSpec((1, tm // tq, height, MLA_HEADS * tq),
                                            lambda i: (i // tiles_per_b, i % tiles_per_b, 0, 0))
    in_specs = [
        pl.BlockSpec((tm, D_MODEL), lambda i: (i, 0)),
        pl.BlockSpec((1, r, D_MODEL), lambda i: (i // tiles_per_b, 0, 3)),
        pl.BlockSpec((1, r, D_MODEL), lambda i: (i // tiles_per_b, 0, 4)),
        full((1, D_MODEL)),
        full((D_MODEL, Q_LORA)), full((D_MODEL, KV_LORA)), full((QK_ROPE, D_MODEL)), full((D_MODEL, S5_WIDTH)),
        full((1, Q_LORA)), full((Q_LORA, MLA_HEADS * QK_NOPE)), full((MLA_HEADS * QK_ROPE, Q_LORA)),
        full((MLA_HEADS, KV_LORA, QK_NOPE)), full((1, KV_LORA)),
        pl.BlockSpec((half, tm), lambda i: (0, i % tiles_per_b)),
        pl.BlockSpec((half, tm), lambda i: (0, i % tiles_per_b)),
    ]
    out_specs = [head_cols(KV_LORA), head_cols(QK_ROPE), rows(KV_LORA), rows(KV_LORA), cols(KV_LORA),
                 cols(QK_ROPE), rows(QK_ROPE), rows(S5_WIDTH)]
    out_shape = [
        jax.ShapeDtypeStruct((nb, t_len // tq, KV_LORA, MLA_HEADS * tq), BF16),
        jax.ShapeDtypeStruct((nb, t_len // tq, QK_ROPE, MLA_HEADS * tq), BF16),
        jax.ShapeDtypeStruct((nb, t_len, KV_LORA), F32),
        jax.ShapeDtypeStruct((nb, t_len, KV_LORA), BF16),
        jax.ShapeDtypeStruct((nb, KV_LORA, t_len), BF16),
        jax.ShapeDtypeStruct((nb, QK_ROPE, t_len), F32),
        jax.ShapeDtypeStruct((nb, t_len, QK_ROPE), BF16),
        jax.ShapeDtypeStruct((nb, t_len, S5_WIDTH), F32),
    ]
    return pl.pallas_call(
        functools.partial(_mixin_kernel, tq=tq),
        grid=(n // tm,),
        in_specs=in_specs,
        out_specs=out_specs,
        out_shape=out_shape,
        compiler_params=_cparams(("parallel",)),
        name="mixer_in",
    )(x, mod, mod, p["norm_mix"], p["w_q"], p["w_kv"], p["w_kr_t"], p["w_u"], p["norm_q"], p["w_uq_nope"],
      p["w_uq_rope_t"], p["w_uk_h"], p["norm_kv"], cos_t, sin_t)


def _prompt_attn_kernel(ql_ref, qr_ref, kc_ref, kr_ref, kct_ref, wuv_ref, g_ref, o_ref, *scratch, tq, tk):
    qi = pl.program_id(1)
    n_half = len(scratch) // 3
    m_ref, l_ref, acc_ref = scratch[:n_half], scratch[n_half:2 * n_half], scratch[2 * n_half:]
    width = m_ref[0].shape[1]
    heads_per_half = width // tq
    for hf in range(n_half):
        m_ref[hf][...] = jnp.full_like(m_ref[hf], -jnp.inf)
        l_ref[hf][...] = jnp.zeros_like(l_ref[hf])
        acc_ref[hf][...] = jnp.zeros_like(acc_ref[hf])

    def step(ki, diag):
        start = pl.multiple_of(ki * tk, tk)
        kc = kc_ref[0, pl.ds(start, tk), :]
        kr = kr_ref[0, pl.ds(start, tk), :]
        kct = kct_ref[0, :, pl.ds(start, tk)]
        masked = diag is not None
        if masked:
            k_local = lax.broadcasted_iota(jnp.int32, (tk, width), 0) + diag * tk
            t_local = lax.broadcasted_iota(jnp.int32, (tk, width), 1) & (tq - 1)
            keep = k_local <= t_local
        for hf in range(n_half):
            lanes = slice(hf * width, (hf + 1) * width)
            s = _dot(kc, ql_ref[0, 0, :, lanes]) + _dot(kr, qr_ref[0, 0, :, lanes])
            if masked:
                s = jnp.where(keep, s, -jnp.inf)
            m_old = m_ref[hf][...]
            m_new = jnp.maximum(m_old, jnp.max(s, axis=0, keepdims=True))
            alpha = jnp.exp(m_old - m_new)
            pexp = jnp.exp(s - m_new)
            l_ref[hf][...] = alpha * l_ref[hf][...] + jnp.sum(pexp, axis=0, keepdims=True)
            acc_ref[hf][...] = alpha * acc_ref[hf][...] + _dot(kct, pexp.astype(BF16))
            m_ref[hf][...] = m_new

    def body(ki, carry):
        step(ki, None)
        return carry

    blocks_per_tile = tq // tk
    lax.fori_loop(0, qi * blocks_per_tile, body, 0)
    for d in range(blocks_per_tile):
        step(qi * blocks_per_tile + d, d)
    parts = []
    for hd in range(MLA_HEADS):
        hf = hd // heads_per_half
        lanes = slice((hd % heads_per_half) * tq, (hd % heads_per_half + 1) * tq)
        o_t = acc_ref[hf][:, lanes] / l_ref[hf][:, lanes]
        parts.append(_dot(wuv_ref[hd], o_t.astype(BF16)))
    attn_t = _rms_rows(jnp.concatenate(parts, axis=0), g_ref[...])
    o_ref[0] = attn_t.T.astype(BF16)


def _prompt_attn(q_lat_t, q_rope_t, kc, kr, kc_t, w_uv_h, g_col, *, tq, tk=256, n_half=1):
    nb, n_q, _, _ = q_lat_t.shape
    t_len = n_q * tq
    width = MLA_HEADS * tq // n_half
    assert tq % tk == 0
    return pl.pallas_call(
        functools.partial(_prompt_attn_kernel, tq=tq, tk=tk),
        grid=(nb, n_q),
        in_specs=[
            pl.BlockSpec((1, 1, KV_LORA, MLA_HEADS * tq), lambda b, i: (b, i, 0, 0)),
            pl.BlockSpec((1, 1, QK_ROPE, MLA_HEADS * tq), lambda b, i: (b, i, 0, 0)),
            pl.BlockSpec((1, t_len, KV_LORA), lambda b, i: (b, 0, 0)),
            pl.BlockSpec((1, t_len, QK_ROPE), lambda b, i: (b, 0, 0)),
            pl.BlockSpec((1, KV_LORA, t_len), lambda b, i: (b, 0, 0)),
            pl.BlockSpec((MLA_HEADS, V_HEAD, KV_LORA), lambda b, i: (0, 0, 0)),
            pl.BlockSpec((MLA_WIDTH, 1), lambda b, i: (0, 0)),
        ],
        out_specs=pl.BlockSpec((1, tq, MLA_WIDTH), lambda b, i: (b, i, 0)),
        out_shape=jax.ShapeDtypeStruct((nb, t_len, MLA_WIDTH), BF16),
        scratch_shapes=([pltpu.VMEM((1, width), F32)] * (2 * n_half) + [pltpu.VMEM((KV_LORA, width), F32)] * n_half),
        compiler_params=_cparams(("parallel", "parallel")),
        name="prompt_attention",
    )(q_lat_t, q_rope_t, kc, kr, kc_t, w_uv_h, g_col)


def _sample_attn_kernel(pt_ref, ql_ref, qr_ref, kcn_ref, krn_ref, ckv_hbm, kr_hbm, o_ref, kbuf, rbuf, sem_k, sem_r,
                        *, pages, streams):
    nb, n_pages = pt_ref.shape
    n_chunks = n_pages // pages
    n_groups = nb // streams
    total = n_groups * n_chunks
    n_slots = kbuf.shape[0]
    ahead = n_slots - 1

    def chunk_copies(g, slot):
        grp = g // n_chunks
        c = g % n_chunks
        copies = []
        for st in range(streams):
            b = grp * streams + st
            for i in range(pages):
                page = pt_ref[b, c * pages + i]
                copies.append(pltpu.make_async_copy(
                    ckv_hbm.at[0, page], kbuf.at[slot, st, pl.ds(i * PAGE, PAGE), :], sem_k.at[slot, st]))
                copies.append(pltpu.make_async_copy(
                    kr_hbm.at[0, page], rbuf.at[slot, st, i], sem_r.at[slot, st]))
        return copies

    grp = pl.program_id(0)

    @pl.when(grp == 0)
    def _():
        for g0 in range(ahead):
            for cp in chunk_copies(g0, g0):
                cp.start()

    qs = [(ql_ref[st], qr_ref[st]) for st in range(streams)]

    def chunk_body(c, carry):
        g = grp * n_chunks + c
        slot = g % n_slots

        @pl.when(g + ahead < total)
        def _():
            for cp in chunk_copies(g + ahead, (g + ahead) % n_slots):
                cp.start()

        for cp in chunk_copies(g, slot):
            cp.wait()
        out = []
        for st in range(streams):
            m_old, l_old, acc_old = carry[st]
            ql, qr = qs[st]
            kc = kbuf[slot, st].astype(BF16)
            kr_t = jnp.concatenate([rbuf[slot, st, i].astype(BF16) for i in range(pages)], axis=1)
            s = _dot_nt(ql, kc) + _dot(qr, kr_t)
            m_new = jnp.maximum(m_old, jnp.max(s, axis=1, keepdims=True))
            alpha = jnp.exp(m_old - m_new)
            pexp = jnp.exp(s - m_new)
            l_new = alpha * l_old + jnp.sum(pexp, axis=1, keepdims=True)
            acc_new = alpha * acc_old + _dot(pexp.astype(BF16), kc)
            out.append((m_new, l_new, acc_new))
        return tuple(out)

    init = tuple((jnp.full((MLA_HEADS, 1), -jnp.inf, F32), jnp.zeros((MLA_HEADS, 1), F32),
                  jnp.zeros((MLA_HEADS, KV_LORA), F32)) for _ in range(streams))
    final = lax.fori_loop(0, n_chunks, chunk_body, init)
    for st in range(streams):
        m_old, l_old, acc_old = final[st]
        ql, qr = qs[st]
        kcn = kcn_ref[st].astype(BF16).astype(F32)
        krn = krn_ref[st].astype(BF16).astype(F32)
        s_n = (jnp.sum(ql.astype(F32) * kcn, axis=1, keepdims=True)
               + jnp.sum(qr.astype(F32) * krn, axis=1, keepdims=True))
        m_f = jnp.maximum(m_old, s_n)
        a_f = jnp.exp(m_old - m_f)
        p_n = jnp.exp(s_n - m_f)
        l_f = a_f * l_old + p_n
        acc_f = a_f * acc_old + p_n.astype(BF16).astype(F32) * kcn
        o_ref[st] = acc_f / l_f


def _sample_attn(page_table, q_lat, q_rope, ckv_new, krope_new, cache_ckv, cache_krope_t, *, pages=16, streams=2,
                 slots=3):
    nb, n_pages = page_table.shape
    assert n_pages % pages == 0 and nb % streams == 0 and (nb // streams) * (n_pages // pages) >= slots
    per_group = lambda rows, width: pl.BlockSpec((streams, rows, width), lambda g, pt: (g, 0, 0))
    grid_spec = pltpu.PrefetchScalarGridSpec(
        num_scalar_prefetch=1,
        grid=(nb // streams,),
        in_specs=[
            per_group(MLA_HEADS, KV_LORA), per_group(MLA_HEADS, QK_ROPE),
            per_group(1, KV_LORA), per_group(1, QK_ROPE),
            pl.BlockSpec(memory_space=pl.ANY), pl.BlockSpec(memory_space=pl.ANY),
        ],
        out_specs=per_group(MLA_HEADS, KV_LORA),
        scratch_shapes=[
            pltpu.VMEM((slots, streams, pages * PAGE, KV_LORA), F32),
            pltpu.VMEM((slots, streams, pages, QK_ROPE, PAGE), F32),
            pltpu.SemaphoreType.DMA((slots, streams)),
            pltpu.SemaphoreType.DMA((slots, streams)),
        ],
    )
    return pl.pallas_call(
        functools.partial(_sample_attn_kernel, pages=pages, streams=streams),
        grid_spec=grid_spec,
        out_shape=jax.ShapeDtypeStruct((nb, MLA_HEADS, KV_LORA), F32),
        compiler_params=_cparams(("arbitrary",)),
        name="sample_paged_attention",
    )(page_table, q_lat, q_rope, ckv_new, krope_new, cache_ckv, cache_krope_t)


def _sample_attn_out_kernel(o_ref, wuv_ref, g_ref, out_ref):
    parts = [_dot_nt(o_ref[hd].astype(BF16), wuv_ref[hd]) for hd in range(MLA_HEADS)]
    out_ref[...] = _rms(jnp.concatenate(parts, axis=1), g_ref[...]).astype(BF16)


def _sample_attn_out(o_lat_h, w_uv_h, g):
    nb = o_lat_h.shape[1]
    return pl.pallas_call(
        _sample_attn_out_kernel,
        out_shape=jax.ShapeDtypeStruct((nb, MLA_WIDTH), BF16),
        name="sample_attention_out",
    )(o_lat_h, w_uv_h, g)


def _same_group(shape, row_shift, col_shift):
    rows = lax.broadcasted_iota(jnp.int32, shape, 0)
    cols = lax.broadcasted_iota(jnp.int32, shape, 1)
    mask = GROUPS_PER_SLAB - 1
    return ((rows >> row_shift) & mask) == ((cols >> col_shift) & mask)


def _s5_prompt_kernel(u_ref, wr_ref, we_ref, wf_ref, tr_ref, te_ref, tf_ref, are_ref, aim_ref, d_ref,
                      y_ref, hre_ref, him_ref, r_ref, e_ref, f_ref, uc_ref, s_ref, hp_ref, yc_ref):
    n_chunks = uc_ref.shape[0]
    n_blk = CHUNK_COLS // MXU_TILE
    ch_bits = S5_GROUP.bit_length() - 1
    st_bits = S5_STATE.bit_length() - 1

    @pl.when(pl.program_id(1) == 0)
    def _():
        zero = jnp.zeros((), F32)
        r_ref[...] = jnp.where(_same_group(r_ref.shape, ch_bits, ch_bits),
                               _dot(wr_ref[0], tr_ref[...]), zero).astype(BF16)
        e_ref[...] = jnp.where(_same_group(e_ref.shape, ch_bits, st_bits),
                               _dot(we_ref[0], te_ref[...]), zero).astype(BF16)
        f_ref[...] = jnp.where(_same_group(f_ref.shape, st_bits, ch_bits),
                               _dot(tf_ref[...], wf_ref[0]), zero).astype(BF16)

    for s in range(S5_CHUNK):
        uc_ref[:, s * LANES:(s + 1) * LANES] = u_ref[0, pl.ds(s, n_chunks, stride=S5_CHUNK), :]
    uc = uc_ref[...]
    ub = uc.astype(BF16)

    s_ref[...] = _dot(ub, e_ref[...])

    a_re = are_ref[0]
    a_im = aim_ref[0]

    def scan(k, carry):
        h_re, h_im = carry
        hp_ref[pl.ds(k, 1), :] = jnp.concatenate([h_re, h_im], axis=1)
        row = s_ref[pl.ds(k, 1), :]
        n_re = a_re * h_re - a_im * h_im + row[:, :SLAB_STATE]
        n_im = a_re * h_im + a_im * h_re + row[:, SLAB_STATE:]
        return n_re, n_im

    zero_state = jnp.zeros((1, SLAB_STATE), F32)
    h_re, h_im = lax.fori_loop(0, n_chunks, scan, (zero_state, zero_state), unroll=4)
    hre_ref[0, 0] = h_re
    him_ref[0, 0] = h_im

    yc_ref[...] = _dot(hp_ref[...].astype(BF16), f_ref[...]) + uc * d_ref[0]
    for tb in range(n_blk):
        yc_ref[:, tb * MXU_TILE:(tb + 1) * MXU_TILE] += _dot(ub[:, :(tb + 1) * MXU_TILE],
                                                             r_ref[(n_blk - 1 - tb) * MXU_TILE:, :])
    y = jax.nn.gelu(yc_ref[...])
    for t in range(S5_CHUNK):
        y_ref[0, pl.ds(t, n_chunks, stride=S5_CHUNK), :] = y[:, t * LANES:(t + 1) * LANES]


def _s5_prompt(u, ops):
    nb, t_len, _ = u.shape
    n_chunks = t_len // S5_CHUNK
    slab = lambda shape: pl.BlockSpec((1,) + shape, lambda j, b: (j,) + (0,) * len(shape))
    full = lambda shape: pl.BlockSpec(shape, lambda j, b: (0,) * len(shape))
    y, hre, him = pl.pallas_call(
        _s5_prompt_kernel,
        grid=(N_SLABS, nb),
        in_specs=[
            pl.BlockSpec((1, t_len, LANES), lambda j, b: (b, 0, j)),
            slab((CHUNK_COLS, 2 * S5_GROUP)), slab((CHUNK_COLS, 2 * S5_STATE)), slab((2 * S5_STATE, CHUNK_COLS)),
            full((2 * S5_GROUP, MXU_TILE)), full((2 * S5_STATE, 2 * SLAB_STATE)), full((2 * SLAB_STATE, 2 * S5_STATE)),
            slab((1, SLAB_STATE)), slab((1, SLAB_STATE)), slab((1, CHUNK_COLS)),
        ],
        out_specs=[
            pl.BlockSpec((1, t_len, LANES), lambda j, b: (b, 0, j)),
            pl.BlockSpec((1, 1, 1, SLAB_STATE), lambda j, b: (j, b, 0, 0)),
            pl.BlockSpec((1, 1, 1, SLAB_STATE), lambda j, b: (j, b, 0, 0)),
        ],
        out_shape=[
            jax.ShapeDtypeStruct((nb, t_len, S5_WIDTH), F32),
            jax.ShapeDtypeStruct((N_SLABS, nb, 1, SLAB_STATE), F32),
            jax.ShapeDtypeStruct((N_SLABS, nb, 1, SLAB_STATE), F32),
        ],
        scratch_shapes=[
            pltpu.VMEM((CHUNK_COLS, MXU_TILE), BF16),
            pltpu.VMEM((CHUNK_COLS, 2 * SLAB_STATE), BF16),
            pltpu.VMEM((2 * SLAB_STATE, CHUNK_COLS), BF16),
            pltpu.VMEM((n_chunks, CHUNK_COLS), F32),
            pltpu.VMEM((n_chunks, 2 * SLAB_STATE), F32),
            pltpu.VMEM((n_chunks, 2 * SLAB_STATE), F32),
            pltpu.VMEM((n_chunks, CHUNK_COLS), F32),
        ],
        compiler_params=_cparams(("arbitrary", "arbitrary")),
        name="s5_prompt",
    )(u, ops["wr"], ops["we"], ops["wf"], ops["tile_r"], ops["tile_e"], ops["tile_f"], ops["a_re"], ops["a_im"],
      ops["d_chunk"])

    def to_state(h):
        return h.reshape(N_SLABS, nb, GROUPS_PER_SLAB, S5_STATE).transpose(1, 0, 2, 3).reshape(nb, S5_GROUPS, S5_STATE)

    return y, to_state(hre), to_state(him)


def _split_bf16(x):
    hi = x.astype(BF16)
    return hi, (x - hi.astype(F32)).astype(BF16)


def _s5_sample_kernel(u_ref, h0re_ref, h0im_ref, bd_hi_ref, bd_lo_ref, cd_ref, lre_ref, lim_ref, d_ref,
                      y_ref, hre_ref, him_ref):
    u = u_ref[...]
    u_hi, u_lo = _split_bf16(u)
    bu = _dot(u_hi, bd_hi_ref[0]) + (_dot(u_hi, bd_lo_ref[0]) + _dot(u_lo, bd_hi_ref[0]))
    l_re = lre_ref[0]
    l_im = lim_ref[0]
    h0_re = h0re_ref[...]
    h0_im = h0im_ref[...]
    h_re = l_re * h0_re - l_im * h0_im + bu[:, :SLAB_STATE]
    h_im = l_re * h0_im + l_im * h0_re + bu[:, SLAB_STATE:]
    hre_ref[...] = h_re
    him_ref[...] = h_im
    h = jnp.concatenate([h_re, h_im], axis=1).astype(BF16)
    y_ref[...] = jax.nn.gelu(_dot(h, cd_ref[0]) + u * d_ref[0])


def _s5_sample(u, h0_re, h0_im, ops):
    nb = u.shape[0]
    n_state = S5_GROUPS * S5_STATE
    slab = lambda shape: pl.BlockSpec((1,) + shape, lambda j: (j,) + (0,) * len(shape))
    col = lambda width: pl.BlockSpec((nb, width), lambda j: (0, j))
    y, hre, him = pl.pallas_call(
        _s5_sample_kernel,
        grid=(N_SLABS,),
        in_specs=[
            col(LANES), col(SLAB_STATE), col(SLAB_STATE),
            slab((LANES, 2 * SLAB_STATE)), slab((LANES, 2 * SLAB_STATE)), slab((2 * SLAB_STATE, LANES)),
            slab((1, SLAB_STATE)), slab((1, SLAB_STATE)), slab((1, LANES)),
        ],
        out_specs=[col(LANES), col(SLAB_STATE), col(SLAB_STATE)],
        out_shape=[
            jax.ShapeDtypeStruct((nb, S5_WIDTH), F32),
            jax.ShapeDtypeStruct((nb, n_state), F32),
            jax.ShapeDtypeStruct((nb, n_state), F32),
        ],
        compiler_params=_cparams(("parallel",)),
        name="s5_sample",
    )(u, h0_re.reshape(nb, n_state), h0_im.reshape(nb, n_state), ops["bd_hi"], ops["bd_lo"], ops["cd"],
      ops["l_re"], ops["l_im"], ops["d_slab"])
    return y, hre.reshape(nb, S5_GROUPS, S5_STATE), him.reshape(nb, S5_GROUPS, S5_STATE)


def _mixout_kernel(x_ref, attn_ref, y_ref, gt_ref, wglu_ref, bglu_ref, gs_ref, woa_ref, woy_ref, o_ref):
    y = y_ref[...]
    z = _dot(y.astype(BF16), wglu_ref[...]) + bglu_ref[...]
    yn = _rms(y * jax.nn.sigmoid(z), gs_ref[...]).astype(BF16)
    mix = _dot(attn_ref[...], woa_ref[...]) + _dot(yn, woy_ref[...])
    o_ref[...] = x_ref[...] + gt_ref[0] * mix


def _mixout(x, attn, y, mod, tiles_per_b, p, *, tm):
    n = x.shape[0]
    r = mod.shape[1]
    full = lambda shape: pl.BlockSpec(shape, lambda i: (0,) * len(shape))
    return pl.pallas_call(
        _mixout_kernel,
        grid=(n // tm,),
        in_specs=[
            pl.BlockSpec((tm, D_MODEL), lambda i: (i, 0)),
            pl.BlockSpec((tm, MLA_WIDTH), lambda i: (i, 0)),
            pl.BlockSpec((tm, S5_WIDTH), lambda i: (i, 0)),
            pl.BlockSpec((1, r, D_MODEL), lambda i: (i // tiles_per_b, 0, 5)),
            full((S5_WIDTH, S5_WIDTH)), full((1, S5_WIDTH)), full((1, S5_WIDTH)),
            full((MLA_WIDTH, D_MODEL)), full((S5_WIDTH, D_MODEL)),
        ],
        out_specs=pl.BlockSpec((tm, D_MODEL), lambda i: (i, 0)),
        out_shape=jax.ShapeDtypeStruct((n, D_MODEL), F32),
        compiler_params=_cparams(("parallel",)),
        name="mixer_out",
    )(x, attn, y, mod, p["w_glu"], p["b_glu"], p["norm_ssm_out"], p["w_out_attn"], p["w_out_ssm"])


def _rope_tables(pos):
    half = QK_ROPE // 2
    inv_freq = ROPE_THETA ** (-jnp.arange(half, dtype=F32) / half)
    ang = pos.astype(F32)[:, None] * inv_freq[None, :]
    return jnp.cos(ang).T, jnp.sin(ang).T


def _group_diag(x, g_axis, new_axis):
    x = jnp.expand_dims(x, new_axis)
    shape = [1] * x.ndim
    shape[g_axis if g_axis < new_axis else g_axis + 1] = GROUPS_PER_SLAB
    shape[new_axis] = GROUPS_PER_SLAB
    return x * jnp.eye(GROUPS_PER_SLAB, dtype=x.dtype).reshape(shape)


def _by_slab(x, g_axis):
    return x.reshape(x.shape[:g_axis] + (N_SLABS, GROUPS_PER_SLAB) + x.shape[g_axis + 1:])


def _replicate_over_groups(n_outer, n_inner):
    eo = jnp.eye(n_outer, dtype=F32)[:, None, :, None, None]
    ei = jnp.eye(n_inner, dtype=F32)[None, :, None, None, :]
    ones = jnp.ones((1, 1, 1, GROUPS_PER_SLAB, 1), F32)
    return (eo * ei * ones).reshape(n_outer * n_inner, n_outer * GROUPS_PER_SLAB * n_inner)


def _s5_operators(a_re, a_im, log_dt, b_re, b_im, c_re, c_im, d_skip):
    hp = lax.Precision.HIGHEST
    a_re, a_im, b_re, b_im, c_re, c_im = (v.astype(F32) for v in (a_re, a_im, b_re, b_im, c_re, c_im))
    dt = jnp.exp(log_dt.astype(F32))[:, None]
    z_re = a_re * dt
    z_im = a_im * dt

    def lam_pow(n):
        mag = jnp.exp(z_re[None] * n[:, None, None])
        ang = z_im[None] * n[:, None, None]
        return mag * jnp.cos(ang), mag * jnp.sin(ang)

    steps = jnp.arange(S5_CHUNK + 1, dtype=F32)
    pw_re, pw_im = lam_pow(steps)
    lb_re, lb_im = pw_re[1], pw_im[1]
    den = a_re * a_re + a_im * a_im
    q_re = ((lb_re - 1.0) * a_re + lb_im * a_im) / den
    q_im = (lb_im * a_re - (lb_re - 1.0) * a_im) / den
    bb_re = q_re[:, :, None] * b_re - q_im[:, :, None] * b_im
    bb_im = q_re[:, :, None] * b_im + q_im[:, :, None] * b_re

    cp_re = c_re[None] * pw_re[:S5_CHUNK, :, None, :] - c_im[None] * pw_im[:S5_CHUNK, :, None, :]
    cp_im = c_re[None] * pw_im[:S5_CHUNK, :, None, :] + c_im[None] * pw_re[:S5_CHUNK, :, None, :]
    m = jnp.einsum("ngcq,gqd->ngcd", jnp.concatenate([cp_re, -cp_im], axis=3),
                   jnp.concatenate([bb_re, bb_im], axis=1), precision=hp)
    m_ext = jnp.concatenate([m, jnp.zeros_like(m[:1])], axis=0)
    n_blk = S5_CHUNK // 2
    d_i = (n_blk - 1 - jnp.arange(n_blk))[:, None, None]
    s_i = jnp.arange(2)[None, :, None]
    t_i = jnp.arange(2)[None, None, :]
    lag = 2 * d_i + t_i - s_i
    mg = _by_slab(m_ext[lag], 3)
    wr = mg.transpose(3, 0, 1, 4, 6, 2, 5).reshape(N_SLABS, CHUNK_COLS, 2 * S5_GROUP)

    pr_re, pr_im = lam_pow(S5_CHUNK - 1 - steps[:S5_CHUNK])
    w_re = pr_re[:, :, :, None] * bb_re[None] - pr_im[:, :, :, None] * bb_im[None]
    w_im = pr_re[:, :, :, None] * bb_im[None] + pr_im[:, :, :, None] * bb_re[None]

    def e_half(v):
        return _by_slab(v, 1).transpose(1, 0, 2, 4, 3).reshape(N_SLABS, CHUNK_COLS, S5_STATE)

    we = jnp.concatenate([e_half(w_re), e_half(w_im)], axis=2)

    g_re = c_re[None] * pw_re[1:, :, None, :] - c_im[None] * pw_im[1:, :, None, :]
    g_im = c_re[None] * pw_im[1:, :, None, :] + c_im[None] * pw_re[1:, :, None, :]

    def f_half(v):
        return _by_slab(v, 1).transpose(1, 4, 0, 2, 3).reshape(N_SLABS, S5_STATE, CHUNK_COLS)

    wf = jnp.concatenate([f_half(g_re), f_half(-g_im)], axis=1)

    def slab_vec(v):
        return v.reshape(N_SLABS, 1, SLAB_STATE)

    d_slab = d_skip.astype(F32).reshape(N_SLABS, 1, LANES)
    tile_e = _replicate_over_groups(2, S5_STATE)

    def bd_half(v):
        return _group_diag(_by_slab(v, 0).transpose(0, 1, 3, 2), 1, 3).reshape(N_SLABS, LANES, SLAB_STATE)

    def cd_half(v):
        return _group_diag(_by_slab(v, 0).transpose(0, 1, 3, 2), 1, 3).reshape(N_SLABS, SLAB_STATE, LANES)

    bd = jnp.concatenate([bd_half(bb_re), bd_half(bb_im)], axis=2)
    cd = jnp.concatenate([cd_half(c_re), cd_half(-c_im)], axis=1)
    bd_hi = bd.astype(BF16)
    bd_lo = (bd - bd_hi.astype(F32)).astype(BF16)
    return {
        "wr": wr.astype(BF16), "we": we.astype(BF16), "wf": wf.astype(BF16),
        "tile_r": _replicate_over_groups(2, S5_GROUP).astype(BF16),
        "tile_e": tile_e.astype(BF16), "tile_f": tile_e.T.astype(BF16),
        "a_re": slab_vec(pw_re[S5_CHUNK]), "a_im": slab_vec(pw_im[S5_CHUNK]),
        "d_chunk": jnp.tile(d_slab, (1, 1, S5_CHUNK)), "d_slab": d_slab,
        "bd_hi": bd_hi, "bd_lo": bd_lo, "cd": cd.astype(BF16),
        "l_re": slab_vec(lb_re), "l_im": slab_vec(lb_im),
    }


def _layer_params(w_in, w_uq, w_uk, w_uv, w_glu, w_out, norm_mix, norm_q, norm_kv, norm_attn_out, norm_ssm_out,
                  b_glu):
    c0, c1, c2 = Q_LORA, Q_LORA + KV_LORA, Q_LORA + KV_LORA + QK_ROPE
    w_uq_h = w_uq.reshape(Q_LORA, MLA_HEADS, QK_NOPE + QK_ROPE)
    return {
        "w_q": w_in[:, :c0].astype(BF16),
        "w_kv": w_in[:, c0:c1].astype(BF16),
        "w_kr_t": w_in[:, c1:c2].T.astype(BF16),
        "w_u": w_in[:, c2:].astype(BF16),
        "w_uq_nope": w_uq_h[:, :, :QK_NOPE].reshape(Q_LORA, MLA_HEADS * QK_NOPE).astype(BF16),
        "w_uq_rope_t": w_uq_h[:, :, QK_NOPE:].reshape(Q_LORA, MLA_HEADS * QK_ROPE).T.astype(BF16),
        "w_uk_h": w_uk.transpose(1, 0, 2).astype(BF16),
        "w_uv_h": w_uv.transpose(1, 2, 0).astype(BF16),
        "w_glu": w_glu.astype(BF16),
        "w_out_attn": w_out[:MLA_WIDTH].astype(BF16),
        "w_out_ssm": w_out[MLA_WIDTH:].astype(BF16),
        "norm_mix": norm_mix.reshape(1, D_MODEL),
        "norm_q": norm_q.reshape(1, Q_LORA),
        "norm_kv": norm_kv.reshape(1, KV_LORA),
        "norm_attn_out": norm_attn_out.reshape(1, MLA_WIDTH),
        "norm_attn_out_col": norm_attn_out.reshape(MLA_WIDTH, 1),
        "norm_ssm_out": norm_ssm_out.reshape(1, S5_WIDTH),
        "b_glu": b_glu.reshape(1, S5_WIDTH),
    }


def kernel(x_prompt, x_sample, c_prompt, c_sample, cache_ckv, cache_krope, state_s5_re, state_s5_im, page_table, w_ada, b_ada, norm_ffn1, ffn1_w1, ffn1_w3, ffn1_w2, norm_mix, w_in, norm_q, w_uq, norm_kv, w_uk, w_uv, s5_a_re, s5_a_im, s5_log_dt, s5_b_re, s5_b_im, s5_c_re, s5_c_im, s5_d, w_glu, b_glu, norm_attn_out, norm_ssm_out, w_out, norm_ffn2, ffn2_w1, ffn2_w3, ffn2_w2, norm_final):
    bp, seq, _ = x_prompt.shape
    bs = x_sample.shape[0]
    depth = w_ada.shape[0]
    assert depth == 1 and x_sample.shape[1] == 1
    n_pages = page_table.shape[1]
    past_len = n_pages * PAGE
    l = 0

    pad = (-(bs + bp)) % 8
    c_all = jnp.concatenate([c_sample, c_prompt, jnp.zeros((pad, D_MODEL), F32)], axis=0)
    mod = _ada(c_all, w_ada[l], b_ada[l])
    mod_s = mod[:bs].reshape(1, bs, ADA_CHUNKS * D_MODEL)
    mod_p = mod[bs:bs + bp].reshape(bp, 1, ADA_CHUNKS * D_MODEL)

    p = _layer_params(w_in[l], w_uq[l], w_uk[l], w_uv[l], w_glu[l], w_out[l], norm_mix[l], norm_q[l], norm_kv[l],
                      norm_attn_out[l], norm_ssm_out[l], b_glu[l])
    ops = _s5_operators(s5_a_re[l], s5_a_im[l], s5_log_dt[l], s5_b_re[l], s5_b_im[l], s5_c_re[l], s5_c_im[l],
                        s5_d[l])
    f1 = (ffn1_w1[l].astype(BF16), ffn1_w3[l].astype(BF16), ffn1_w2[l].astype(BF16))
    f2 = (ffn2_w1[l].astype(BF16), ffn2_w3[l].astype(BF16), ffn2_w2[l].astype(BF16))
    cos_p, sin_p = _rope_tables(jnp.arange(seq))
    cos_s, sin_s = _rope_tables(jnp.full((bs,), past_len))

    tm_p = 512
    xp = x_prompt.reshape(bp * seq, D_MODEL)
    xp = _ffn(xp, mod_p, 0, seq // tm_p, norm_ffn1[l], *f1, tm=tm_p)
    tm_mix = 512
    tq_p = 512
    q_lat_t, q_rope_t, ckv_p, kc_p, kct_p, krope_t_p, kr_p, u_p = _mixin(xp, mod_p, seq // tm_mix, seq, p, cos_p,
                                                                         sin_p, tm=tm_mix, tq=tq_p)
    attn_p = _prompt_attn(q_lat_t, q_rope_t, kc_p, kr_p, kct_p, p["w_uv_h"], p["norm_attn_out_col"], tq=tq_p)
    y_p, hre_p, him_p = _s5_prompt(u_p, ops)
    xp = _mixout(xp, attn_p.reshape(bp * seq, MLA_WIDTH), y_p.reshape(bp * seq, S5_WIDTH), mod_p, seq // tm_mix, p,
                 tm=tm_mix)
    y_prompt = _ffn(xp, mod_p, 6, seq // tm_p, norm_ffn2[l], *f2, norm_final, tm=tm_p).reshape(bp, seq, D_MODEL)

    xs = x_sample.reshape(bs, D_MODEL)
    xs = _ffn(xs, mod_s, 0, 1, norm_ffn1[l], *f1, tm=bs)
    q_lat_ts, q_rope_ts, ckv_s, _, _, krope_t_s, _, u_s = _mixin(xs, mod_s, 1, bs, p, cos_s, sin_s, tm=bs, tq=bs)
    krope_s = krope_t_s[0].T
    q_lat_s = q_lat_ts.reshape(KV_LORA, MLA_HEADS, bs).transpose(2, 1, 0)
    q_rope_s = q_rope_ts.reshape(QK_ROPE, MLA_HEADS, bs).transpose(2, 1, 0)
    o_lat = _sample_attn(page_table, q_lat_s, q_rope_s,
                         ckv_s.reshape(bs, 1, KV_LORA), krope_s.reshape(bs, 1, QK_ROPE), cache_ckv[l:l + 1],
                         cache_krope[l:l + 1].transpose(0, 1, 3, 2))
    attn_s = _sample_attn_out(o_lat.transpose(1, 0, 2), p["w_uv_h"], p["norm_attn_out"])
    y_s, hre_s, him_s = _s5_sample(u_s.reshape(bs, S5_WIDTH), state_s5_re[l], state_s5_im[l], ops)
    xs = _mixout(xs, attn_s, y_s, mod_s, 1, p, tm=bs)
    y_sample = _ffn(xs, mod_s, 6, 1, norm_ffn2[l], *f2, norm_final, tm=bs).reshape(bs, 1, D_MODEL)

    return (y_prompt, y_sample,
            ckv_p[None], krope_t_p.transpose(0, 2, 1)[None],
            ckv_s.reshape(1, bs, 1, KV_LORA), krope_s.reshape(1, bs, 1, QK_ROPE),
            hre_p[None], him_p[None], hre_s[None], him_s[None])
```

```python
import functools

import jax
import jax.numpy as jnp
from jax import lax
from jax.experimental import pallas as pl
from jax.experimental.pallas import tpu as pltpu

F32 = jnp.float32
BF16 = jnp.bfloat16

D_MODEL = 2048
D_FF = 5632
MLA_HEADS = 8
QK_NOPE = 128
QK_ROPE = 64
V_HEAD = 128
Q_LORA = 512
KV_LORA = 256
MLA_WIDTH = 1024
S5_WIDTH = 1024
S5_GROUP = 16
S5_GROUPS = 64
S5_STATE = 64
ADA_CHUNKS = 9
PAGE = 128
ROPE_THETA = 10000.0
SOFTMAX_SCALE = (QK_NOPE + QK_ROPE) ** -0.5
EPS = 1e-6

LANES = 128
S5_CHUNK = 16
GROUPS_PER_SLAB = LANES // S5_GROUP
N_SLABS = S5_WIDTH // LANES
SLAB_STATE = GROUPS_PER_SLAB * S5_STATE
CHUNK_COLS = S5_CHUNK * LANES
MXU_TILE = 2 * LANES
VMEM_LIMIT = 56 * 1024 * 1024


def _cparams(sem):
    return pltpu.CompilerParams(dimension_semantics=sem, vmem_limit_bytes=VMEM_LIMIT)


def _rms(x, g):
    return x * lax.rsqrt(jnp.mean(x * x, axis=-1, keepdims=True) + EPS) * g


def _rms_rows(x, g):
    return x * lax.rsqrt(jnp.mean(x * x, axis=0, keepdims=True) + EPS) * g


def _dot(a, b):
    return jnp.dot(a, b, preferred_element_type=F32)


def _dot_nt(a, b):
    return lax.dot_general(a, b, (((1,), (1,)), ((), ())), preferred_element_type=F32)


def _ada_kernel(c_ref, w_ref, b_ref, o_ref):
    c = c_ref[...]
    a = (c * jax.nn.sigmoid(c)).astype(BF16)
    o_ref[...] = _dot(a, w_ref[...].astype(BF16)) + b_ref[...]


def _ada(c, w_ada, b_ada):
    rows = c.shape[0]
    n = w_ada.shape[1]
    tn = 1024
    return pl.pallas_call(
        _ada_kernel,
        grid=(n // tn,),
        in_specs=[
            pl.BlockSpec((rows, D_MODEL), lambda j: (0, 0)),
            pl.BlockSpec((D_MODEL, tn), lambda j: (0, j)),
            pl.BlockSpec((1, tn), lambda j: (0, j)),
        ],
        out_specs=pl.BlockSpec((rows, tn), lambda j: (0, j)),
        out_shape=jax.ShapeDtypeStruct((rows, n), F32),
        compiler_params=_cparams(("arbitrary",)),
        name="ada_modulation",
    )(c, w_ada, b_ada.reshape(1, n))


def _ffn_kernel(x_ref, sh_ref, sc_ref, gt_ref, g_ref, w1_ref, w3_ref, w2_ref, *rest, final_norm):
    if final_norm:
        gf_ref, o_ref, h_ref, act_ref = rest
    else:
        o_ref, h_ref, act_ref = rest
    j = pl.program_id(1)
    last = pl.num_programs(1) - 1

    def gate():
        h = h_ref[...]
        a = _dot(h, w1_ref[...])
        b = _dot(h, w3_ref[...])
        act_ref[...] = (a * jax.nn.sigmoid(a) * b).astype(BF16)

    def down():
        return _dot(act_ref[...], w2_ref[...])

    @pl.when(j == 0)
    def _():
        h = _rms(x_ref[...], g_ref[...]) * (1.0 + sc_ref[0]) + sh_ref[0]
        h_ref[...] = h.astype(BF16)
        gate()

    @pl.when(j == 1)
    def _():
        o_ref[...] = down()
        gate()

    @pl.when((j > 1) & (j < last))
    def _():
        o_ref[...] += down()
        gate()

    @pl.when(j == last)
    def _():
        y = x_ref[...] + 0.5 * gt_ref[0] * (o_ref[...] + down())
        if final_norm:
            y = _rms(y, gf_ref[...])
        o_ref[...] = y


def _ffn(x, mod, chunk0, tiles_per_b, g, w1, w3, w2, gf=None, *, tm, tf=512):
    n = x.shape[0]
    r = mod.shape[1]
    n_ff = D_FF // tf
    assert n_ff >= 2

    def mod_spec(k):
        return pl.BlockSpec((1, r, D_MODEL), lambda i, j: (i // tiles_per_b, 0, k))

    in_specs = [
        pl.BlockSpec((tm, D_MODEL), lambda i, j: (i, 0)),
        mod_spec(chunk0), mod_spec(chunk0 + 1), mod_spec(chunk0 + 2),
        pl.BlockSpec((1, D_MODEL), lambda i, j: (0, 0)),
        pl.BlockSpec((D_MODEL, tf), lambda i, j: (0, jnp.minimum(j, n_ff - 1))),
        pl.BlockSpec((D_MODEL, tf), lambda i, j: (0, jnp.minimum(j, n_ff - 1))),
        pl.BlockSpec((tf, D_MODEL), lambda i, j: (jnp.maximum(j - 1, 0), 0)),
    ]
    args = [x, mod, mod, mod, g.reshape(1, D_MODEL), w1, w3, w2]
    if gf is not None:
        in_specs.append(pl.BlockSpec((1, D_MODEL), lambda i, j: (0, 0)))
        args.append(gf.reshape(1, D_MODEL))
    return pl.pallas_call(
        functools.partial(_ffn_kernel, final_norm=gf is not None),
        grid=(n // tm, n_ff + 1),
        in_specs=in_specs,
        out_specs=pl.BlockSpec((tm, D_MODEL), lambda i, j: (i, 0)),
        out_shape=jax.ShapeDtypeStruct((n, D_MODEL), F32),
        scratch_shapes=[pltpu.VMEM((tm, D_MODEL), BF16), pltpu.VMEM((tm, tf), BF16)],
        compiler_params=_cparams(("parallel", "arbitrary")),
        name="macaron_ffn",
    )(*args)


def _rope_rows(x, cos, sin):
    half = QK_ROPE // 2
    x1 = x[:half]
    x2 = x[half:]
    return x1 * cos - x2 * sin, x1 * sin + x2 * cos


def _mixin_kernel(x_ref, sh_ref, sc_ref, g_ref, wq_ref, wkv_ref, wkr_ref, wu_ref, gq_ref, wqn_ref, wqr_ref,
                  wuk_ref, gkv_ref, cos_ref, sin_ref,
                  qlat_ref, qrope_ref, ckv_ref, kc_ref, kct_ref, krope_ref, kr_ref, u_ref, *, tq):
    half = QK_ROPE // 2
    n_q = x_ref.shape[0] // tq
    h = (_rms(x_ref[...], g_ref[...]) * (1.0 + sc_ref[0]) + sh_ref[0]).astype(BF16)
    cos = cos_ref[...]
    sin = sin_ref[...]

    u_ref[0] = _dot(h, wu_ref[...])

    ckv = _rms(_dot(h, wkv_ref[...]), gkv_ref[...])
    ckv_ref[0] = ckv
    kc_ref[0] = ckv.astype(BF16)
    kct_ref[0] = ckv.T.astype(BF16)

    k1, k2 = _rope_rows(_dot_nt(wkr_ref[...], h), cos, sin)
    kr_t = jnp.concatenate([k1, k2], axis=0)
    krope_ref[0] = kr_t
    kr_ref[0] = kr_t.T.astype(BF16)

    qn = _rms(_dot(h, wq_ref[...]), gq_ref[...]).astype(BF16)
    q_nope = (_dot(qn, wqn_ref[...]) * SOFTMAX_SCALE).astype(BF16)
    q_rope_t = _dot_nt(wqr_ref[...], qn) * SOFTMAX_SCALE
    for hd in range(MLA_HEADS):
        ql_t = _dot_nt(wuk_ref[hd], q_nope[:, hd * QK_NOPE:(hd + 1) * QK_NOPE]).astype(BF16)
        r1, r2 = _rope_rows(q_rope_t[hd * QK_ROPE:(hd + 1) * QK_ROPE], cos, sin)
        r1 = r1.astype(BF16)
        r2 = r2.astype(BF16)
        for qq in range(n_q):
            src = slice(qq * tq, (qq + 1) * tq)
            dst = slice(hd * tq, (hd + 1) * tq)
            qlat_ref[0, qq, :, dst] = ql_t[:, src]
            qrope_ref[0, qq, :half, dst] = r1[:, src]
            qrope_ref[0, qq, half:, dst] = r2[:, src]


def _mixin(x, mod, tiles_per_b, t_len, p, cos_t, sin_t, *, tm, tq):
    n = x.shape[0]
    nb = n // t_len
    r = mod.shape[1]
    half = QK_ROPE // 2
    full = lambda shape: pl.BlockSpec(shape, lambda i: (0,) * len(shape))
    rows = lambda width: pl.BlockSpec((1, tm, width), lambda i: (i // tiles_per_b, i % tiles_per_b, 0))
    cols = lambda height: pl.BlockSpec((1, height, tm), lambda i: (i // tiles_per_b, 0, i % tiles_per_b))
    head_cols = lambda height: pl.BlockSpec((1, tm // tq, height, MLA_HEADS * tq),
                                            lambda i: (i // tiles_per_b, i % tiles_per_b, 0, 0))
    in_specs = [
        pl.BlockSpec((tm, D_MODEL), lambda i: (i, 0)),
        pl.BlockSpec((1, r, D_MODEL), lambda i: (i // tiles_per_b, 0, 3)),
        pl.BlockSpec((1, r, D_MODEL), lambda i: (i // tiles_per_b, 0, 4)),
        full((1, D_MODEL)),
        full((D_MODEL, Q_LORA)), full((D_MODEL, KV_LORA)), full((QK_ROPE, D_MODEL)), full((D_MODEL, S5_WIDTH)),
        full((1, Q_LORA)), full((Q_LORA, MLA_HEADS * QK_NOPE)), full((MLA_HEADS * QK_ROPE, Q_LORA)),
        full((MLA_HEADS, KV_LORA, QK_NOPE)), full((1, KV_LORA)),
        pl.BlockSpec((half, tm), lambda i: (0, i % tiles_per_b)),
        pl.BlockSpec((half, tm), lambda i: (0, i % tiles_per_b)),
    ]
    out_specs = [head_cols(KV_LORA), head_cols(QK_ROPE), rows(KV_LORA), rows(KV_LORA), cols(KV_LORA),
                 cols(QK_ROPE), rows(QK_ROPE), rows(S5_WIDTH)]
    out_shape = [
        jax.ShapeDtypeStruct((nb, t_len // tq, KV_LORA, MLA_HEADS * tq), BF16),
        jax.ShapeDtypeStruct((nb, t_len // tq, QK_ROPE, MLA_HEADS * tq), BF16),
        jax.ShapeDtypeStruct((nb, t_len, KV_LORA), F32),
        jax.ShapeDtypeStruct((nb, t_len, KV_LORA), BF16),
        jax.ShapeDtypeStruct((nb, KV_LORA, t_len), BF16),
        jax.ShapeDtypeStruct((nb, QK_ROPE, t_len), F32),
        jax.ShapeDtypeStruct((nb, t_len, QK_ROPE), BF16),
        jax.ShapeDtypeStruct((nb, t_len, S5_WIDTH), F32),
    ]
    return pl.pallas_call(
        functools.partial(_mixin_kernel, tq=tq),
        grid=(n // tm,),
        in_specs=in_specs,
        out_specs=out_specs,
        out_shape=out_shape,
        compiler_params=_cparams(("parallel",)),
        name="mixer_in",
    )(x, mod, mod, p["norm_mix"], p["w_q"], p["w_kv"], p["w_kr_t"], p["w_u"], p["norm_q"], p["w_uq_nope"],
      p["w_uq_rope_t"], p["w_uk_h"], p["norm_kv"], cos_t, sin_t)


def _prompt_attn_kernel(ql_ref, qr_ref, kc_ref, kr_ref, kct_ref, wuv_ref, g_ref, o_ref, *scratch, tq, tk):
    qi = pl.program_id(1)
    n_half = len(scratch) // 3
    m_ref, l_ref, acc_ref = scratch[:n_half], scratch[n_half:2 * n_half], scratch[2 * n_half:]
    width = m_ref[0].shape[1]
    heads_per_half = width // tq
    for hf in range(n_half):
        m_ref[hf][...] = jnp.full_like(m_ref[hf], -jnp.inf)
        l_ref[hf][...] = jnp.zeros_like(l_ref[hf])
        acc_ref[hf][...] = jnp.zeros_like(acc_ref[hf])

    def step(ki, diag):
        start = pl.multiple_of(ki * tk, tk)
        kc = kc_ref[0, pl.ds(start, tk), :]
        kr = kr_ref[0, pl.ds(start, tk), :]
        kct = kct_ref[0, :, pl.ds(start, tk)]
        masked = diag is not None
        if masked:
            k_local = lax.broadcasted_iota(jnp.int32, (tk, width), 0) + diag * tk
            t_local = lax.broadcasted_iota(jnp.int32, (tk, width), 1) & (tq - 1)
            keep = k_local <= t_local
        for hf in range(n_half):
            lanes = slice(hf * width, (hf + 1) * width)
            s = _dot(kc, ql_ref[0, 0, :, lanes]) + _dot(kr, qr_ref[0, 0, :, lanes])
            if masked:
                s = jnp.where(keep, s, -jnp.inf)
            m_old = m_ref[hf][...]
            m_new = jnp.maximum(m_old, jnp.max(s, axis=0, keepdims=True))
            alpha = jnp.exp(m_old - m_new)
            pexp = jnp.exp(s - m_new)
            l_ref[hf][...] = alpha * l_ref[hf][...] + jnp.sum(pexp, axis=0, keepdims=True)
            acc_ref[hf][...] = alpha * acc_ref[hf][...] + _dot(kct, pexp.astype(BF16))
            m_ref[hf][...] = m_new

    def body(ki, carry):
        step(ki, None)
        return carry

    blocks_per_tile = tq // tk
    lax.fori_loop(0, qi * blocks_per_tile, body, 0)
    for d in range(blocks_per_tile):
        step(qi * blocks_per_tile + d, d)
    parts = []
    for hd in range(MLA_HEADS):
        hf = hd // heads_per_half
        lanes = slice((hd % heads_per_half) * tq, (hd % heads_per_half + 1) * tq)
        o_t = acc_ref[hf][:, lanes] / l_ref[hf][:, lanes]
        parts.append(_dot(wuv_ref[hd], o_t.astype(BF16)))
    attn_t = _rms_rows(jnp.concatenate(parts, axis=0), g_ref[...])
    o_ref[0] = attn_t.T.astype(BF16)


def _prompt_attn(q_lat_t, q_rope_t, kc, kr, kc_t, w_uv_h, g_col, *, tq, tk=256, n_half=1):
    nb, n_q, _, _ = q_lat_t.shape
    t_len = n_q * tq
    width = MLA_HEADS * tq // n_half
    assert tq % tk == 0
    return pl.pallas_call(
        functools.partial(_prompt_attn_kernel, tq=tq, tk=tk),
        grid=(nb, n_q),
        in_specs=[
            pl.BlockSpec((1, 1, KV_LORA, MLA_HEADS * tq), lambda b, i: (b, i, 0, 0)),
            pl.BlockSpec((1, 1, QK_ROPE, MLA_HEADS * tq), lambda b, i: (b, i, 0, 0)),
            pl.BlockSpec((1, t_len, KV_LORA), lambda b, i: (b, 0, 0)),
            pl.BlockSpec((1, t_len, QK_ROPE), lambda b, i: (b, 0, 0)),
            pl.BlockSpec((1, KV_LORA, t_len), lambda b, i: (b, 0, 0)),
            pl.BlockSpec((MLA_HEADS, V_HEAD, KV_LORA), lambda b, i: (0, 0, 0)),
            pl.BlockSpec((MLA_WIDTH, 1), lambda b, i: (0, 0)),
        ],
        out_specs=pl.BlockSpec((1, tq, MLA_WIDTH), lambda b, i: (b, i, 0)),
        out_shape=jax.ShapeDtypeStruct((nb, t_len, MLA_WIDTH), BF16),
        scratch_shapes=([pltpu.VMEM((1, width), F32)] * (2 * n_half) + [pltpu.VMEM((KV_LORA, width), F32)] * n_half),
        compiler_params=_cparams(("parallel", "parallel")),
        name="prompt_attention",
    )(q_lat_t, q_rope_t, kc, kr, kc_t, w_uv_h, g_col)


def _sample_attn_kernel(pt_ref, ql_ref, qr_ref, kcn_ref, krn_ref, ckv_hbm, kr_hbm, o_ref, kbuf, rbuf, sem_k, sem_r,
                        *, pages, streams):
    nb, n_pages = pt_ref.shape
    n_chunks = n_pages // pages
    n_groups = nb // streams
    total = n_groups * n_chunks
    n_slots = kbuf.shape[0]
    ahead = n_slots - 1

    def chunk_copies(g, slot):
        grp = g // n_chunks
        c = g % n_chunks
        copies = []
        for st in range(streams):
            b = grp * streams + st
            for i in range(pages):
                page = pt_ref[b, c * pages + i]
                copies.append(pltpu.make_async_copy(
                    ckv_hbm.at[0, page], kbuf.at[slot, st, pl.ds(i * PAGE, PAGE), :], sem_k.at[slot, st]))
                copies.append(pltpu.make_async_copy(
                    kr_hbm.at[0, page], rbuf.at[slot, st, i], sem_r.at[slot, st]))
        return copies

    grp = pl.program_id(0)

    @pl.when(grp == 0)
    def _():
        for g0 in range(ahead):
            for cp in chunk_copies(g0, g0):
                cp.start()

    qs = [(ql_ref[st], qr_ref[st]) for st in range(streams)]

    def chunk_body(c, carry):
        g = grp * n_chunks + c
        slot = g % n_slots

        @pl.when(g + ahead < total)
        def _():
            for cp in chunk_copies(g + ahead, (g + ahead) % n_slots):
                cp.start()

        for cp in chunk_copies(g, slot):
            cp.wait()
        out = []
        for st in range(streams):
            m_old, l_old, acc_old = carry[st]
            ql, qr = qs[st]
            kc = kbuf[slot, st].astype(BF16)
            kr_t = jnp.concatenate([rbuf[slot, st, i].astype(BF16) for i in range(pages)], axis=1)
            s = _dot_nt(ql, kc) + _dot(qr, kr_t)
            m_new = jnp.maximum(m_old, jnp.max(s, axis=1, keepdims=True))
            alpha = jnp.exp(m_old - m_new)
            pexp = jnp.exp(s - m_new)
            l_new = alpha * l_old + jnp.sum(pexp, axis=1, keepdims=True)
            acc_new = alpha * acc_old + _dot(pexp.astype(BF16), kc)
            out.append((m_new, l_new, acc_new))
        return tuple(out)

    init = tuple((jnp.full((MLA_HEADS, 1), -jnp.inf, F32), jnp.zeros((MLA_HEADS, 1), F32),
                  jnp.zeros((MLA_HEADS, KV_LORA), F32)) for _ in range(streams))
    final = lax.fori_loop(0, n_chunks, chunk_body, init)
    for st in range(streams):
        m_old, l_old, acc_old = final[st]
        ql, qr = qs[st]
        kcn = kcn_ref[st].astype(BF16).astype(F32)
        krn = krn_ref[st].astype(BF16).astype(F32)
        s_n = (jnp.sum(ql.astype(F32) * kcn, axis=1, keepdims=True)
               + jnp.sum(qr.astype(F32) * krn, axis=1, keepdims=True))
        m_f = jnp.maximum(m_old, s_n)
        a_f = jnp.exp(m_old - m_f)
        p_n = jnp.exp(s_n - m_f)
        l_f = a_f * l_old + p_n
        acc_f = a_f * acc_old + p_n.astype(BF16).astype(F32) * kcn
        o_ref[st] = acc_f / l_f


def _sample_attn(page_table, q_lat, q_rope, ckv_new, krope_new, cache_ckv, cache_krope_t, *, pages=16, streams=2,
                 slots=3):
    nb, n_pages = page_table.shape
    assert n_pages % pages == 0 and nb % streams == 0 and (nb // streams) * (n_pages // pages) >= slots
    per_group = lambda rows, width: pl.BlockSpec((streams, rows, width), lambda g, pt: (g, 0, 0))
    grid_spec = pltpu.PrefetchScalarGridSpec(
        num_scalar_prefetch=1,
        grid=(nb // streams,),
        in_specs=[
            per_group(MLA_HEADS, KV_LORA), per_group(MLA_HEADS, QK_ROPE),
            per_group(1, KV_LORA), per_group(1, QK_ROPE),
            pl.BlockSpec(memory_space=pl.ANY), pl.BlockSpec(memory_space=pl.ANY),
        ],
        out_specs=per_group(MLA_HEADS, KV_LORA),
        scratch_shapes=[
            pltpu.VMEM((slots, streams, pages * PAGE, KV_LORA), F32),
            pltpu.VMEM((slots, streams, pages, QK_ROPE, PAGE), F32),
            pltpu.SemaphoreType.DMA((slots, streams)),
            pltpu.SemaphoreType.DMA((slots, streams)),
        ],
    )
    return pl.pallas_call(
        functools.partial(_sample_attn_kernel, pages=pages, streams=streams),
        grid_spec=grid_spec,
        out_shape=jax.ShapeDtypeStruct((nb, MLA_HEADS, KV_LORA), F32),
        compiler_params=_cparams(("arbitrary",)),
        name="sample_paged_attention",
    )(page_table, q_lat, q_rope, ckv_new, krope_new, cache_ckv, cache_krope_t)


def _sample_attn_out_kernel(o_ref, wuv_ref, g_ref, out_ref):
    parts = [_dot_nt(o_ref[hd].astype(BF16), wuv_ref[hd]) for hd in range(MLA_HEADS)]
    out_ref[...] = _rms(jnp.concatenate(parts, axis=1), g_ref[...]).astype(BF16)


def _sample_attn_out(o_lat_h, w_uv_h, g):
    nb = o_lat_h.shape[1]
    return pl.pallas_call(
        _sample_attn_out_kernel,
        out_shape=jax.ShapeDtypeStruct((nb, MLA_WIDTH), BF16),
        name="sample_attention_out",
    )(o_lat_h, w_uv_h, g)


def _same_group(shape, row_shift, col_shift):
    rows = lax.broadcasted_iota(jnp.int32, shape, 0)
    cols = lax.broadcasted_iota(jnp.int32, shape, 1)
    mask = GROUPS_PER_SLAB - 1
    return ((rows >> row_shift) & mask) == ((cols >> col_shift) & mask)


def _s5_prompt_kernel(u_ref, wr_ref, we_ref, wf_ref, tr_ref, te_ref, tf_ref, are_ref, aim_ref, d_ref,
                      y_ref, hre_ref, him_ref, r_ref, e_ref, f_ref, uc_ref, s_ref, hp_ref, yc_ref):
    n_chunks = uc_ref.shape[0]
    n_blk = CHUNK_COLS // MXU_TILE
    ch_bits = S5_GROUP.bit_length() - 1
    st_bits = S5_STATE.bit_length() - 1

    @pl.when(pl.program_id(1) == 0)
    def _():
        zero = jnp.zeros((), F32)
        r_ref[...] = jnp.where(_same_group(r_ref.shape, ch_bits, ch_bits),
                               _dot(wr_ref[0], tr_ref[...]), zero).astype(BF16)
        e_ref[...] = jnp.where(_same_group(e_ref.shape, ch_bits, st_bits),
                               _dot(we_ref[0], te_ref[...]), zero).astype(BF16)
        f_ref[...] = jnp.where(_same_group(f_ref.shape, st_bits, ch_bits),
                               _dot(tf_ref[...], wf_ref[0]), zero).astype(BF16)

    for s in range(S5_CHUNK):
        uc_ref[:, s * LANES:(s + 1) * LANES] = u_ref[0, pl.ds(s, n_chunks, stride=S5_CHUNK), :]
    uc = uc_ref[...]
    ub = uc.astype(BF16)

    s_ref[...] = _dot(ub, e_ref[...])

    a_re = are_ref[0]
    a_im = aim_ref[0]

    def scan(k, carry):
        h_re, h_im = carry
        hp_ref[pl.ds(k, 1), :] = jnp.concatenate([h_re, h_im], axis=1)
        row = s_ref[pl.ds(k, 1), :]
        n_re = a_re * h_re - a_im * h_im + row[:, :SLAB_STATE]
        n_im = a_re * h_im + a_im * h_re + row[:, SLAB_STATE:]
        return n_re, n_im

    zero_state = jnp.zeros((1, SLAB_STATE), F32)
    h_re, h_im = lax.fori_loop(0, n_chunks, scan, (zero_state, zero_state), unroll=4)
    hre_ref[0, 0] = h_re
    him_ref[0, 0] = h_im

    yc_ref[...] = _dot(hp_ref[...].astype(BF16), f_ref[...]) + uc * d_ref[0]
    for tb in range(n_blk):
        yc_ref[:, tb * MXU_TILE:(tb + 1) * MXU_TILE] += _dot(ub[:, :(tb + 1) * MXU_TILE],
                                                             r_ref[(n_blk - 1 - tb) * MXU_TILE:, :])
    y = jax.nn.gelu(yc_ref[...])
    for t in range(S5_CHUNK):
        y_ref[0, pl.ds(t, n_chunks, stride=S5_CHUNK), :] = y[:, t * LANES:(t + 1) * LANES]


def _s5_prompt(u, ops):
    nb, t_len, _ = u.shape
    n_chunks = t_len // S5_CHUNK
    slab = lambda shape: pl.BlockSpec((1,) + shape, lambda j, b: (j,) + (0,) * len(shape))
    full = lambda shape: pl.BlockSpec(shape, lambda j, b: (0,) * len(shape))
    y, hre, him = pl.pallas_call(
        _s5_prompt_kernel,
        grid=(N_SLABS, nb),
        in_specs=[
            pl.BlockSpec((1, t_len, LANES), lambda j, b: (b, 0, j)),
            slab((CHUNK_COLS, 2 * S5_GROUP)), slab((CHUNK_COLS, 2 * S5_STATE)), slab((2 * S5_STATE, CHUNK_COLS)),
            full((2 * S5_GROUP, MXU_TILE)), full((2 * S5_STATE, 2 * SLAB_STATE)), full((2 * SLAB_STATE, 2 * S5_STATE)),
            slab((1, SLAB_STATE)), slab((1, SLAB_STATE)), slab((1, CHUNK_COLS)),
        ],
        out_specs=[
            pl.BlockSpec((1, t_len, LANES), lambda j, b: (b, 0, j)),
            pl.BlockSpec((1, 1, 1, SLAB_STATE), lambda j, b: (j, b, 0, 0)),
            pl.BlockSpec((1, 1, 1, SLAB_STATE), lambda j, b: (j, b, 0, 0)),
        ],
        out_shape=[
            jax.ShapeDtypeStruct((nb, t_len, S5_WIDTH), F32),
            jax.ShapeDtypeStruct((N_SLABS, nb, 1, SLAB_STATE), F32),
            jax.ShapeDtypeStruct((N_SLABS, nb, 1, SLAB_STATE), F32),
        ],
        scratch_shapes=[
            pltpu.VMEM((CHUNK_COLS, MXU_TILE), BF16),
            pltpu.VMEM((CHUNK_COLS, 2 * SLAB_STATE), BF16),
            pltpu.VMEM((2 * SLAB_STATE, CHUNK_COLS), BF16),
            pltpu.VMEM((n_chunks, CHUNK_COLS), F32),
            pltpu.VMEM((n_chunks, 2 * SLAB_STATE), F32),
            pltpu.VMEM((n_chunks, 2 * SLAB_STATE), F32),
            pltpu.VMEM((n_chunks, CHUNK_COLS), F32),
        ],
        compiler_params=_cparams(("arbitrary", "arbitrary")),
        name="s5_prompt",
    )(u, ops["wr"], ops["we"], ops["wf"], ops["tile_r"], ops["tile_e"], ops["tile_f"], ops["a_re"], ops["a_im"],
      ops["d_chunk"])

    def to_state(h):
        return h.reshape(N_SLABS, nb, GROUPS_PER_SLAB, S5_STATE).transpose(1, 0, 2, 3).reshape(nb, S5_GROUPS, S5_STATE)

    return y, to_state(hre), to_state(him)


def _split_bf16(x):
    hi = x.astype(BF16)
    return hi, (x - hi.astype(F32)).astype(BF16)


def _s5_sample_kernel(u_ref, h0re_ref, h0im_ref, bd_hi_ref, bd_lo_ref, cd_ref, lre_ref, lim_ref, d_ref,
                      y_ref, hre_ref, him_ref):
    u = u_ref[...]
    u_hi, u_lo = _split_bf16(u)
    bu = _dot(u_hi, bd_hi_ref[0]) + (_dot(u_hi, bd_lo_ref[0]) + _dot(u_lo, bd_hi_ref[0]))
    l_re = lre_ref[0]
    l_im = lim_ref[0]
    h0_re = h0re_ref[...]
    h0_im = h0im_ref[...]
    h_re = l_re * h0_re - l_im * h0_im + bu[:, :SLAB_STATE]
    h_im = l_re * h0_im + l_im * h0_re + bu[:, SLAB_STATE:]
    hre_ref[...] = h_re
    him_ref[...] = h_im
    h = jnp.concatenate([h_re, h_im], axis=1).astype(BF16)
    y_ref[...] = jax.nn.gelu(_dot(h, cd_ref[0]) + u * d_ref[0])


def _s5_sample(u, h0_re, h0_im, ops):
    nb = u.shape[0]
    n_state = S5_GROUPS * S5_STATE
    slab = lambda shape: pl.BlockSpec((1,) + shape, lambda j: (j,) + (0,) * len(shape))
    col = lambda width: pl.BlockSpec((nb, width), lambda j: (0, j))
    y, hre, him = pl.pallas_call(
        _s5_sample_kernel,
        grid=(N_SLABS,),
        in_specs=[
            col(LANES), col(SLAB_STATE), col(SLAB_STATE),
            slab((LANES, 2 * SLAB_STATE)), slab((LANES, 2 * SLAB_STATE)), slab((2 * SLAB_STATE, LANES)),
            slab((1, SLAB_STATE)), slab((1, SLAB_STATE)), slab((1, LANES)),
        ],
        out_specs=[col(LANES), col(SLAB_STATE), col(SLAB_STATE)],
        out_shape=[
            jax.ShapeDtypeStruct((nb, S5_WIDTH), F32),
            jax.ShapeDtypeStruct((nb, n_state), F32),
            jax.ShapeDtypeStruct((nb, n_state), F32),
        ],
        compiler_params=_cparams(("parallel",)),
        name="s5_sample",
    )(u, h0_re.reshape(nb, n_state), h0_im.reshape(nb, n_state), ops["bd_hi"], ops["bd_lo"], ops["cd"],
      ops["l_re"], ops["l_im"], ops["d_slab"])
    return y, hre.reshape(nb, S5_GROUPS, S5_STATE), him.reshape(nb, S5_GROUPS, S5_STATE)


def _mixout_kernel(x_ref, attn_ref, y_ref, gt_ref, wglu_ref, bglu_ref, gs_ref, woa_ref, woy_ref, o_ref):
    y = y_ref[...]
    z = _dot(y.astype(BF16), wglu_ref[...]) + bglu_ref[...]
    yn = _rms(y * jax.nn.sigmoid(z), gs_ref[...]).astype(BF16)
    mix = _dot(attn_ref[...], woa_ref[...]) + _dot(yn, woy_ref[...])
    o_ref[...] = x_ref[...] + gt_ref[0] * mix


def _mixout(x, attn, y, mod, tiles_per_b, p, *, tm):
    n = x.shape[0]
    r = mod.shape[1]
    full = lambda shape: pl.BlockSpec(shape, lambda i: (0,) * len(shape))
    return pl.pallas_call(
        _mixout_kernel,
        grid=(n // tm,),
        in_specs=[
            pl.BlockSpec((tm, D_MODEL), lambda i: (i, 0)),
            pl.BlockSpec((tm, MLA_WIDTH), lambda i: (i, 0)),
            pl.BlockSpec((tm, S5_WIDTH), lambda i: (i, 0)),
            pl.BlockSpec((1, r, D_MODEL), lambda i: (i // tiles_per_b, 0, 5)),
            full((S5_WIDTH, S5_WIDTH)), full((1, S5_WIDTH)), full((1, S5_WIDTH)),
            full((MLA_WIDTH, D_MODEL)), full((S5_WIDTH, D_MODEL)),
        ],
        out_specs=pl.BlockSpec((tm, D_MODEL), lambda i: (i, 0)),
        out_shape=jax.ShapeDtypeStruct((n, D_MODEL), F32),
        compiler_params=_cparams(("parallel",)),
        name="mixer_out",
    )(x, attn, y, mod, p["w_glu"], p["b_glu"], p["norm_ssm_out"], p["w_out_attn"], p["w_out_ssm"])


def _rope_tables(pos):
    half = QK_ROPE // 2
    inv_freq = ROPE_THETA ** (-jnp.arange(half, dtype=F32) / half)
    ang = pos.astype(F32)[:, None] * inv_freq[None, :]
    return jnp.cos(ang).T, jnp.sin(ang).T


def _group_diag(x, g_axis, new_axis):
    x = jnp.expand_dims(x, new_axis)
    shape = [1] * x.ndim
    shape[g_axis if g_axis < new_axis else g_axis + 1] = GROUPS_PER_SLAB
    shape[new_axis] = GROUPS_PER_SLAB
    return x * jnp.eye(GROUPS_PER_SLAB, dtype=x.dtype).reshape(shape)


def _by_slab(x, g_axis):
    return x.reshape(x.shape[:g_axis] + (N_SLABS, GROUPS_PER_SLAB) + x.shape[g_axis + 1:])


def _replicate_over_groups(n_outer, n_inner):
    eo = jnp.eye(n_outer, dtype=F32)[:, None, :, None, None]
    ei = jnp.eye(n_inner, dtype=F32)[None, :, None, None, :]
    ones = jnp.ones((1, 1, 1, GROUPS_PER_SLAB, 1), F32)
    return (eo * ei * ones).reshape(n_outer * n_inner, n_outer * GROUPS_PER_SLAB * n_inner)


def _s5_operators(a_re, a_im, log_dt, b_re, b_im, c_re, c_im, d_skip):
    hp = lax.Precision.HIGHEST
    a_re, a_im, b_re, b_im, c_re, c_im = (v.astype(F32) for v in (a_re, a_im, b_re, b_im, c_re, c_im))
    dt = jnp.exp(log_dt.astype(F32))[:, None]
    z_re = a_re * dt
    z_im = a_im * dt

    def lam_pow(n):
        mag = jnp.exp(z_re[None] * n[:, None, None])
        ang = z_im[None] * n[:, None, None]
        return mag * jnp.cos(ang), mag * jnp.sin(ang)

    steps = jnp.arange(S5_CHUNK + 1, dtype=F32)
    pw_re, pw_im = lam_pow(steps)
    lb_re, lb_im = pw_re[1], pw_im[1]
    den = a_re * a_re + a_im * a_im
    q_re = ((lb_re - 1.0) * a_re + lb_im * a_im) / den
    q_im = (lb_im * a_re - (lb_re - 1.0) * a_im) / den
    bb_re = q_re[:, :, None] * b_re - q_im[:, :, None] * b_im
    bb_im = q_re[:, :, None] * b_im + q_im[:, :, None] * b_re

    cp_re = c_re[None] * pw_re[:S5_CHUNK, :, None, :] - c_im[None] * pw_im[:S5_CHUNK, :, None, :]
    cp_im = c_re[None] * pw_im[:S5_CHUNK, :, None, :] + c_im[None] * pw_re[:S5_CHUNK, :, None, :]
    m = jnp.einsum("ngcq,gqd->ngcd", jnp.concatenate([cp_re, -cp_im], axis=3),
                   jnp.concatenate([bb_re, bb_im], axis=1), precision=hp)
    m_ext = jnp.concatenate([m, jnp.zeros_like(m[:1])], axis=0)
    n_blk = S5_CHUNK // 2
    d_i = (n_blk - 1 - jnp.arange(n_blk))[:, None, None]
    s_i = jnp.arange(2)[None, :, None]
    t_i = jnp.arange(2)[None, None, :]
    lag = 2 * d_i + t_i - s_i
    mg = _by_slab(m_ext[lag], 3)
    wr = mg.transpose(3, 0, 1, 4, 6, 2, 5).reshape(N_SLABS, CHUNK_COLS, 2 * S5_GROUP)

    pr_re, pr_im = lam_pow(S5_CHUNK - 1 - steps[:S5_CHUNK])
    w_re = pr_re[:, :, :, None] * bb_re[None] - pr_im[:, :, :, None] * bb_im[None]
    w_im = pr_re[:, :, :, None] * bb_im[None] + pr_im[:, :, :, None] * bb_re[None]

    def e_half(v):
        return _by_slab(v, 1).transpose(1, 0, 2, 4, 3).reshape(N_SLABS, CHUNK_COLS, S5_STATE)

    we = jnp.concatenate([e_half(w_re), e_half(w_im)], axis=2)

    g_re = c_re[None] * pw_re[1:, :, None, :] - c_im[None] * pw_im[1:, :, None, :]
    g_im = c_re[None] * pw_im[1:, :, None, :] + c_im[None] * pw_re[1:, :, None, :]

    def f_half(v):
        return _by_slab(v, 1).transpose(1, 4, 0, 2, 3).reshape(N_SLABS, S5_STATE, CHUNK_COLS)

    wf = jnp.concatenate([f_half(g_re), f_half(-g_im)], axis=1)

    def slab_vec(v):
        return v.reshape(N_SLABS, 1, SLAB_STATE)

    d_slab = d_skip.astype(F32).reshape(N_SLABS, 1, LANES)
    tile_e = _replicate_over_groups(2, S5_STATE)

    def bd_half(v):
        return _group_diag(_by_slab(v, 0).transpose(0, 1, 3, 2), 1, 3).reshape(N_SLABS, LANES, SLAB_STATE)

    def cd_half(v):
        return _group_diag(_by_slab(v, 0).transpose(0, 1, 3, 2), 1, 3).reshape(N_SLABS, SLAB_STATE, LANES)

    bd = jnp.concatenate([bd_half(bb_re), bd_half(bb_im)], axis=2)
    cd = jnp.concatenate([cd_half(c_re), cd_half(-c_im)], axis=1)
    bd_hi = bd.astype(BF16)
    bd_lo = (bd - bd_hi.astype(F32)).astype(BF16)
    return {
        "wr": wr.astype(BF16), "we": we.astype(BF16), "wf": wf.astype(BF16),
        "tile_r": _replicate_over_groups(2, S5_GROUP).astype(BF16),
        "tile_e": tile_e.astype(BF16), "tile_f": tile_e.T.astype(BF16),
        "a_re": slab_vec(pw_re[S5_CHUNK]), "a_im": slab_vec(pw_im[S5_CHUNK]),
        "d_chunk": jnp.tile(d_slab, (1, 1, S5_CHUNK)), "d_slab": d_slab,
        "bd_hi": bd_hi, "bd_lo": bd_lo, "cd": cd.astype(BF16),
        "l_re": slab_vec(lb_re), "l_im": slab_vec(lb_im),
    }


def _layer_params(w_in, w_uq, w_uk, w_uv, w_glu, w_out, norm_mix, norm_q, norm_kv, norm_attn_out, norm_ssm_out,
                  b_glu):
    c0, c1, c2 = Q_LORA, Q_LORA + KV_LORA, Q_LORA + KV_LORA + QK_ROPE
    w_uq_h = w_uq.reshape(Q_LORA, MLA_HEADS, QK_NOPE + QK_ROPE)
    return {
        "w_q": w_in[:, :c0].astype(BF16),
        "w_kv": w_in[:, c0:c1].astype(BF16),
        "w_kr_t": w_in[:, c1:c2].T.astype(BF16),
        "w_u": w_in[:, c2:].astype(BF16),
        "w_uq_nope": w_uq_h[:, :, :QK_NOPE].reshape(Q_LORA, MLA_HEADS * QK_NOPE).astype(BF16),
        "w_uq_rope_t": w_uq_h[:, :, QK_NOPE:].reshape(Q_LORA, MLA_HEADS * QK_ROPE).T.astype(BF16),
        "w_uk_h": w_uk.transpose(1, 0, 2).astype(BF16),
        "w_uv_h": w_uv.transpose(1, 2, 0).astype(BF16),
        "w_glu": w_glu.astype(BF16),
        "w_out_attn": w_out[:MLA_WIDTH].astype(BF16),
        "w_out_ssm": w_out[MLA_WIDTH:].astype(BF16),
        "norm_mix": norm_mix.reshape(1, D_MODEL),
        "norm_q": norm_q.reshape(1, Q_LORA),
        "norm_kv": norm_kv.reshape(1, KV_LORA),
        "norm_attn_out": norm_attn_out.reshape(1, MLA_WIDTH),
        "norm_attn_out_col": norm_attn_out.reshape(MLA_WIDTH, 1),
        "norm_ssm_out": norm_ssm_out.reshape(1, S5_WIDTH),
        "b_glu": b_glu.reshape(1, S5_WIDTH),
    }


def kernel(x_prompt, x_sample, c_prompt, c_sample, cache_ckv, cache_krope, state_s5_re, state_s5_im, page_table, w_ada, b_ada, norm_ffn1, ffn1_w1, ffn1_w3, ffn1_w2, norm_mix, w_in, norm_q, w_uq, norm_kv, w_uk, w_uv, s5_a_re, s5_a_im, s5_log_dt, s5_b_re, s5_b_im, s5_c_re, s5_c_im, s5_d, w_glu, b_glu, norm_attn_out, norm_ssm_out, w_out, norm_ffn2, ffn2_w1, ffn2_w3, ffn2_w2, norm_final):
    bp, seq, _ = x_prompt.shape
    bs = x_sample.shape[0]
    depth = w_ada.shape[0]
    assert depth == 1 and x_sample.shape[1] == 1
    n_pages = page_table.shape[1]
    past_len = n_pages * PAGE
    l = 0

    pad = (-(bs + bp)) % 8
    c_all = jnp.concatenate([c_sample, c_prompt, jnp.zeros((pad, D_MODEL), F32)], axis=0)
    mod = _ada(c_all, w_ada[l], b_ada[l])
    mod_s = mod[:bs].reshape(1, bs, ADA_CHUNKS * D_MODEL)
    mod_p = mod[bs:bs + bp].reshape(bp, 1, ADA_CHUNKS * D_MODEL)

    p = _layer_params(w_in[l], w_uq[l], w_uk[l], w_uv[l], w_glu[l], w_out[l], norm_mix[l], norm_q[l], norm_kv[l],
                      norm_attn_out[l], norm_ssm_out[l], b_glu[l])
    ops = _s5_operators(s5_a_re[l], s5_a_im[l], s5_log_dt[l], s5_b_re[l], s5_b_im[l], s5_c_re[l], s5_c_im[l],
                        s5_d[l])
    f1 = (ffn1_w1[l].astype(BF16), ffn1_w3[l].astype(BF16), ffn1_w2[l].astype(BF16))
    f2 = (ffn2_w1[l].astype(BF16), ffn2_w3[l].astype(BF16), ffn2_w2[l].astype(BF16))
    cos_p, sin_p = _rope_tables(jnp.arange(seq))
    cos_s, sin_s = _rope_tables(jnp.full((bs,), past_len))

    tm_p = 512
    xp = x_prompt.reshape(bp * seq, D_MODEL)
    xp = _ffn(xp, mod_p, 0, seq // tm_p, norm_ffn1[l], *f1, tm=tm_p)
    tm_mix = 512
    tq_p = 512
    q_lat_t, q_rope_t, ckv_p, kc_p, kct_p, krope_t_p, kr_p, u_p = _mixin(xp, mod_p, seq // tm_mix, seq, p, cos_p,
                                                                         sin_p, tm=tm_mix, tq=tq_p)
    attn_p = _prompt_attn(q_lat_t, q_rope_t, kc_p, kr_p, kct_p, p["w_uv_h"], p["norm_attn_out_col"], tq=tq_p)
    y_p, hre_p, him_p = _s5_prompt(u_p, ops)
    xp = _mixout(xp, attn_p.reshape(bp * seq, MLA_WIDTH), y_p.reshape(bp * seq, S5_WIDTH), mod_p, seq // tm_mix, p,
                 tm=tm_mix)
    y_prompt = _ffn(xp, mod_p, 6, seq // tm_p, norm_ffn2[l], *f2, norm_final, tm=tm_p).reshape(bp, seq, D_MODEL)

    xs = x_sample.reshape(bs, D_MODEL)
    xs = _ffn(xs, mod_s, 0, 1, norm_ffn1[l], *f1, tm=bs)
    q_lat_ts, q_rope_ts, ckv_s, _, _, krope_t_s, _, u_s = _mixin(xs, mod_s, 1, bs, p, cos_s, sin_s, tm=bs, tq=bs)
    krope_s = krope_t_s[0].T
    q_lat_s = q_lat_ts.reshape(KV_LORA, MLA_HEADS, bs).transpose(2, 1, 0)
    q_rope_s = q_rope_ts.reshape(QK_ROPE, MLA_HEADS, bs).transpose(2, 1, 0)
    o_lat = _sample_attn(page_table, q_lat_s, q_rope_s,
                         ckv_s.reshape(bs, 1, KV_LORA), krope_s.reshape(bs, 1, QK_ROPE), cache_ckv[l:l + 1],
                         cache_krope[l:l + 1].transpose(0, 1, 3, 2))
    attn_s = _sample_attn_out(o_lat.transpose(1, 0, 2), p["w_uv_h"], p["norm_attn_out"])
    y_s, hre_s, him_s = _s5_sample(u_s.reshape(bs, S5_WIDTH), state_s5_re[l], state_s5_im[l], ops)
    xs = _mixout(xs, attn_s, y_s, mod_s, 1, p, tm=bs)
    y_sample = _ffn(xs, mod_s, 6, 1, norm_ffn2[l], *f2, norm_final, tm=bs).reshape(bs, 1, D_MODEL)

    return (y_prompt, y_sample,
            ckv_p[None], krope_t_p.transpose(0, 2, 1)[None],
            ckv_s.reshape(1, bs, 1, KV_LORA), krope_s.reshape(1, bs, 1, QK_ROPE),
            hre_p[None], him_p[None], hre_s[None], him_s[None])
```

```python
import functools

import jax
import jax.numpy as jnp
from jax import lax
from jax.experimental import pallas as pl
from jax.experimental.pallas import tpu as pltpu

F32 = jnp.float32
BF16 = jnp.bfloat16

D_MODEL = 2048
D_FF = 5632
MLA_HEADS = 8
QK_NOPE = 128
QK_ROPE = 64
V_HEAD = 128
Q_LORA = 512
KV_LORA = 256
MLA_WIDTH = 1024
S5_WIDTH = 1024
S5_GROUP = 16
S5_GROUPS = 64
S5_STATE = 64
ADA_CHUNKS = 9
PAGE = 128
ROPE_THETA = 10000.0
SOFTMAX_SCALE = (QK_NOPE + QK_ROPE) ** -0.5
EPS = 1e-6

LANES = 128
S5_CHUNK = 16
GROUPS_PER_SLAB = LANES // S5_GROUP
N_SLABS = S5_WIDTH // LANES
SLAB_STATE = GROUPS_PER_SLAB * S5_STATE
CHUNK_COLS = S5_CHUNK * LANES
MXU_TILE = 2 * LANES
VMEM_LIMIT = 56 * 1024 * 1024


def _cparams(sem):
    return pltpu.CompilerParams(dimension_semantics=sem, vmem_limit_bytes=VMEM_LIMIT)


def _rms(x, g):
    return x * lax.rsqrt(jnp.mean(x * x, axis=-1, keepdims=True) + EPS) * g


def _rms_rows(x, g):
    return x * lax.rsqrt(jnp.mean(x * x, axis=0, keepdims=True) + EPS) * g


def _dot(a, b):
    return jnp.dot(a, b, preferred_element_type=F32)


def _dot_nt(a, b):
    return lax.dot_general(a, b, (((1,), (1,)), ((), ())), preferred_element_type=F32)


def _ada_kernel(c_ref, w_ref, b_ref, o_ref):
    c = c_ref[...]
    a = (c * jax.nn.sigmoid(c)).astype(BF16)
    o_ref[...] = _dot(a, w_ref[...].astype(BF16)) + b_ref[...]


def _ada(c, w_ada, b_ada):
    rows = c.shape[0]
    n = w_ada.shape[1]
    tn = 1024
    return pl.pallas_call(
        _ada_kernel,
        grid=(n // tn,),
        in_specs=[
            pl.BlockSpec((rows, D_MODEL), lambda j: (0, 0)),
            pl.BlockSpec((D_MODEL, tn), lambda j: (0, j)),
            pl.BlockSpec((1, tn), lambda j: (0, j)),
        ],
        out_specs=pl.BlockSpec((rows, tn), lambda j: (0, j)),
        out_shape=jax.ShapeDtypeStruct((rows, n), F32),
        compiler_params=_cparams(("arbitrary",)),
        name="ada_modulation",
    )(c, w_ada, b_ada.reshape(1, n))


def _ffn_kernel(x_ref, sh_ref, sc_ref, gt_ref, g_ref, w1_ref, w3_ref, w2_ref, w2_last_ref, *rest, final_norm):
    if final_norm:
        gf_ref, o_ref, h_ref, act_ref = rest
    else:
        o_ref, h_ref, act_ref = rest
    j = pl.program_id(1)
    last = pl.num_programs(1) - 1

    def gate():
        h = h_ref[...]
        a = _dot(h, w1_ref[...])
        b = _dot(h, w3_ref[...])
        act_ref[...] = (a * jax.nn.sigmoid(a) * b).astype(BF16)

    def down():
        return _dot(act_ref[...], w2_ref[...])

    @pl.when(j == 0)
    def _():
        h = _rms(x_ref[...], g_ref[...]) * (1.0 + sc_ref[0]) + sh_ref[0]
        h_ref[...] = h.astype(BF16)
        gate()

    @pl.when(j == 1)
    def _():
        o_ref[...] = down()
        gate()

    @pl.when((j > 1) & (j < last))
    def _():
        o_ref[...] += down()
        gate()

    @pl.when(j == last)
    def _():
        prev = down()
        gate()
        y = x_ref[...] + 0.5 * gt_ref[0] * (o_ref[...] + prev + _dot(act_ref[...], w2_last_ref[...]))
        if final_norm:
            y = _rms(y, gf_ref[...])
        o_ref[...] = y


def _ffn(x, mod, chunk0, tiles_per_b, g, w1, w3, w2, gf=None, *, tm, tf=512):
    n = x.shape[0]
    r = mod.shape[1]
    n_ff = D_FF // tf
    assert n_ff >= 3

    def mod_spec(k):
        return pl.BlockSpec((1, r, D_MODEL), lambda i, j: (i // tiles_per_b, 0, k))

    in_specs = [
        pl.BlockSpec((tm, D_MODEL), lambda i, j: (i, 0)),
        mod_spec(chunk0), mod_spec(chunk0 + 1), mod_spec(chunk0 + 2),
        pl.BlockSpec((1, D_MODEL), lambda i, j: (0, 0)),
        pl.BlockSpec((D_MODEL, tf), lambda i, j: (0, j)),
        pl.BlockSpec((D_MODEL, tf), lambda i, j: (0, j)),
        pl.BlockSpec((tf, D_MODEL), lambda i, j: (jnp.maximum(j - 1, 0), 0)),
        pl.BlockSpec((tf, D_MODEL), lambda i, j: (n_ff - 1, 0)),
    ]
    args = [x, mod, mod, mod, g.reshape(1, D_MODEL), w1, w3, w2, w2]
    if gf is not None:
        in_specs.append(pl.BlockSpec((1, D_MODEL), lambda i, j: (0, 0)))
        args.append(gf.reshape(1, D_MODEL))
    return pl.pallas_call(
        functools.partial(_ffn_kernel, final_norm=gf is not None),
        grid=(n // tm, n_ff),
        in_specs=in_specs,
        out_specs=pl.BlockSpec((tm, D_MODEL), lambda i, j: (i, 0)),
        out_shape=jax.ShapeDtypeStruct((n, D_MODEL), F32),
        scratch_shapes=[pltpu.VMEM((tm, D_MODEL), BF16), pltpu.VMEM((tm, tf), BF16)],
        compiler_params=_cparams(("parallel", "arbitrary")),
        name="macaron_ffn",
    )(*args)


def _rope_rows(x, cos, sin):
    half = QK_ROPE // 2
    x1 = x[:half]
    x2 = x[half:]
    return x1 * cos - x2 * sin, x1 * sin + x2 * cos


def _mixin_kernel(x_ref, sh_ref, sc_ref, g_ref, wq_ref, wkv_ref, wkr_ref, wu_ref, gq_ref, wqn_ref, wqr_ref,
                  wuk_ref, gkv_ref, cos_ref, sin_ref,
                  qlat_ref, qrope_ref, ckv_ref, kc_ref, kct_ref, krope_ref, kr_ref, u_ref, *, tq):
    half = QK_ROPE // 2
    n_q = x_ref.shape[0] // tq
    h = (_rms(x_ref[...], g_ref[...]) * (1.0 + sc_ref[0]) + sh_ref[0]).astype(BF16)
    cos = cos_ref[...]
    sin = sin_ref[...]

    u_ref[0] = _dot(h, wu_ref[...])

    ckv = _rms(_dot(h, wkv_ref[...]), gkv_ref[...])
    ckv_ref[0] = ckv
    kc_ref[0] = ckv.astype(BF16)
    kct_ref[0] = ckv.T.astype(BF16)

    k1, k2 = _rope_rows(_dot_nt(wkr_ref[...], h), cos, sin)
    kr_t = jnp.concatenate([k1, k2], axis=0)
    krope_ref[0] = kr_t
    kr_ref[0] = kr_t.T.astype(BF16)

    qn = _rms(_dot(h, wq_ref[...]), gq_ref[...]).astype(BF16)
    q_nope = (_dot(qn, wqn_ref[...]) * SOFTMAX_SCALE).astype(BF16)
    q_rope_t = _dot_nt(wqr_ref[...], qn) * SOFTMAX_SCALE
    for hd in range(MLA_HEADS):
        ql_t = _dot_nt(wuk_ref[hd], q_nope[:, hd * QK_NOPE:(hd + 1) * QK_NOPE]).astype(BF16)
        r1, r2 = _rope_rows(q_rope_t[hd * QK_ROPE:(hd + 1) * QK_ROPE], cos, sin)
        r1 = r1.astype(BF16)
        r2 = r2.astype(BF16)
        for qq in range(n_q):
            src = slice(qq * tq, (qq + 1) * tq)
            dst = slice(hd * tq, (hd + 1) * tq)
            qlat_ref[0, qq, :, dst] = ql_t[:, src]
            qrope_ref[0, qq, :half, dst] = r1[:, src]
            qrope_ref[0, qq, half:, dst] = r2[:, src]


def _mixin(x, mod, tiles_per_b, t_len, p, cos_t, sin_t, *, tm, tq):
    n = x.shape[0]
    nb = n // t_len
    r = mod.shape[1]
    half = QK_ROPE // 2
    full = lambda shape: pl.BlockSpec(shape, lambda i: (0,) * len(shape))
    rows = lambda width: pl.BlockSpec((1, tm, width), lambda i: (i // tiles_per_b, i % tiles_per_b, 0))
    cols = lambda height: pl.BlockSpec((1, height, tm), lambda i: (i // tiles_per_b, 0, i % tiles_per_b))
    head_cols = lambda height: pl.BlockSpec((1, tm // tq, height, MLA_HEADS * tq),
                                            lambda i: (i // tiles_per_b, i % tiles_per_b, 0, 0))
    in_specs = [
        pl.BlockSpec((tm, D_MODEL), lambda i: (i, 0)),
        pl.BlockSpec((1, r, D_MODEL), lambda i: (i // tiles_per_b, 0, 3)),
        pl.BlockSpec((1, r, D_MODEL), lambda i: (i // tiles_per_b, 0, 4)),
        full((1, D_MODEL)),
        full((D_MODEL, Q_LORA)), full((D_MODEL, KV_LORA)), full((QK_ROPE, D_MODEL)), full((D_MODEL, S5_WIDTH)),
        full((1, Q_LORA)), full((Q_LORA, MLA_HEADS * QK_NOPE)), full((MLA_HEADS * QK_ROPE, Q_LORA)),
        full((MLA_HEADS, KV_LORA, QK_NOPE)), full((1, KV_LORA)),
        pl.BlockSpec((half, tm), lambda i: (0, i % tiles_per_b)),
        pl.BlockSpec((half, tm), lambda i: (0, i % tiles_per_b)),
    ]
    out_specs = [head_cols(KV_LORA), head_cols(QK_ROPE), rows(KV_LORA), rows(KV_LORA), cols(KV_LORA),
                 cols(QK_ROPE), rows(QK_ROPE), rows(S5_WIDTH)]
    out_shape = [
        jax.ShapeDtypeStruct((nb, t_len // tq, KV_LORA, MLA_HEADS * tq), BF16),
        jax.ShapeDtypeStruct((nb, t_len // tq, QK_ROPE, MLA_HEADS * tq), BF16),
        jax.ShapeDtypeStruct((nb, t_len, KV_LORA), F32),
        jax.ShapeDtypeStruct((nb, t_len, KV_LORA), BF16),
        jax.ShapeDtypeStruct((nb, KV_LORA, t_len), BF16),
        jax.ShapeDtypeStruct((nb, QK_ROPE, t_len), F32),
        jax.ShapeDtypeStruct((nb, t_len, QK_ROPE), BF16),
        jax.ShapeDtypeStruct((nb, t_len, S5_WIDTH), F32),
    ]
    return pl.pallas_call(
        functools.partial(_mixin_kernel, tq=tq),
        grid=(n // tm,),
        in_specs=in_specs,
        out_specs=out_specs,
        out_shape=out_shape,
        compiler_params=_cparams(("parallel",)),
        name="mixer_in",
    )(x, mod, mod, p["norm_mix"], p["w_q"], p["w_kv"], p["w_kr_t"], p["w_u"], p["norm_q"], p["w_uq_nope"],
      p["w_uq_rope_t"], p["w_uk_h"], p["norm_kv"], cos_t, sin_t)


def _prompt_attn_kernel(ql_ref, qr_ref, kc_ref, kr_ref, kct_ref, wuv_ref, g_ref, o_ref, *scratch, tq, tk):
    qi = pl.program_id(1)
    n_half = len(scratch) // 3
    m_ref, l_ref, acc_ref = scratch[:n_half], scratch[n_half:2 * n_half], scratch[2 * n_half:]
    width = m_ref[0].shape[1]
    heads_per_half = width // tq
    for hf in range(n_half):
        m_ref[hf][...] = jnp.full_like(m_ref[hf], -jnp.inf)
        l_ref[hf][...] = jnp.zeros_like(l_ref[hf])
        acc_ref[hf][...] = jnp.zeros_like(acc_ref[hf])

    def step(ki, diag):
        start = pl.multiple_of(ki * tk, tk)
        kc = kc_ref[0, pl.ds(start, tk), :]
        kr = kr_ref[0, pl.ds(start, tk), :]
        kct = kct_ref[0, :, pl.ds(start, tk)]
        masked = diag is not None
        if masked:
            k_local = lax.broadcasted_iota(jnp.int32, (tk, width), 0) + diag * tk
            t_local = lax.broadcasted_iota(jnp.int32, (tk, width), 1) & (tq - 1)
            keep = k_local <= t_local
        for hf in range(n_half):
            lanes = slice(hf * width, (hf + 1) * width)
            s = _dot(kc, ql_ref[0, 0, :, lanes]) + _dot(kr, qr_ref[0, 0, :, lanes])
            if masked:
                s = jnp.where(keep, s, -jnp.inf)
            m_old = m_ref[hf][...]
            m_new = jnp.maximum(m_old, jnp.max(s, axis=0, keepdims=True))
            alpha = jnp.exp(m_old - m_new)
            pexp = jnp.exp(s - m_new)
            l_ref[hf][...] = alpha * l_ref[hf][...] + jnp.sum(pexp, axis=0, keepdims=True)
            acc_ref[hf][...] = alpha * acc_ref[hf][...] + _dot(kct, pexp.astype(BF16))
            m_ref[hf][...] = m_new

    def body(ki, carry):
        step(ki, None)
        return carry

    blocks_per_tile = tq // tk
    lax.fori_loop(0, qi * blocks_per_tile, body, 0)
    for d in range(blocks_per_tile):
        step(qi * blocks_per_tile + d, d)
    parts = []
    for hd in range(MLA_HEADS):
        hf = hd // heads_per_half
        lanes = slice((hd % heads_per_half) * tq, (hd % heads_per_half + 1) * tq)
        o_t = acc_ref[hf][:, lanes] / l_ref[hf][:, lanes]
        parts.append(_dot(wuv_ref[hd], o_t.astype(BF16)))
    attn_t = _rms_rows(jnp.concatenate(parts, axis=0), g_ref[...])
    o_ref[0] = attn_t.T.astype(BF16)


def _prompt_attn(q_lat_t, q_rope_t, kc, kr, kc_t, w_uv_h, g_col, *, tq, tk=256, n_half=1):
    nb, n_q, _, _ = q_lat_t.shape
    t_len = n_q * tq
    width = MLA_HEADS * tq // n_half
    assert tq % tk == 0
    return pl.pallas_call(
        functools.partial(_prompt_attn_kernel, tq=tq, tk=tk),
        grid=(nb, n_q),
        in_specs=[
            pl.BlockSpec((1, 1, KV_LORA, MLA_HEADS * tq), lambda b, i: (b, i, 0, 0)),
            pl.BlockSpec((1, 1, QK_ROPE, MLA_HEADS * tq), lambda b, i: (b, i, 0, 0)),
            pl.BlockSpec((1, t_len, KV_LORA), lambda b, i: (b, 0, 0)),
            pl.BlockSpec((1, t_len, QK_ROPE), lambda b, i: (b, 0, 0)),
            pl.BlockSpec((1, KV_LORA, t_len), lambda b, i: (b, 0, 0)),
            pl.BlockSpec((MLA_HEADS, V_HEAD, KV_LORA), lambda b, i: (0, 0, 0)),
            pl.BlockSpec((MLA_WIDTH, 1), lambda b, i: (0, 0)),
        ],
        out_specs=pl.BlockSpec((1, tq, MLA_WIDTH), lambda b, i: (b, i, 0)),
        out_shape=jax.ShapeDtypeStruct((nb, t_len, MLA_WIDTH), BF16),
        scratch_shapes=([pltpu.VMEM((1, width), F32)] * (2 * n_half) + [pltpu.VMEM((KV_LORA, width), F32)] * n_half),
        compiler_params=_cparams(("parallel", "parallel")),
        name="prompt_attention",
    )(q_lat_t, q_rope_t, kc, kr, kc_t, w_uv_h, g_col)


def _sample_attn_kernel(pt_ref, ql_ref, qr_ref, kcn_ref, krn_ref, ckv_hbm, kr_hbm, o_ref, kbuf, rbuf, sem_k, sem_r,
                        *, pages, streams):
    nb, n_pages = pt_ref.shape
    n_chunks = n_pages // pages
    n_groups = nb // streams
    total = n_groups * n_chunks
    n_slots = kbuf.shape[0]
    ahead = n_slots - 1

    def chunk_copies(g, slot):
        grp = g // n_chunks
        c = g % n_chunks
        copies = []
        for st in range(streams):
            b = grp * streams + st
            for i in range(pages):
                page = pt_ref[b, c * pages + i]
                copies.append(pltpu.make_async_copy(
                    ckv_hbm.at[0, page], kbuf.at[slot, st, pl.ds(i * PAGE, PAGE), :], sem_k.at[slot, st]))
                copies.append(pltpu.make_async_copy(
                    kr_hbm.at[0, page], rbuf.at[slot, st, i], sem_r.at[slot, st]))
        return copies

    grp = pl.program_id(0)

    @pl.when(grp == 0)
    def _():
        for g0 in range(ahead):
            for cp in chunk_copies(g0, g0):
                cp.start()

    qs = [(ql_ref[st], qr_ref[st]) for st in range(streams)]

    def chunk_body(c, carry):
        g = grp * n_chunks + c
        slot = g % n_slots

        @pl.when(g + ahead < total)
        def _():
            for cp in chunk_copies(g + ahead, (g + ahead) % n_slots):
                cp.start()

        for cp in chunk_copies(g, slot):
            cp.wait()
        out = []
        for st in range(streams):
            m_old, l_old, acc_old = carry[st]
            ql, qr = qs[st]
            kc = kbuf[slot, st].astype(BF16)
            kr_t = jnp.concatenate([rbuf[slot, st, i].astype(BF16) for i in range(pages)], axis=1)
            s = _dot_nt(ql, kc) + _dot(qr, kr_t)
            m_new = jnp.maximum(m_old, jnp.max(s, axis=1, keepdims=True))
            alpha = jnp.exp(m_old - m_new)
            pexp = jnp.exp(s - m_new)
            l_new = alpha * l_old + jnp.sum(pexp, axis=1, keepdims=True)
            acc_new = alpha * acc_old + _dot(pexp.astype(BF16), kc)
            out.append((m_new, l_new, acc_new))
        return tuple(out)

    init = tuple((jnp.full((MLA_HEADS, 1), -jnp.inf, F32), jnp.zeros((MLA_HEADS, 1), F32),
                  jnp.zeros((MLA_HEADS, KV_LORA), F32)) for _ in range(streams))
    final = lax.fori_loop(0, n_chunks, chunk_body, init)
    for st in range(streams):
        m_old, l_old, acc_old = final[st]
        ql, qr = qs[st]
        kcn = kcn_ref[st].astype(BF16).astype(F32)
        krn = krn_ref[st].astype(BF16).astype(F32)
        s_n = (jnp.sum(ql.astype(F32) * kcn, axis=1, keepdims=True)
               + jnp.sum(qr.astype(F32) * krn, axis=1, keepdims=True))
        m_f = jnp.maximum(m_old, s_n)
        a_f = jnp.exp(m_old - m_f)
        p_n = jnp.exp(s_n - m_f)
        l_f = a_f * l_old + p_n
        acc_f = a_f * acc_old + p_n.astype(BF16).astype(F32) * kcn
        o_ref[st] = acc_f / l_f


def _sample_attn(page_table, q_lat, q_rope, ckv_new, krope_new, cache_ckv, cache_krope_t, *, pages=16, streams=2,
                 slots=3):
    nb, n_pages = page_table.shape
    assert n_pages % pages == 0 and nb % streams == 0 and (nb // streams) * (n_pages // pages) >= slots
    per_group = lambda rows, width: pl.BlockSpec((streams, rows, width), lambda g, pt: (g, 0, 0))
    grid_spec = pltpu.PrefetchScalarGridSpec(
        num_scalar_prefetch=1,
        grid=(nb // streams,),
        in_specs=[
            per_group(MLA_HEADS, KV_LORA), per_group(MLA_HEADS, QK_ROPE),
            per_group(1, KV_LORA), per_group(1, QK_ROPE),
            pl.BlockSpec(memory_space=pl.ANY), pl.BlockSpec(memory_space=pl.ANY),
        ],
        out_specs=per_group(MLA_HEADS, KV_LORA),
        scratch_shapes=[
            pltpu.VMEM((slots, streams, pages * PAGE, KV_LORA), F32),
            pltpu.VMEM((slots, streams, pages, QK_ROPE, PAGE), F32),
            pltpu.SemaphoreType.DMA((slots, streams)),
            pltpu.SemaphoreType.DMA((slots, streams)),
        ],
    )
    return pl.pallas_call(
        functools.partial(_sample_attn_kernel, pages=pages, streams=streams),
        grid_spec=grid_spec,
        out_shape=jax.ShapeDtypeStruct((nb, MLA_HEADS, KV_LORA), F32),
        compiler_params=_cparams(("arbitrary",)),
        name="sample_paged_attention",
    )(page_table, q_lat, q_rope, ckv_new, krope_new, cache_ckv, cache_krope_t)


def _sample_attn_out_kernel(o_ref, wuv_ref, g_ref, out_ref):
    parts = [_dot_nt(o_ref[hd].astype(BF16), wuv_ref[hd]) for hd in range(MLA_HEADS)]
    out_ref[...] = _rms(jnp.concatenate(parts, axis=1), g_ref[...]).astype(BF16)


def _sample_attn_out(o_lat_h, w_uv_h, g):
    nb = o_lat_h.shape[1]
    return pl.pallas_call(
        _sample_attn_out_kernel,
        out_shape=jax.ShapeDtypeStruct((nb, MLA_WIDTH), BF16),
        name="sample_attention_out",
    )(o_lat_h, w_uv_h, g)


def _same_group(shape, row_shift, col_shift):
    rows = lax.broadcasted_iota(jnp.int32, shape, 0)
    cols = lax.broadcasted_iota(jnp.int32, shape, 1)
    mask = GROUPS_PER_SLAB - 1
    return ((rows >> row_shift) & mask) == ((cols >> col_shift) & mask)


def _s5_prompt_kernel(u_ref, wr_ref, we_ref, wf_ref, tr_ref, te_ref, tf_ref, are_ref, aim_ref, d_ref,
                      y_ref, hre_ref, him_ref, r_ref, e_ref, f_ref, uc_ref, s_ref, hp_ref, yc_ref):
    n_chunks = uc_ref.shape[0]
    n_blk = CHUNK_COLS // MXU_TILE
    ch_bits = S5_GROUP.bit_length() - 1
    st_bits = S5_STATE.bit_length() - 1

    @pl.when(pl.program_id(1) == 0)
    def _():
        zero = jnp.zeros((), F32)
        r_ref[...] = jnp.where(_same_group(r_ref.shape, ch_bits, ch_bits),
                               _dot(wr_ref[0], tr_ref[...]), zero).astype(BF16)
        e_ref[...] = jnp.where(_same_group(e_ref.shape, ch_bits, st_bits),
                               _dot(we_ref[0], te_ref[...]), zero).astype(BF16)
        f_ref[...] = jnp.where(_same_group(f_ref.shape, st_bits, ch_bits),
                               _dot(tf_ref[...], wf_ref[0]), zero).astype(BF16)

    for s in range(S5_CHUNK):
        uc_ref[:, s * LANES:(s + 1) * LANES] = u_ref[0, pl.ds(s, n_chunks, stride=S5_CHUNK), :]
    uc = uc_ref[...]
    ub = uc.astype(BF16)

    s_ref[...] = _dot(ub, e_ref[...])

    a_re = are_ref[0]
    a_im = aim_ref[0]

    def scan(k, carry):
        h_re, h_im = carry
        hp_ref[pl.ds(k, 1), :] = jnp.concatenate([h_re, h_im], axis=1)
        row = s_ref[pl.ds(k, 1), :]
        n_re = a_re * h_re - a_im * h_im + row[:, :SLAB_STATE]
        n_im = a_re * h_im + a_im * h_re + row[:, SLAB_STATE:]
        return n_re, n_im

    zero_state = jnp.zeros((1, SLAB_STATE), F32)
    h_re, h_im = lax.fori_loop(0, n_chunks, scan, (zero_state, zero_state), unroll=4)
    hre_ref[0, 0] = h_re
    him_ref[0, 0] = h_im

    yc_ref[...] = _dot(hp_ref[...].astype(BF16), f_ref[...]) + uc * d_ref[0]
    for tb in range(n_blk):
        yc_ref[:, tb * MXU_TILE:(tb + 1) * MXU_TILE] += _dot(ub[:, :(tb + 1) * MXU_TILE],
                                                             r_ref[(n_blk - 1 - tb) * MXU_TILE:, :])
    y = jax.nn.gelu(yc_ref[...])
    for t in range(S5_CHUNK):
        y_ref[0, pl.ds(t, n_chunks, stride=S5_CHUNK), :] = y[:, t * LANES:(t + 1) * LANES]


def _s5_prompt(u, ops):
    nb, t_len, _ = u.shape
    n_chunks = t_len // S5_CHUNK
    slab = lambda shape: pl.BlockSpec((1,) + shape, lambda j, b: (j,) + (0,) * len(shape))
    full = lambda shape: pl.BlockSpec(shape, lambda j, b: (0,) * len(shape))
    y, hre, him = pl.pallas_call(
        _s5_prompt_kernel,
        grid=(N_SLABS, nb),
        in_specs=[
            pl.BlockSpec((1, t_len, LANES), lambda j, b: (b, 0, j)),
            slab((CHUNK_COLS, 2 * S5_GROUP)), slab((CHUNK_COLS, 2 * S5_STATE)), slab((2 * S5_STATE, CHUNK_COLS)),
            full((2 * S5_GROUP, MXU_TILE)), full((2 * S5_STATE, 2 * SLAB_STATE)), full((2 * SLAB_STATE, 2 * S5_STATE)),
            slab((1, SLAB_STATE)), slab((1, SLAB_STATE)), slab((1, CHUNK_COLS)),
        ],
        out_specs=[
            pl.BlockSpec((1, t_len, LANES), lambda j, b: (b, 0, j)),
            pl.BlockSpec((1, 1, 1, SLAB_STATE), lambda j, b: (j, b, 0, 0)),
            pl.BlockSpec((1, 1, 1, SLAB_STATE), lambda j, b: (j, b, 0, 0)),
        ],
        out_shape=[
            jax.ShapeDtypeStruct((nb, t_len, S5_WIDTH), F32),
            jax.ShapeDtypeStruct((N_SLABS, nb, 1, SLAB_STATE), F32),
            jax.ShapeDtypeStruct((N_SLABS, nb, 1, SLAB_STATE), F32),
        ],
        scratch_shapes=[
            pltpu.VMEM((CHUNK_COLS, MXU_TILE), BF16),
            pltpu.VMEM((CHUNK_COLS, 2 * SLAB_STATE), BF16),
            pltpu.VMEM((2 * SLAB_STATE, CHUNK_COLS), BF16),
            pltpu.VMEM((n_chunks, CHUNK_COLS), F32),
            pltpu.VMEM((n_chunks, 2 * SLAB_STATE), F32),
            pltpu.VMEM((n_chunks, 2 * SLAB_STATE), F32),
            pltpu.VMEM((n_chunks, CHUNK_COLS), F32),
        ],
        compiler_params=_cparams(("arbitrary", "arbitrary")),
        name="s5_prompt",
    )(u, ops["wr"], ops["we"], ops["wf"], ops["tile_r"], ops["tile_e"], ops["tile_f"], ops["a_re"], ops["a_im"],
      ops["d_chunk"])

    def to_state(h):
        return h.reshape(N_SLABS, nb, GROUPS_PER_SLAB, S5_STATE).transpose(1, 0, 2, 3).reshape(nb, S5_GROUPS, S5_STATE)

    return y, to_state(hre), to_state(him)


def _split_bf16(x):
    hi = x.astype(BF16)
    return hi, (x - hi.astype(F32)).astype(BF16)


def _s5_sample_kernel(u_ref, h0re_ref, h0im_ref, bd_hi_ref, bd_lo_ref, cd_ref, lre_ref, lim_ref, d_ref,
                      y_ref, hre_ref, him_ref):
    u = u_ref[...]
    u_hi, u_lo = _split_bf16(u)
    bu = _dot(u_hi, bd_hi_ref[0]) + (_dot(u_hi, bd_lo_ref[0]) + _dot(u_lo, bd_hi_ref[0]))
    l_re = lre_ref[0]
    l_im = lim_ref[0]
    h0_re = h0re_ref[...]
    h0_im = h0im_ref[...]
    h_re = l_re * h0_re - l_im * h0_im + bu[:, :SLAB_STATE]
    h_im = l_re * h0_im + l_im * h0_re + bu[:, SLAB_STATE:]
    hre_ref[...] = h_re
    him_ref[...] = h_im
    h = jnp.concatenate([h_re, h_im], axis=1).astype(BF16)
    y_ref[...] = jax.nn.gelu(_dot(h, cd_ref[0]) + u * d_ref[0])


def _s5_sample(u, h0_re, h0_im, ops):
    nb = u.shape[0]
    n_state = S5_GROUPS * S5_STATE
    slab = lambda shape: pl.BlockSpec((1,) + shape, lambda j: (j,) + (0,) * len(shape))
    col = lambda width: pl.BlockSpec((nb, width), lambda j: (0, j))
    y, hre, him = pl.pallas_call(
        _s5_sample_kernel,
        grid=(N_SLABS,),
        in_specs=[
            col(LANES), col(SLAB_STATE), col(SLAB_STATE),
            slab((LANES, 2 * SLAB_STATE)), slab((LANES, 2 * SLAB_STATE)), slab((2 * SLAB_STATE, LANES)),
            slab((1, SLAB_STATE)), slab((1, SLAB_STATE)), slab((1, LANES)),
        ],
        out_specs=[col(LANES), col(SLAB_STATE), col(SLAB_STATE)],
        out_shape=[
            jax.ShapeDtypeStruct((nb, S5_WIDTH), F32),
            jax.ShapeDtypeStruct((nb, n_state), F32),
            jax.ShapeDtypeStruct((nb, n_state), F32),
        ],
        compiler_params=_cparams(("parallel",)),
        name="s5_sample",
    )(u, h0_re.reshape(nb, n_state), h0_im.reshape(nb, n_state), ops["bd_hi"], ops["bd_lo"], ops["cd"],
      ops["l_re"], ops["l_im"], ops["d_slab"])
    return y, hre.reshape(nb, S5_GROUPS, S5_STATE), him.reshape(nb, S5_GROUPS, S5_STATE)


def _mixout_kernel(x_ref, attn_ref, y_ref, gt_ref, wglu_ref, bglu_ref, gs_ref, woa_ref, woy_ref, o_ref):
    y = y_ref[...]
    z = _dot(y.astype(BF16), wglu_ref[...]) + bglu_ref[...]
    yn = _rms(y * jax.nn.sigmoid(z), gs_ref[...]).astype(BF16)
    mix = _dot(attn_ref[...], woa_ref[...]) + _dot(yn, woy_ref[...])
    o_ref[...] = x_ref[...] + gt_ref[0] * mix


def _mixout(x, attn, y, mod, tiles_per_b, p, *, tm):
    n = x.shape[0]
    r = mod.shape[1]
    full = lambda shape: pl.BlockSpec(shape, lambda i: (0,) * len(shape))
    return pl.pallas_call(
        _mixout_kernel,
        grid=(n // tm,),
        in_specs=[
            pl.BlockSpec((tm, D_MODEL), lambda i: (i, 0)),
            pl.BlockSpec((tm, MLA_WIDTH), lambda i: (i, 0)),
            pl.BlockSpec((tm, S5_WIDTH), lambda i: (i, 0)),
            pl.BlockSpec((1, r, D_MODEL), lambda i: (i // tiles_per_b, 0, 5)),
            full((S5_WIDTH, S5_WIDTH)), full((1, S5_WIDTH)), full((1, S5_WIDTH)),
            full((MLA_WIDTH, D_MODEL)), full((S5_WIDTH, D_MODEL)),
        ],
        out_specs=pl.BlockSpec((tm, D_MODEL), lambda i: (i, 0)),
        out_shape=jax.ShapeDtypeStruct((n, D_MODEL), F32),
        compiler_params=_cparams(("parallel",)),
        name="mixer_out",
    )(x, attn, y, mod, p["w_glu"], p["b_glu"], p["norm_ssm_out"], p["w_out_attn"], p["w_out_ssm"])


def _rope_tables(pos):
    half = QK_ROPE // 2
    inv_freq = ROPE_THETA ** (-jnp.arange(half, dtype=F32) / half)
    ang = pos.astype(F32)[:, None] * inv_freq[None, :]
    return jnp.cos(ang).T, jnp.sin(ang).T


def _group_diag(x, g_axis, new_axis):
    x = jnp.expand_dims(x, new_axis)
    shape = [1] * x.ndim
    shape[g_axis if g_axis < new_axis else g_axis + 1] = GROUPS_PER_SLAB
    shape[new_axis] = GROUPS_PER_SLAB
    return x * jnp.eye(GROUPS_PER_SLAB, dtype=x.dtype).reshape(shape)


def _by_slab(x, g_axis):
    return x.reshape(x.shape[:g_axis] + (N_SLABS, GROUPS_PER_SLAB) + x.shape[g_axis + 1:])


def _replicate_over_groups(n_outer, n_inner):
    eo = jnp.eye(n_outer, dtype=F32)[:, None, :, None, None]
    ei = jnp.eye(n_inner, dtype=F32)[None, :, None, None, :]
    ones = jnp.ones((1, 1, 1, GROUPS_PER_SLAB, 1), F32)
    return (eo * ei * ones).reshape(n_outer * n_inner, n_outer * GROUPS_PER_SLAB * n_inner)


def _s5_operators(a_re, a_im, log_dt, b_re, b_im, c_re, c_im, d_skip):
    hp = lax.Precision.HIGHEST
    a_re, a_im, b_re, b_im, c_re, c_im = (v.astype(F32) for v in (a_re, a_im, b_re, b_im, c_re, c_im))
    dt = jnp.exp(log_dt.astype(F32))[:, None]
    z_re = a_re * dt
    z_im = a_im * dt

    def lam_pow(n):
        mag = jnp.exp(z_re[None] * n[:, None, None])
        ang = z_im[None] * n[:, None, None]
        return mag * jnp.cos(ang), mag * jnp.sin(ang)

    steps = jnp.arange(S5_CHUNK + 1, dtype=F32)
    pw_re, pw_im = lam_pow(steps)
    lb_re, lb_im = pw_re[1], pw_im[1]
    den = a_re * a_re + a_im * a_im
    q_re = ((lb_re - 1.0) * a_re + lb_im * a_im) / den
    q_im = (lb_im * a_re - (lb_re - 1.0) * a_im) / den
    bb_re = q_re[:, :, None] * b_re - q_im[:, :, None] * b_im
    bb_im = q_re[:, :, None] * b_im + q_im[:, :, None] * b_re

    cp_re = c_re[None] * pw_re[:S5_CHUNK, :, None, :] - c_im[None] * pw_im[:S5_CHUNK, :, None, :]
    cp_im = c_re[None] * pw_im[:S5_CHUNK, :, None, :] + c_im[None] * pw_re[:S5_CHUNK, :, None, :]
    m = jnp.einsum("ngcq,gqd->ngcd", jnp.concatenate([cp_re, -cp_im], axis=3),
                   jnp.concatenate([bb_re, bb_im], axis=1), precision=hp)
    m_ext = jnp.concatenate([m, jnp.zeros_like(m[:1])], axis=0)
    n_blk = S5_CHUNK // 2
    d_i = (n_blk - 1 - jnp.arange(n_blk))[:, None, None]
    s_i = jnp.arange(2)[None, :, None]
    t_i = jnp.arange(2)[None, None, :]
    lag = 2 * d_i + t_i - s_i
    mg = _by_slab(m_ext[lag], 3)
    wr = mg.transpose(3, 0, 1, 4, 6, 2, 5).reshape(N_SLABS, CHUNK_COLS, 2 * S5_GROUP)

    pr_re, pr_im = lam_pow(S5_CHUNK - 1 - steps[:S5_CHUNK])
    w_re = pr_re[:, :, :, None] * bb_re[None] - pr_im[:, :, :, None] * bb_im[None]
    w_im = pr_re[:, :, :, None] * bb_im[None] + pr_im[:, :, :, None] * bb_re[None]

    def e_half(v):
        return _by_slab(v, 1).transpose(1, 0, 2, 4, 3).reshape(N_SLABS, CHUNK_COLS, S5_STATE)

    we = jnp.concatenate([e_half(w_re), e_half(w_im)], axis=2)

    g_re = c_re[None] * pw_re[1:, :, None, :] - c_im[None] * pw_im[1:, :, None, :]
    g_im = c_re[None] * pw_im[1:, :, None, :] + c_im[None] * pw_re[1:, :, None, :]

    def f_half(v):
        return _by_slab(v, 1).transpose(1, 4, 0, 2, 3).reshape(N_SLABS, S5_STATE, CHUNK_COLS)

    wf = jnp.concatenate([f_half(g_re), f_half(-g_im)], axis=1)

    def slab_vec(v):
        return v.reshape(N_SLABS, 1, SLAB_STATE)

    d_slab = d_skip.astype(F32).reshape(N_SLABS, 1, LANES)
    tile_e = _replicate_over_groups(2, S5_STATE)

    def bd_half(v):
        return _group_diag(_by_slab(v, 0).transpose(0, 1, 3, 2), 1, 3).reshape(N_SLABS, LANES, SLAB_STATE)

    def cd_half(v):
        return _group_diag(_by_slab(v, 0).transpose(0, 1, 3, 2), 1, 3).reshape(N_SLABS, SLAB_STATE, LANES)

    bd = jnp.concatenate([bd_half(bb_re), bd_half(bb_im)], axis=2)
    cd = jnp.concatenate([cd_half(c_re), cd_half(-c_im)], axis=1)
    bd_hi = bd.astype(BF16)
    bd_lo = (bd - bd_hi.astype(F32)).astype(BF16)
    return {
        "wr": wr.astype(BF16), "we": we.astype(BF16), "wf": wf.astype(BF16),
        "tile_r": _replicate_over_groups(2, S5_GROUP).astype(BF16),
        "tile_e": tile_e.astype(BF16), "tile_f": tile_e.T.astype(BF16),
        "a_re": slab_vec(pw_re[S5_CHUNK]), "a_im": slab_vec(pw_im[S5_CHUNK]),
        "d_chunk": jnp.tile(d_slab, (1, 1, S5_CHUNK)), "d_slab": d_slab,
        "bd_hi": bd_hi, "bd_lo": bd_lo, "cd": cd.astype(BF16),
        "l_re": slab_vec(lb_re), "l_im": slab_vec(lb_im),
    }


def _layer_params(w_in, w_uq, w_uk, w_uv, w_glu, w_out, norm_mix, norm_q, norm_kv, norm_attn_out, norm_ssm_out,
                  b_glu):
    c0, c1, c2 = Q_LORA, Q_LORA + KV_LORA, Q_LORA + KV_LORA + QK_ROPE
    w_uq_h = w_uq.reshape(Q_LORA, MLA_HEADS, QK_NOPE + QK_ROPE)
    return {
        "w_q": w_in[:, :c0].astype(BF16),
        "w_kv": w_in[:, c0:c1].astype(BF16),
        "w_kr_t": w_in[:, c1:c2].T.astype(BF16),
        "w_u": w_in[:, c2:].astype(BF16),
        "w_uq_nope": w_uq_h[:, :, :QK_NOPE].reshape(Q_LORA, MLA_HEADS * QK_NOPE).astype(BF16),
        "w_uq_rope_t": w_uq_h[:, :, QK_NOPE:].reshape(Q_LORA, MLA_HEADS * QK_ROPE).T.astype(BF16),
        "w_uk_h": w_uk.transpose(1, 0, 2).astype(BF16),
        "w_uv_h": w_uv.transpose(1, 2, 0).astype(BF16),
        "w_glu": w_glu.astype(BF16),
        "w_out_attn": w_out[:MLA_WIDTH].astype(BF16),
        "w_out_ssm": w_out[MLA_WIDTH:].astype(BF16),
        "norm_mix": norm_mix.reshape(1, D_MODEL),
        "norm_q": norm_q.reshape(1, Q_LORA),
        "norm_kv": norm_kv.reshape(1, KV_LORA),
        "norm_attn_out": norm_attn_out.reshape(1, MLA_WIDTH),
        "norm_attn_out_col": norm_attn_out.reshape(MLA_WIDTH, 1),
        "norm_ssm_out": norm_ssm_out.reshape(1, S5_WIDTH),
        "b_glu": b_glu.reshape(1, S5_WIDTH),
    }


def kernel(x_prompt, x_sample, c_prompt, c_sample, cache_ckv, cache_krope, state_s5_re, state_s5_im, page_table, w_ada, b_ada, norm_ffn1, ffn1_w1, ffn1_w3, ffn1_w2, norm_mix, w_in, norm_q, w_uq, norm_kv, w_uk, w_uv, s5_a_re, s5_a_im, s5_log_dt, s5_b_re, s5_b_im, s5_c_re, s5_c_im, s5_d, w_glu, b_glu, norm_attn_out, norm_ssm_out, w_out, norm_ffn2, ffn2_w1, ffn2_w3, ffn2_w2, norm_final):
    bp, seq, _ = x_prompt.shape
    bs = x_sample.shape[0]
    depth = w_ada.shape[0]
    assert depth == 1 and x_sample.shape[1] == 1
    n_pages = page_table.shape[1]
    past_len = n_pages * PAGE
    l = 0

    pad = (-(bs + bp)) % 8
    c_all = jnp.concatenate([c_sample, c_prompt, jnp.zeros((pad, D_MODEL), F32)], axis=0)
    mod = _ada(c_all, w_ada[l], b_ada[l])
    mod_s = mod[:bs].reshape(1, bs, ADA_CHUNKS * D_MODEL)
    mod_p = mod[bs:bs + bp].reshape(bp, 1, ADA_CHUNKS * D_MODEL)

    p = _layer_params(w_in[l], w_uq[l], w_uk[l], w_uv[l], w_glu[l], w_out[l], norm_mix[l], norm_q[l], norm_kv[l],
                      norm_attn_out[l], norm_ssm_out[l], b_glu[l])
    ops = _s5_operators(s5_a_re[l], s5_a_im[l], s5_log_dt[l], s5_b_re[l], s5_b_im[l], s5_c_re[l], s5_c_im[l],
                        s5_d[l])
    f1 = (ffn1_w1[l].astype(BF16), ffn1_w3[l].astype(BF16), ffn1_w2[l].astype(BF16))
    f2 = (ffn2_w1[l].astype(BF16), ffn2_w3[l].astype(BF16), ffn2_w2[l].astype(BF16))
    cos_p, sin_p = _rope_tables(jnp.arange(seq))
    cos_s, sin_s = _rope_tables(jnp.full((bs,), past_len))

    tm_p = 512
    xp = x_prompt.reshape(bp * seq, D_MODEL)
    xp = _ffn(xp, mod_p, 0, seq // tm_p, norm_ffn1[l], *f1, tm=tm_p)
    tm_mix = 512
    tq_p = 512
    q_lat_t, q_rope_t, ckv_p, kc_p, kct_p, krope_t_p, kr_p, u_p = _mixin(xp, mod_p, seq // tm_mix, seq, p, cos_p,
                                                                         sin_p, tm=tm_mix, tq=tq_p)
    attn_p = _prompt_attn(q_lat_t, q_rope_t, kc_p, kr_p, kct_p, p["w_uv_h"], p["norm_attn_out_col"], tq=tq_p)
    y_p, hre_p, him_p = _s5_prompt(u_p, ops)
    xp = _mixout(xp, attn_p.reshape(bp * seq, MLA_WIDTH), y_p.reshape(bp * seq, S5_WIDTH), mod_p, seq // tm_mix, p,
                 tm=tm_mix)
    y_prompt = _ffn(xp, mod_p, 6, seq // tm_p, norm_ffn2[l], *f2, norm_final, tm=tm_p).reshape(bp, seq, D_MODEL)

    xs = x_sample.reshape(bs, D_MODEL)
    xs = _ffn(xs, mod_s, 0, 1, norm_ffn1[l], *f1, tm=bs)
    q_lat_ts, q_rope_ts, ckv_s, _, _, krope_t_s, _, u_s = _mixin(xs, mod_s, 1, bs, p, cos_s, sin_s, tm=bs, tq=bs)
    krope_s = krope_t_s[0].T
    q_lat_s = q_lat_ts.reshape(KV_LORA, MLA_HEADS, bs).transpose(2, 1, 0)
    q_rope_s = q_rope_ts.reshape(QK_ROPE, MLA_HEADS, bs).transpose(2, 1, 0)
    o_lat = _sample_attn(page_table, q_lat_s, q_rope_s,
                         ckv_s.reshape(bs, 1, KV_LORA), krope_s.reshape(bs, 1, QK_ROPE), cache_ckv[l:l + 1],
                         cache_krope[l:l + 1].transpose(0, 1, 3, 2))
    attn_s = _sample_attn_out(o_lat.transpose(1, 0, 2), p["w_uv_h"], p["norm_attn_out"])
    y_s, hre_s, him_s = _s5_sample(u_s.reshape(bs, S5_WIDTH), state_s5_re[l], state_s5_im[l], ops)
    xs = _mixout(xs, attn_s, y_s, mod_s, 1, p, tm=bs)
    y_sample = _ffn(xs, mod_s, 6, 1, norm_ffn2[l], *f2, norm_final, tm=bs).reshape(bs, 1, D_MODEL)

    return (y_prompt, y_sample,
            ckv_p[None], krope_t_p.transpose(0, 2, 1)[None],
            ckv_s.reshape(1, bs, 1, KV_LORA), krope_s.reshape(1, bs, 1, QK_ROPE),
            hre_p[None], him_p[None], hre_s[None], him_s[None])
```

```python
import functools

import jax
import jax.numpy as jnp
from jax import lax
from jax.experimental import pallas as pl
from jax.experimental.pallas import tpu as pltpu

F32 = jnp.float32
BF16 = jnp.bfloat16

D_MODEL = 2048
D_FF = 5632
MLA_HEADS = 8
QK_NOPE = 128
QK_ROPE = 64
V_HEAD = 128
Q_LORA = 512
KV_LORA = 256
MLA_WIDTH = 1024
S5_WIDTH = 1024
S5_GROUP = 16
S5_GROUPS = 64
S5_STATE = 64
ADA_CHUNKS = 9
PAGE = 128
ROPE_THETA = 10000.0
SOFTMAX_SCALE = (QK_NOPE + QK_ROPE) ** -0.5
EPS = 1e-6

LANES = 128
S5_CHUNK = 16
GROUPS_PER_SLAB = LANES // S5_GROUP
N_SLABS = S5_WIDTH // LANES
SLAB_STATE = GROUPS_PER_SLAB * S5_STATE
CHUNK_COLS = S5_CHUNK * LANES
MXU_TILE = 2 * LANES
VMEM_LIMIT = 56 * 1024 * 1024


def _cparams(sem):
    return pltpu.CompilerParams(dimension_semantics=sem, vmem_limit_bytes=VMEM_LIMIT)


def _rms(x, g):
    return x * lax.rsqrt(jnp.mean(x * x, axis=-1, keepdims=True) + EPS) * g


def _rms_rows(x, g):
    return x * lax.rsqrt(jnp.mean(x * x, axis=0, keepdims=True) + EPS) * g


def _dot(a, b):
    return jnp.dot(a, b, preferred_element_type=F32)


def _dot_nt(a, b):
    return lax.dot_general(a, b, (((1,), (1,)), ((), ())), preferred_element_type=F32)


def _ada_kernel(c_ref, w_ref, b_ref, o_ref):
    c = c_ref[...]
    a = (c * jax.nn.sigmoid(c)).astype(BF16)
    o_ref[...] = _dot(a, w_ref[...].astype(BF16)) + b_ref[...]


def _ada(c, w_ada, b_ada):
    rows = c.shape[0]
    n = w_ada.shape[1]
    tn = 1024
    return pl.pallas_call(
        _ada_kernel,
        grid=(n // tn,),
        in_specs=[
            pl.BlockSpec((rows, D_MODEL), lambda j: (0, 0)),
            pl.BlockSpec((D_MODEL, tn), lambda j: (0, j)),
            pl.BlockSpec((1, tn), lambda j: (0, j)),
        ],
        out_specs=pl.BlockSpec((rows, tn), lambda j: (0, j)),
        out_shape=jax.ShapeDtypeStruct((rows, n), F32),
        compiler_params=_cparams(("arbitrary",)),
        name="ada_modulation",
    )(c, w_ada, b_ada.reshape(1, n))


def _ffn_kernel(x_ref, sh_ref, sc_ref, gt_ref, g_ref, w13_ref, w2_ref, w2_last_ref, *rest, final_norm):
    if final_norm:
        gf_ref, o_ref, h_ref, act_ref = rest
    else:
        o_ref, h_ref, act_ref = rest
    j = pl.program_id(1)
    last = pl.num_programs(1) - 1
    tf = act_ref.shape[1]

    def gate():
        ab = _dot(h_ref[...], w13_ref[0])
        a = ab[:, :tf]
        act_ref[...] = (a * jax.nn.sigmoid(a) * ab[:, tf:]).astype(BF16)

    def down():
        return _dot(act_ref[...], w2_ref[...])

    @pl.when(j == 0)
    def _():
        h = _rms(x_ref[...], g_ref[...]) * (1.0 + sc_ref[0]) + sh_ref[0]
        h_ref[...] = h.astype(BF16)
        gate()

    @pl.when(j == 1)
    def _():
        o_ref[...] = down()
        gate()

    @pl.when((j > 1) & (j < last))
    def _():
        o_ref[...] += down()
        gate()

    @pl.when(j == last)
    def _():
        prev = down()
        gate()
        y = x_ref[...] + 0.5 * gt_ref[0] * (o_ref[...] + prev + _dot(act_ref[...], w2_last_ref[...]))
        if final_norm:
            y = _rms(y, gf_ref[...])
        o_ref[...] = y


def _ffn_weights(w1, w3, w2, tf=512):
    n_ff = D_FF // tf
    w13 = jnp.concatenate([w1.reshape(D_MODEL, n_ff, tf), w3.reshape(D_MODEL, n_ff, tf)], axis=2)
    return w13.transpose(1, 0, 2).astype(BF16), w2.astype(BF16)


def _ffn(x, mod, chunk0, tiles_per_b, g, w13, w2, gf=None, *, tm):
    n = x.shape[0]
    r = mod.shape[1]
    n_ff = w13.shape[0]
    tf = w13.shape[2] // 2
    assert n_ff >= 3

    def mod_spec(k):
        return pl.BlockSpec((1, r, D_MODEL), lambda i, j: (i // tiles_per_b, 0, k))

    in_specs = [
        pl.BlockSpec((tm, D_MODEL), lambda i, j: (i, 0)),
        mod_spec(chunk0), mod_spec(chunk0 + 1), mod_spec(chunk0 + 2),
        pl.BlockSpec((1, D_MODEL), lambda i, j: (0, 0)),
        pl.BlockSpec((1, D_MODEL, 2 * tf), lambda i, j: (j, 0, 0)),
        pl.BlockSpec((tf, D_MODEL), lambda i, j: (jnp.maximum(j - 1, 0), 0)),
        pl.BlockSpec((tf, D_MODEL), lambda i, j: (n_ff - 1, 0)),
    ]
    args = [x, mod, mod, mod, g.reshape(1, D_MODEL), w13, w2, w2]
    if gf is not None:
        in_specs.append(pl.BlockSpec((1, D_MODEL), lambda i, j: (0, 0)))
        args.append(gf.reshape(1, D_MODEL))
    return pl.pallas_call(
        functools.partial(_ffn_kernel, final_norm=gf is not None),
        grid=(n // tm, n_ff),
        in_specs=in_specs,
        out_specs=pl.BlockSpec((tm, D_MODEL), lambda i, j: (i, 0)),
        out_shape=jax.ShapeDtypeStruct((n, D_MODEL), F32),
        scratch_shapes=[pltpu.VMEM((tm, D_MODEL), BF16), pltpu.VMEM((tm, tf), BF16)],
        compiler_params=_cparams(("parallel", "arbitrary")),
        name="macaron_ffn",
    )(*args)


def _rope_rows(x, cos, sin):
    half = QK_ROPE // 2
    x1 = x[:half]
    x2 = x[half:]
    return x1 * cos - x2 * sin, x1 * sin + x2 * cos


def _mixin_kernel(x_ref, sh_ref, sc_ref, g_ref, wq_ref, wkv_ref, wkr_ref, wu_ref, gq_ref, wqn_ref, wqr_ref,
                  wuk_ref, gkv_ref, cos_ref, sin_ref,
                  qlat_ref, qrope_ref, ckv_ref, kc_ref, kct_ref, krope_ref, kr_ref, u_ref, *, tq):
    half = QK_ROPE // 2
    n_q = x_ref.shape[0] // tq
    h = (_rms(x_ref[...], g_ref[...]) * (1.0 + sc_ref[0]) + sh_ref[0]).astype(BF16)
    cos = cos_ref[...]
    sin = sin_ref[...]

    u_ref[0] = _dot(h, wu_ref[...])

    ckv = _rms(_dot(h, wkv_ref[...]), gkv_ref[...])
    ckv_ref[0] = ckv
    kc_ref[0] = ckv.astype(BF16)
    kct_ref[0] = ckv.T.astype(BF16)

    k1, k2 = _rope_rows(_dot_nt(wkr_ref[...], h), cos, sin)
    kr_t = jnp.concatenate([k1, k2], axis=0)
    krope_ref[0] = kr_t
    kr_ref[0] = kr_t.T.astype(BF16)

    qn = _rms(_dot(h, wq_ref[...]), gq_ref[...]).astype(BF16)
    q_nope = (_dot(qn, wqn_ref[...]) * SOFTMAX_SCALE).astype(BF16)
    q_rope_t = _dot_nt(wqr_ref[...], qn) * SOFTMAX_SCALE
    for hd in range(MLA_HEADS):
        ql_t = _dot_nt(wuk_ref[hd], q_nope[:, hd * QK_NOPE:(hd + 1) * QK_NOPE]).astype(BF16)
        r1, r2 = _rope_rows(q_rope_t[hd * QK_ROPE:(hd + 1) * QK_ROPE], cos, sin)
        r1 = r1.astype(BF16)
        r2 = r2.astype(BF16)
        for qq in range(n_q):
            src = slice(qq * tq, (qq + 1) * tq)
            dst = slice(hd * tq, (hd + 1) * tq)
            qlat_ref[0, qq, :, dst] = ql_t[:, src]
            qrope_ref[0, qq, :half, dst] = r1[:, src]
            qrope_ref[0, qq, half:, dst] = r2[:, src]


def _mixin(x, mod, tiles_per_b, t_len, p, cos_t, sin_t, *, tm, tq):
    n = x.shape[0]
    nb = n // t_len
    r = mod.shape[1]
    half = QK_ROPE // 2
    full = lambda shape: pl.BlockSpec(shape, lambda i: (0,) * len(shape))
    rows = lambda width: pl.BlockSpec((1, tm, width), lambda i: (i // tiles_per_b, i % tiles_per_b, 0))
    cols = lambda height: pl.BlockSpec((1, height, tm), lambda i: (i // tiles_per_b, 0, i % tiles_per_b))
    head_cols = lambda height: pl.BlockSpec((1, tm // tq, height, MLA_HEADS * tq),
                                            lambda i: (i // tiles_per_b, i % tiles_per_b, 0, 0))
    in_specs = [
        pl.BlockSpec((tm, D_MODEL), lambda i: (i, 0)),
        pl.BlockSpec((1, r, D_MODEL), lambda i: (i // tiles_per_b, 0, 3)),
        pl.BlockSpec((1, r, D_MODEL), lambda i: (i // tiles_per_b, 0, 4)),
        full((1, D_MODEL)),
        full((D_MODEL, Q_LORA)), full((D_MODEL, KV_LORA)), full((QK_ROPE, D_MODEL)), full((D_MODEL, S5_WIDTH)),
        full((1, Q_LORA)), full((Q_LORA, MLA_HEADS * QK_NOPE)), full((MLA_HEADS * QK_ROPE, Q_LORA)),
        full((MLA_HEADS, KV_LORA, QK_NOPE)), full((1, KV_LORA)),
        pl.BlockSpec((half, tm), lambda i: (0, i % tiles_per_b)),
        pl.BlockSpec((half, tm), lambda i: (0, i % tiles_per_b)),
    ]
    out_specs = [head_cols(KV_LORA), head_cols(QK_ROPE), rows(KV_LORA), rows(KV_LORA), cols(KV_LORA),
                 cols(QK_ROPE), rows(QK_ROPE), rows(S5_WIDTH)]
    out_shape = [
        jax.ShapeDtypeStruct((nb, t_len // tq, KV_LORA, MLA_HEADS * tq), BF16),
        jax.ShapeDtypeStruct((nb, t_len // tq, QK_ROPE, MLA_HEADS * tq), BF16),
        jax.ShapeDtypeStruct((nb, t_len, KV_LORA), F32),
        jax.ShapeDtypeStruct((nb, t_len, KV_LORA), BF16),
        jax.ShapeDtypeStruct((nb, KV_LORA, t_len), BF16),
        jax.ShapeDtypeStruct((nb, QK_ROPE, t_len), F32),
        jax.ShapeDtypeStruct((nb, t_len, QK_ROPE), BF16),
        jax.ShapeDtypeStruct((nb, t_len, S5_WIDTH), F32),
    ]
    return pl.pallas_call(
        functools.partial(_mixin_kernel, tq=tq),
        grid=(n // tm,),
        in_specs=in_specs,
        out_specs=out_specs,
        out_shape=out_shape,
        compiler_params=_cparams(("parallel",)),
        name="mixer_in",
    )(x, mod, mod, p["norm_mix"], p["w_q"], p["w_kv"], p["w_kr_t"], p["w_u"], p["norm_q"], p["w_uq_nope"],
      p["w_uq_rope_t"], p["w_uk_h"], p["norm_kv"], cos_t, sin_t)


def _prompt_attn_kernel(ql_ref, qr_ref, kc_ref, kr_ref, kct_ref, wuv_ref, g_ref, o_ref, *scratch, tq, tk):
    qi = pl.program_id(1)
    n_half = len(scratch) // 3
    m_ref, l_ref, acc_ref = scratch[:n_half], scratch[n_half:2 * n_half], scratch[2 * n_half:]
    width = m_ref[0].shape[1]
    heads_per_half = width // tq
    for hf in range(n_half):
        m_ref[hf][...] = jnp.full_like(m_ref[hf], -jnp.inf)
        l_ref[hf][...] = jnp.zeros_like(l_ref[hf])
        acc_ref[hf][...] = jnp.zeros_like(acc_ref[hf])

    def step(ki, diag):
        start = pl.multiple_of(ki * tk, tk)
        kc = kc_ref[0, pl.ds(start, tk), :]
        kr = kr_ref[0, pl.ds(start, tk), :]
        kct = kct_ref[0, :, pl.ds(start, tk)]
        masked = diag is not None
        if masked:
            k_local = lax.broadcasted_iota(jnp.int32, (tk, width), 0) + diag * tk
            t_local = lax.broadcasted_iota(jnp.int32, (tk, width), 1) & (tq - 1)
            keep = k_local <= t_local
        for hf in range(n_half):
            lanes = slice(hf * width, (hf + 1) * width)
            s = _dot(kc, ql_ref[0, 0, :, lanes]) + _dot(kr, qr_ref[0, 0, :, lanes])
            if masked:
                s = jnp.where(keep, s, -jnp.inf)
            m_old = m_ref[hf][...]
            m_new = jnp.maximum(m_old, jnp.max(s, axis=0, keepdims=True))
            alpha = jnp.exp(m_old - m_new)
            pexp = jnp.exp(s - m_new)
            l_ref[hf][...] = alpha * l_ref[hf][...] + jnp.sum(pexp, axis=0, keepdims=True)
            acc_ref[hf][...] = alpha * acc_ref[hf][...] + _dot(kct, pexp.astype(BF16))
            m_ref[hf][...] = m_new

    def body(ki, carry):
        step(ki, None)
        return carry

    blocks_per_tile = tq // tk
    lax.fori_loop(0, qi * blocks_per_tile, body, 0)
    for d in range(blocks_per_tile):
        step(qi * blocks_per_tile + d, d)
    parts = []
    for hd in range(MLA_HEADS):
        hf = hd // heads_per_half
        lanes = slice((hd % heads_per_half) * tq, (hd % heads_per_half + 1) * tq)
        o_t = acc_ref[hf][:, lanes] / l_ref[hf][:, lanes]
        parts.append(_dot(wuv_ref[hd], o_t.astype(BF16)))
    attn_t = _rms_rows(jnp.concatenate(parts, axis=0), g_ref[...])
    o_ref[0] = attn_t.T.astype(BF16)


def _prompt_attn(q_lat_t, q_rope_t, kc, kr, kc_t, w_uv_h, g_col, *, tq, tk=256, n_half=1):
    nb, n_q, _, _ = q_lat_t.shape
    t_len = n_q * tq
    width = MLA_HEADS * tq // n_half
    assert tq % tk == 0
    return pl.pallas_call(
        functools.partial(_prompt_attn_kernel, tq=tq, tk=tk),
        grid=(nb, n_q),
        in_specs=[
            pl.BlockSpec((1, 1, KV_LORA, MLA_HEADS * tq), lambda b, i: (b, i, 0, 0)),
            pl.BlockSpec((1, 1, QK_ROPE, MLA_HEADS * tq), lambda b, i: (b, i, 0, 0)),
            pl.BlockSpec((1, t_len, KV_LORA), lambda b, i: (b, 0, 0)),
            pl.BlockSpec((1, t_len, QK_ROPE), lambda b, i: (b, 0, 0)),
            pl.BlockSpec((1, KV_LORA, t_len), lambda b, i: (b, 0, 0)),
            pl.BlockSpec((MLA_HEADS, V_HEAD, KV_LORA), lambda b, i: (0, 0, 0)),
            pl.BlockSpec((MLA_WIDTH, 1), lambda b, i: (0, 0)),
        ],
        out_specs=pl.BlockSpec((1, tq, MLA_WIDTH), lambda b, i: (b, i, 0)),
        out_shape=jax.ShapeDtypeStruct((nb, t_len, MLA_WIDTH), BF16),
        scratch_shapes=([pltpu.VMEM((1, width), F32)] * (2 * n_half) + [pltpu.VMEM((KV_LORA, width), F32)] * n_half),
        compiler_params=_cparams(("parallel", "parallel")),
        name="prompt_attention",
    )(q_lat_t, q_rope_t, kc, kr, kc_t, w_uv_h, g_col)


def _sample_attn_kernel(pt_ref, ql_ref, qr_ref, kcn_ref, krn_ref, ckv_hbm, kr_hbm, o_ref, kbuf, rbuf, sem_k, sem_r,
                        *, pages, streams):
    nb, n_pages = pt_ref.shape
    n_chunks = n_pages // pages
    n_groups = nb // streams
    total = n_groups * n_chunks
    n_slots = kbuf.shape[0]
    ahead = n_slots - 1

    def chunk_copies(g, slot):
        grp = g // n_chunks
        c = g % n_chunks
        copies = []
        for st in range(streams):
            b = grp * streams + st
            for i in range(pages):
                page = pt_ref[b, c * pages + i]
                copies.append(pltpu.make_async_copy(
                    ckv_hbm.at[0, page], kbuf.at[slot, st, pl.ds(i * PAGE, PAGE), :], sem_k.at[slot, st]))
                copies.append(pltpu.make_async_copy(
                    kr_hbm.at[0, page], rbuf.at[slot, st, i], sem_r.at[slot, st]))
        return copies

    grp = pl.program_id(0)

    @pl.when(grp == 0)
    def _():
        for g0 in range(ahead):
            for cp in chunk_copies(g0, g0):
                cp.start()

    qs = [(ql_ref[st], qr_ref[st]) for st in range(streams)]

    def chunk_body(c, carry):
        g = grp * n_chunks + c
        slot = g % n_slots

        @pl.when(g + ahead < total)
        def _():
            for cp in chunk_copies(g + ahead, (g + ahead) % n_slots):
                cp.start()

        for cp in chunk_copies(g, slot):
            cp.wait()
        out = []
        for st in range(streams):
            m_old, l_old, acc_old = carry[st]
            ql, qr = qs[st]
            kc = kbuf[slot, st].astype(BF16)
            kr_t = jnp.concatenate([rbuf[slot, st, i].astype(BF16) for i in range(pages)], axis=1)
            s = _dot_nt(ql, kc) + _dot(qr, kr_t)
            m_new = jnp.maximum(m_old, jnp.max(s, axis=1, keepdims=True))
            alpha = jnp.exp(m_old - m_new)
            pexp = jnp.exp(s - m_new)
            l_new = alpha * l_old + jnp.sum(pexp, axis=1, keepdims=True)
            acc_new = alpha * acc_old + _dot(pexp.astype(BF16), kc)
            out.append((m_new, l_new, acc_new))
        return tuple(out)

    init = tuple((jnp.full((MLA_HEADS, 1), -jnp.inf, F32), jnp.zeros((MLA_HEADS, 1), F32),
                  jnp.zeros((MLA_HEADS, KV_LORA), F32)) for _ in range(streams))
    final = lax.fori_loop(0, n_chunks, chunk_body, init)
    for st in range(streams):
        m_old, l_old, acc_old = final[st]
        ql, qr = qs[st]
        kcn = kcn_ref[st].astype(BF16).astype(F32)
        krn = krn_ref[st].astype(BF16).astype(F32)
        s_n = (jnp.sum(ql.astype(F32) * kcn, axis=1, keepdims=True)
               + jnp.sum(qr.astype(F32) * krn, axis=1, keepdims=True))
        m_f = jnp.maximum(m_old, s_n)
        a_f = jnp.exp(m_old - m_f)
        p_n = jnp.exp(s_n - m_f)
        l_f = a_f * l_old + p_n
        acc_f = a_f * acc_old + p_n.astype(BF16).astype(F32) * kcn
        o_ref[st] = acc_f / l_f


def _sample_attn(page_table, q_lat, q_rope, ckv_new, krope_new, cache_ckv, cache_krope_t, *, pages=16, streams=2,
                 slots=3):
    nb, n_pages = page_table.shape
    assert n_pages % pages == 0 and nb % streams == 0 and (nb // streams) * (n_pages // pages) >= slots
    per_group = lambda rows, width: pl.BlockSpec((streams, rows, width), lambda g, pt: (g, 0, 0))
    grid_spec = pltpu.PrefetchScalarGridSpec(
        num_scalar_prefetch=1,
        grid=(nb // streams,),
        in_specs=[
            per_group(MLA_HEADS, KV_LORA), per_group(MLA_HEADS, QK_ROPE),
            per_group(1, KV_LORA), per_group(1, QK_ROPE),
            pl.BlockSpec(memory_space=pl.ANY), pl.BlockSpec(memory_space=pl.ANY),
        ],
        out_specs=per_group(MLA_HEADS, KV_LORA),
        scratch_shapes=[
            pltpu.VMEM((slots, streams, pages * PAGE, KV_LORA), F32),
            pltpu.VMEM((slots, streams, pages, QK_ROPE, PAGE), F32),
            pltpu.SemaphoreType.DMA((slots, streams)),
            pltpu.SemaphoreType.DMA((slots, streams)),
        ],
    )
    return pl.pallas_call(
        functools.partial(_sample_attn_kernel, pages=pages, streams=streams),
        grid_spec=grid_spec,
        out_shape=jax.ShapeDtypeStruct((nb, MLA_HEADS, KV_LORA), F32),
        compiler_params=_cparams(("arbitrary",)),
        name="sample_paged_attention",
    )(page_table, q_lat, q_rope, ckv_new, krope_new, cache_ckv, cache_krope_t)


def _sample_attn_out_kernel(o_ref, wuv_ref, g_ref, out_ref):
    parts = [_dot_nt(o_ref[hd].astype(BF16), wuv_ref[hd]) for hd in range(MLA_HEADS)]
    out_ref[...] = _rms(jnp.concatenate(parts, axis=1), g_ref[...]).astype(BF16)


def _sample_attn_out(o_lat_h, w_uv_h, g):
    nb = o_lat_h.shape[1]
    return pl.pallas_call(
        _sample_attn_out_kernel,
        out_shape=jax.ShapeDtypeStruct((nb, MLA_WIDTH), BF16),
        name="sample_attention_out",
    )(o_lat_h, w_uv_h, g)


def _same_group(shape, row_shift, col_shift):
    rows = lax.broadcasted_iota(jnp.int32, shape, 0)
    cols = lax.broadcasted_iota(jnp.int32, shape, 1)
    mask = GROUPS_PER_SLAB - 1
    return ((rows >> row_shift) & mask) == ((cols >> col_shift) & mask)


def _s5_prompt_kernel(u_ref, wr_ref, we_ref, wf_ref, tr_ref, te_ref, tf_ref, are_ref, aim_ref, d_ref,
                      y_ref, hre_ref, him_ref, r_ref, e_ref, f_ref, uc_ref, s_ref, hp_ref, yc_ref):
    n_chunks = uc_ref.shape[0]
    n_blk = CHUNK_COLS // MXU_TILE
    ch_bits = S5_GROUP.bit_length() - 1
    st_bits = S5_STATE.bit_length() - 1

    @pl.when(pl.program_id(1) == 0)
    def _():
        zero = jnp.zeros((), F32)
        r_ref[...] = jnp.where(_same_group(r_ref.shape, ch_bits, ch_bits),
                               _dot(wr_ref[0], tr_ref[...]), zero).astype(BF16)
        e_ref[...] = jnp.where(_same_group(e_ref.shape, ch_bits, st_bits),
                               _dot(we_ref[0], te_ref[...]), zero).astype(BF16)
        f_ref[...] = jnp.where(_same_group(f_ref.shape, st_bits, ch_bits),
                               _dot(tf_ref[...], wf_ref[0]), zero).astype(BF16)

    for s in range(S5_CHUNK):
        uc_ref[:, s * LANES:(s + 1) * LANES] = u_ref[0, pl.ds(s, n_chunks, stride=S5_CHUNK), :]
    uc = uc_ref[...]
    ub = uc.astype(BF16)

    s_ref[...] = _dot(ub, e_ref[...])

    a_re = are_ref[0]
    a_im = aim_ref[0]

    def scan(k, carry):
        h_re, h_im = carry
        hp_ref[pl.ds(k, 1), :] = jnp.concatenate([h_re, h_im], axis=1)
        row = s_ref[pl.ds(k, 1), :]
        n_re = a_re * h_re - a_im * h_im + row[:, :SLAB_STATE]
        n_im = a_re * h_im + a_im * h_re + row[:, SLAB_STATE:]
        return n_re, n_im

    zero_state = jnp.zeros((1, SLAB_STATE), F32)
    h_re, h_im = lax.fori_loop(0, n_chunks, scan, (zero_state, zero_state), unroll=4)
    hre_ref[0, 0] = h_re
    him_ref[0, 0] = h_im

    yc_ref[...] = _dot(hp_ref[...].astype(BF16), f_ref[...]) + uc * d_ref[0]
    for tb in range(n_blk):
        yc_ref[:, tb * MXU_TILE:(tb + 1) * MXU_TILE] += _dot(ub[:, :(tb + 1) * MXU_TILE],
                                                             r_ref[(n_blk - 1 - tb) * MXU_TILE:, :])
    y = jax.nn.gelu(yc_ref[...])
    for t in range(S5_CHUNK):
        y_ref[0, pl.ds(t, n_chunks, stride=S5_CHUNK), :] = y[:, t * LANES:(t + 1) * LANES]


def _s5_prompt(u, ops):
    nb, t_len, _ = u.shape
    n_chunks = t_len // S5_CHUNK
    slab = lambda shape: pl.BlockSpec((1,) + shape, lambda j, b: (j,) + (0,) * len(shape))
    full = lambda shape: pl.BlockSpec(shape, lambda j, b: (0,) * len(shape))
    y, hre, him = pl.pallas_call(
        _s5_prompt_kernel,
        grid=(N_SLABS, nb),
        in_specs=[
            pl.BlockSpec((1, t_len, LANES), lambda j, b: (b, 0, j)),
            slab((CHUNK_COLS, 2 * S5_GROUP)), slab((CHUNK_COLS, 2 * S5_STATE)), slab((2 * S5_STATE, CHUNK_COLS)),
            full((2 * S5_GROUP, MXU_TILE)), full((2 * S5_STATE, 2 * SLAB_STATE)), full((2 * SLAB_STATE, 2 * S5_STATE)),
            slab((1, SLAB_STATE)), slab((1, SLAB_STATE)), slab((1, CHUNK_COLS)),
        ],
        out_specs=[
            pl.BlockSpec((1, t_len, LANES), lambda j, b: (b, 0, j)),
            pl.BlockSpec((1, 1, 1, SLAB_STATE), lambda j, b: (j, b, 0, 0)),
            pl.BlockSpec((1, 1, 1, SLAB_STATE), lambda j, b: (j, b, 0, 0)),
        ],
        out_shape=[
            jax.ShapeDtypeStruct((nb, t_len, S5_WIDTH), F32),
            jax.ShapeDtypeStruct((N_SLABS, nb, 1, SLAB_STATE), F32),
            jax.ShapeDtypeStruct((N_SLABS, nb, 1, SLAB_STATE), F32),
        ],
        scratch_shapes=[
            pltpu.VMEM((CHUNK_COLS, MXU_TILE), BF16),
            pltpu.VMEM((CHUNK_COLS, 2 * SLAB_STATE), BF16),
            pltpu.VMEM((2 * SLAB_STATE, CHUNK_COLS), BF16),
            pltpu.VMEM((n_chunks, CHUNK_COLS), F32),
            pltpu.VMEM((n_chunks, 2 * SLAB_STATE), F32),
            pltpu.VMEM((n_chunks, 2 * SLAB_STATE), F32),
            pltpu.VMEM((n_chunks, CHUNK_COLS), F32),
        ],
        compiler_params=_cparams(("arbitrary", "arbitrary")),
        name="s5_prompt",
    )(u, ops["wr"], ops["we"], ops["wf"], ops["tile_r"], ops["tile_e"], ops["tile_f"], ops["a_re"], ops["a_im"],
      ops["d_chunk"])

    def to_state(h):
        return h.reshape(N_SLABS, nb, GROUPS_PER_SLAB, S5_STATE).transpose(1, 0, 2, 3).reshape(nb, S5_GROUPS, S5_STATE)

    return y, to_state(hre), to_state(him)


def _split_bf16(x):
    hi = x.astype(BF16)
    return hi, (x - hi.astype(F32)).astype(BF16)


def _s5_sample_kernel(u_ref, h0re_ref, h0im_ref, bd_hi_ref, bd_lo_ref, cd_ref, lre_ref, lim_ref, d_ref,
                      y_ref, hre_ref, him_ref):
    u = u_ref[...]
    u_hi, u_lo = _split_bf16(u)
    bu = _dot(u_hi, bd_hi_ref[0]) + (_dot(u_hi, bd_lo_ref[0]) + _dot(u_lo, bd_hi_ref[0]))
    l_re = lre_ref[0]
    l_im = lim_ref[0]
    h0_re = h0re_ref[...]
    h0_im = h0im_ref[...]
    h_re = l_re * h0_re - l_im * h0_im + bu[:, :SLAB_STATE]
    h_im = l_re * h0_im + l_im * h0_re + bu[:, SLAB_STATE:]
    hre_ref[...] = h_re
    him_ref[...] = h_im
    h = jnp.concatenate([h_re, h_im], axis=1).astype(BF16)
    y_ref[...] = jax.nn.gelu(_dot(h, cd_ref[0]) + u * d_ref[0])


def _s5_sample(u, h0_re, h0_im, ops):
    nb = u.shape[0]
    n_state = S5_GROUPS * S5_STATE
    slab = lambda shape: pl.BlockSpec((1,) + shape, lambda j: (j,) + (0,) * len(shape))
    col = lambda width: pl.BlockSpec((nb, width), lambda j: (0, j))
    y, hre, him = pl.pallas_call(
        _s5_sample_kernel,
        grid=(N_SLABS,),
        in_specs=[
            col(LANES), col(SLAB_STATE), col(SLAB_STATE),
            slab((LANES, 2 * SLAB_STATE)), slab((LANES, 2 * SLAB_STATE)), slab((2 * SLAB_STATE, LANES)),
            slab((1, SLAB_STATE)), slab((1, SLAB_STATE)), slab((1, LANES)),
        ],
        out_specs=[col(LANES), col(SLAB_STATE), col(SLAB_STATE)],
        out_shape=[
            jax.ShapeDtypeStruct((nb, S5_WIDTH), F32),
            jax.ShapeDtypeStruct((nb, n_state), F32),
            jax.ShapeDtypeStruct((nb, n_state), F32),
        ],
        compiler_params=_cparams(("parallel",)),
        name="s5_sample",
    )(u, h0_re.reshape(nb, n_state), h0_im.reshape(nb, n_state), ops["bd_hi"], ops["bd_lo"], ops["cd"],
      ops["l_re"], ops["l_im"], ops["d_slab"])
    return y, hre.reshape(nb, S5_GROUPS, S5_STATE), him.reshape(nb, S5_GROUPS, S5_STATE)


def _mixout_kernel(x_ref, attn_ref, y_ref, gt_ref, wglu_ref, bglu_ref, gs_ref, woa_ref, woy_ref, o_ref):
    y = y_ref[...]
    z = _dot(y.astype(BF16), wglu_ref[...]) + bglu_ref[...]
    yn = _rms(y * jax.nn.sigmoid(z), gs_ref[...]).astype(BF16)
    mix = _dot(attn_ref[...], woa_ref[...]) + _dot(yn, woy_ref[...])
    o_ref[...] = x_ref[...] + gt_ref[0] * mix


def _mixout(x, attn, y, mod, tiles_per_b, p, *, tm):
    n = x.shape[0]
    r = mod.shape[1]
    full = lambda shape: pl.BlockSpec(shape, lambda i: (0,) * len(shape))
    return pl.pallas_call(
        _mixout_kernel,
        grid=(n // tm,),
        in_specs=[
            pl.BlockSpec((tm, D_MODEL), lambda i: (i, 0)),
            pl.BlockSpec((tm, MLA_WIDTH), lambda i: (i, 0)),
            pl.BlockSpec((tm, S5_WIDTH), lambda i: (i, 0)),
            pl.BlockSpec((1, r, D_MODEL), lambda i: (i // tiles_per_b, 0, 5)),
            full((S5_WIDTH, S5_WIDTH)), full((1, S5_WIDTH)), full((1, S5_WIDTH)),
            full((MLA_WIDTH, D_MODEL)), full((S5_WIDTH, D_MODEL)),
        ],
        out_specs=pl.BlockSpec((tm, D_MODEL), lambda i: (i, 0)),
        out_shape=jax.ShapeDtypeStruct((n, D_MODEL), F32),
        compiler_params=_cparams(("parallel",)),
        name="mixer_out",
    )(x, attn, y, mod, p["w_glu"], p["b_glu"], p["norm_ssm_out"], p["w_out_attn"], p["w_out_ssm"])


def _rope_tables(pos):
    half = QK_ROPE // 2
    inv_freq = ROPE_THETA ** (-jnp.arange(half, dtype=F32) / half)
    ang = pos.astype(F32)[:, None] * inv_freq[None, :]
    return jnp.cos(ang).T, jnp.sin(ang).T


def _group_diag(x, g_axis, new_axis):
    x = jnp.expand_dims(x, new_axis)
    shape = [1] * x.ndim
    shape[g_axis if g_axis < new_axis else g_axis + 1] = GROUPS_PER_SLAB
    shape[new_axis] = GROUPS_PER_SLAB
    return x * jnp.eye(GROUPS_PER_SLAB, dtype=x.dtype).reshape(shape)


def _by_slab(x, g_axis):
    return x.reshape(x.shape[:g_axis] + (N_SLABS, GROUPS_PER_SLAB) + x.shape[g_axis + 1:])


def _replicate_over_groups(n_outer, n_inner):
    eo = jnp.eye(n_outer, dtype=F32)[:, None, :, None, None]
    ei = jnp.eye(n_inner, dtype=F32)[None, :, None, None, :]
    ones = jnp.ones((1, 1, 1, GROUPS_PER_SLAB, 1), F32)
    return (eo * ei * ones).reshape(n_outer * n_inner, n_outer * GROUPS_PER_SLAB * n_inner)


def _s5_operators(a_re, a_im, log_dt, b_re, b_im, c_re, c_im, d_skip):
    hp = lax.Precision.HIGHEST
    a_re, a_im, b_re, b_im, c_re, c_im = (v.astype(F32) for v in (a_re, a_im, b_re, b_im, c_re, c_im))
    dt = jnp.exp(log_dt.astype(F32))[:, None]
    z_re = a_re * dt
    z_im = a_im * dt

    def lam_pow(n):
        mag = jnp.exp(z_re[None] * n[:, None, None])
        ang = z_im[None] * n[:, None, None]
        return mag * jnp.cos(ang), mag * jnp.sin(ang)

    steps = jnp.arange(S5_CHUNK + 1, dtype=F32)
    pw_re, pw_im = lam_pow(steps)
    lb_re, lb_im = pw_re[1], pw_im[1]
    den = a_re * a_re + a_im * a_im
    q_re = ((lb_re - 1.0) * a_re + lb_im * a_im) / den
    q_im = (lb_im * a_re - (lb_re - 1.0) * a_im) / den
    bb_re = q_re[:, :, None] * b_re - q_im[:, :, None] * b_im
    bb_im = q_re[:, :, None] * b_im + q_im[:, :, None] * b_re

    cp_re = c_re[None] * pw_re[:S5_CHUNK, :, None, :] - c_im[None] * pw_im[:S5_CHUNK, :, None, :]
    cp_im = c_re[None] * pw_im[:S5_CHUNK, :, None, :] + c_im[None] * pw_re[:S5_CHUNK, :, None, :]
    m = jnp.einsum("ngcq,gqd->ngcd", jnp.concatenate([cp_re, -cp_im], axis=3),
                   jnp.concatenate([bb_re, bb_im], axis=1), precision=hp)
    m_ext = jnp.concatenate([m, jnp.zeros_like(m[:1])], axis=0)
    n_blk = S5_CHUNK // 2
    d_i = (n_blk - 1 - jnp.arange(n_blk))[:, None, None]
    s_i = jnp.arange(2)[None, :, None]
    t_i = jnp.arange(2)[None, None, :]
    lag = 2 * d_i + t_i - s_i
    mg = _by_slab(m_ext[lag], 3)
    wr = mg.transpose(3, 0, 1, 4, 6, 2, 5).reshape(N_SLABS, CHUNK_COLS, 2 * S5_GROUP)

    pr_re, pr_im = lam_pow(S5_CHUNK - 1 - steps[:S5_CHUNK])
    w_re = pr_re[:, :, :, None] * bb_re[None] - pr_im[:, :, :, None] * bb_im[None]
    w_im = pr_re[:, :, :, None] * bb_im[None] + pr_im[:, :, :, None] * bb_re[None]

    def e_half(v):
        return _by_slab(v, 1).transpose(1, 0, 2, 4, 3).reshape(N_SLABS, CHUNK_COLS, S5_STATE)

    we = jnp.concatenate([e_half(w_re), e_half(w_im)], axis=2)

    g_re = c_re[None] * pw_re[1:, :, None, :] - c_im[None] * pw_im[1:, :, None, :]
    g_im = c_re[None] * pw_im[1:, :, None, :] + c_im[None] * pw_re[1:, :, None, :]

    def f_half(v):
        return _by_slab(v, 1).transpose(1, 4, 0, 2, 3).reshape(N_SLABS, S5_STATE, CHUNK_COLS)

    wf = jnp.concatenate([f_half(g_re), f_half(-g_im)], axis=1)

    def slab_vec(v):
        return v.reshape(N_SLABS, 1, SLAB_STATE)

    d_slab = d_skip.astype(F32).reshape(N_SLABS, 1, LANES)
    tile_e = _replicate_over_groups(2, S5_STATE)

    def bd_half(v):
        return _group_diag(_by_slab(v, 0).transpose(0, 1, 3, 2), 1, 3).reshape(N_SLABS, LANES, SLAB_STATE)

    def cd_half(v):
        return _group_diag(_by_slab(v, 0).transpose(0, 1, 3, 2), 1, 3).reshape(N_SLABS, SLAB_STATE, LANES)

    bd = jnp.concatenate([bd_half(bb_re), bd_half(bb_im)], axis=2)
    cd = jnp.concatenate([cd_half(c_re), cd_half(-c_im)], axis=1)
    bd_hi = bd.astype(BF16)
    bd_lo = (bd - bd_hi.astype(F32)).astype(BF16)
    return {
        "wr": wr.astype(BF16), "we": we.astype(BF16), "wf": wf.astype(BF16),
        "tile_r": _replicate_over_groups(2, S5_GROUP).astype(BF16),
        "tile_e": tile_e.astype(BF16), "tile_f": tile_e.T.astype(BF16),
        "a_re": slab_vec(pw_re[S5_CHUNK]), "a_im": slab_vec(pw_im[S5_CHUNK]),
        "d_chunk": jnp.tile(d_slab, (1, 1, S5_CHUNK)), "d_slab": d_slab,
        "bd_hi": bd_hi, "bd_lo": bd_lo, "cd": cd.astype(BF16),
        "l_re": slab_vec(lb_re), "l_im": slab_vec(lb_im),
    }


def _layer_params(w_in, w_uq, w_uk, w_uv, w_glu, w_out, norm_mix, norm_q, norm_kv, norm_attn_out, norm_ssm_out,
                  b_glu):
    c0, c1, c2 = Q_LORA, Q_LORA + KV_LORA, Q_LORA + KV_LORA + QK_ROPE
    w_uq_h = w_uq.reshape(Q_LORA, MLA_HEADS, QK_NOPE + QK_ROPE)
    return {
        "w_q": w_in[:, :c0].astype(BF16),
        "w_kv": w_in[:, c0:c1].astype(BF16),
        "w_kr_t": w_in[:, c1:c2].T.astype(BF16),
        "w_u": w_in[:, c2:].astype(BF16),
        "w_uq_nope": w_uq_h[:, :, :QK_NOPE].reshape(Q_LORA, MLA_HEADS * QK_NOPE).astype(BF16),
        "w_uq_rope_t": w_uq_h[:, :, QK_NOPE:].reshape(Q_LORA, MLA_HEADS * QK_ROPE).T.astype(BF16),
        "w_uk_h": w_uk.transpose(1, 0, 2).astype(BF16),
        "w_uv_h": w_uv.transpose(1, 2, 0).astype(BF16),
        "w_glu": w_glu.astype(BF16),
        "w_out_attn": w_out[:MLA_WIDTH].astype(BF16),
        "w_out_ssm": w_out[MLA_WIDTH:].astype(BF16),
        "norm_mix": norm_mix.reshape(1, D_MODEL),
        "norm_q": norm_q.reshape(1, Q_LORA),
        "norm_kv": norm_kv.reshape(1, KV_LORA),
        "norm_attn_out": norm_attn_out.reshape(1, MLA_WIDTH),
        "norm_attn_out_col": norm_attn_out.reshape(MLA_WIDTH, 1),
        "norm_ssm_out": norm_ssm_out.reshape(1, S5_WIDTH),
        "b_glu": b_glu.reshape(1, S5_WIDTH),
    }


def kernel(x_prompt, x_sample, c_prompt, c_sample, cache_ckv, cache_krope, state_s5_re, state_s5_im, page_table, w_ada, b_ada, norm_ffn1, ffn1_w1, ffn1_w3, ffn1_w2, norm_mix, w_in, norm_q, w_uq, norm_kv, w_uk, w_uv, s5_a_re, s5_a_im, s5_log_dt, s5_b_re, s5_b_im, s5_c_re, s5_c_im, s5_d, w_glu, b_glu, norm_attn_out, norm_ssm_out, w_out, norm_ffn2, ffn2_w1, ffn2_w3, ffn2_w2, norm_final):
    bp, seq, _ = x_prompt.shape
    bs = x_sample.shape[0]
    depth = w_ada.shape[0]
    assert depth == 1 and x_sample.shape[1] == 1
    n_pages = page_table.shape[1]
    past_len = n_pages * PAGE
    l = 0

    pad = (-(bs + bp)) % 8
    c_all = jnp.concatenate([c_sample, c_prompt, jnp.zeros((pad, D_MODEL), F32)], axis=0)
    mod = _ada(c_all, w_ada[l], b_ada[l])
    mod_s = mod[:bs].reshape(1, bs, ADA_CHUNKS * D_MODEL)
    mod_p = mod[bs:bs + bp].reshape(bp, 1, ADA_CHUNKS * D_MODEL)

    p = _layer_params(w_in[l], w_uq[l], w_uk[l], w_uv[l], w_glu[l], w_out[l], norm_mix[l], norm_q[l], norm_kv[l],
                      norm_attn_out[l], norm_ssm_out[l], b_glu[l])
    ops = _s5_operators(s5_a_re[l], s5_a_im[l], s5_log_dt[l], s5_b_re[l], s5_b_im[l], s5_c_re[l], s5_c_im[l],
                        s5_d[l])
    f1 = _ffn_weights(ffn1_w1[l], ffn1_w3[l], ffn1_w2[l])
    f2 = _ffn_weights(ffn2_w1[l], ffn2_w3[l], ffn2_w2[l])
    cos_p, sin_p = _rope_tables(jnp.arange(seq))
    cos_s, sin_s = _rope_tables(jnp.full((bs,), past_len))

    tm_p = 512
    xp = x_prompt.reshape(bp * seq, D_MODEL)
    xp = _ffn(xp, mod_p, 0, seq // tm_p, norm_ffn1[l], *f1, tm=tm_p)
    tm_mix = 512
    tq_p = 512
    q_lat_t, q_rope_t, ckv_p, kc_p, kct_p, krope_t_p, kr_p, u_p = _mixin(xp, mod_p, seq // tm_mix, seq, p, cos_p,
                                                                         sin_p, tm=tm_mix, tq=tq_p)
    attn_p = _prompt_attn(q_lat_t, q_rope_t, kc_p, kr_p, kct_p, p["w_uv_h"], p["norm_attn_out_col"], tq=tq_p)
    y_p, hre_p, him_p = _s5_prompt(u_p, ops)
    xp = _mixout(xp, attn_p.reshape(bp * seq, MLA_WIDTH), y_p.reshape(bp * seq, S5_WIDTH), mod_p, seq // tm_mix, p,
                 tm=tm_mix)
    y_prompt = _ffn(xp, mod_p, 6, seq // tm_p, norm_ffn2[l], *f2, norm_final, tm=tm_p).reshape(bp, seq, D_MODEL)

    xs = x_sample.reshape(bs, D_MODEL)
    xs = _ffn(xs, mod_s, 0, 1, norm_ffn1[l], *f1, tm=bs)
    q_lat_ts, q_rope_ts, ckv_s, _, _, krope_t_s, _, u_s = _mixin(xs, mod_s, 1, bs, p, cos_s, sin_s, tm=bs, tq=bs)
    krope_s = krope_t_s[0].T
    q_lat_s = q_lat_ts.reshape(KV_LORA, MLA_HEADS, bs).transpose(2, 1, 0)
    q_rope_s = q_rope_ts.reshape(QK_ROPE, MLA_HEADS, bs).transpose(2, 1, 0)
    o_lat = _sample_attn(page_table, q_lat_s, q_rope_s,
                         ckv_s.reshape(bs, 1, KV_LORA), krope_s.reshape(bs, 1, QK_ROPE), cache_ckv[l:l + 1],
                         cache_krope[l:l + 1].transpose(0, 1, 3, 2))
    attn_s = _sample_attn_out(o_lat.transpose(1, 0, 2), p["w_uv_h"], p["norm_attn_out"])
    y_s, hre_s, him_s = _s5_sample(u_s.reshape(bs, S5_WIDTH), state_s5_re[l], state_s5_im[l], ops)
    xs = _mixout(xs, attn_s, y_s, mod_s, 1, p, tm=bs)
    y_sample = _ffn(xs, mod_s, 6, 1, norm_ffn2[l], *f2, norm_final, tm=bs).reshape(bs, 1, D_MODEL)

    return (y_prompt, y_sample,
            ckv_p[None], krope_t_p.transpose(0, 2, 1)[None],
            ckv_s.reshape(1, bs, 1, KV_LORA), krope_s.reshape(1, bs, 1, QK_ROPE),
            hre_p[None], him_p[None], hre_s[None], him_s[None])
```

```python
import functools
from typing import NamedTuple

import jax
import jax.numpy as jnp
from jax import lax
from jax.experimental import pallas as pl
from jax.experimental.pallas import tpu as pltpu

F32 = jnp.float32
BF16 = jnp.bfloat16

D_MODEL = 2048
D_FF = 5632
MLA_HEADS = 8
QK_NOPE = 128
QK_ROPE = 64
V_HEAD = 128
Q_LORA = 512
KV_LORA = 256
MLA_WIDTH = 1024
S5_WIDTH = 1024
S5_GROUP = 16
S5_GROUPS = 64
S5_STATE = 64
ADA_CHUNKS = 9
PAGE = 128
ROPE_THETA = 10000.0
SOFTMAX_SCALE = (QK_NOPE + QK_ROPE) ** -0.5
EPS = 1e-6

LANES = 128
S5_CHUNK = 16
GROUPS_PER_SLAB = LANES // S5_GROUP
N_SLABS = S5_WIDTH // LANES
SLAB_STATE = GROUPS_PER_SLAB * S5_STATE
CHUNK_COLS = S5_CHUNK * LANES
MXU_TILE = 2 * LANES
VMEM_LIMIT = 56 * 1024 * 1024


class _Tiles(NamedTuple):
    ada_cols: int = 1024
    ffn_rows: int = 512
    ffn_cols: int = 512
    mix_rows: int = 512
    attn_q: int = 512
    attn_k: int = 512
    pages: int = 16
    streams: int = 2
    slots: int = 3


TILES = _Tiles()


def _cparams(sem):
    return pltpu.CompilerParams(dimension_semantics=sem, vmem_limit_bytes=VMEM_LIMIT)


def _rms(x, g):
    return x * lax.rsqrt(jnp.mean(x * x, axis=-1, keepdims=True) + EPS) * g


def _rms_rows(x, g):
    return x * lax.rsqrt(jnp.mean(x * x, axis=0, keepdims=True) + EPS) * g


def _dot(a, b):
    return jnp.dot(a, b, preferred_element_type=F32)


def _dot_nt(a, b):
    return lax.dot_general(a, b, (((1,), (1,)), ((), ())), preferred_element_type=F32)


def _ada_kernel(c_ref, w_ref, b_ref, o_ref):
    c = c_ref[...]
    a = (c * jax.nn.sigmoid(c)).astype(BF16)
    o_ref[...] = _dot(a, w_ref[...].astype(BF16)) + b_ref[...]


def _ada(c, w_ada, b_ada):
    rows = c.shape[0]
    n = w_ada.shape[1]
    tn = TILES.ada_cols
    return pl.pallas_call(
        _ada_kernel,
        grid=(n // tn,),
        in_specs=[
            pl.BlockSpec((rows, D_MODEL), lambda j: (0, 0)),
            pl.BlockSpec((D_MODEL, tn), lambda j: (0, j)),
            pl.BlockSpec((1, tn), lambda j: (0, j)),
        ],
        out_specs=pl.BlockSpec((rows, tn), lambda j: (0, j)),
        out_shape=jax.ShapeDtypeStruct((rows, n), F32),
        compiler_params=_cparams(("arbitrary",)),
        name="ada_modulation",
    )(c, w_ada, b_ada.reshape(1, n))


def _ffn_kernel(x_ref, sh_ref, sc_ref, gt_ref, g_ref, w1_ref, w3_ref, w2_ref, w2_last_ref, *rest, final_norm):
    if final_norm:
        gf_ref, o_ref, h_ref, act_ref = rest
    else:
        o_ref, h_ref, act_ref = rest
    j = pl.program_id(1)
    last = pl.num_programs(1) - 1

    def gate():
        h = h_ref[...]
        a = _dot(h, w1_ref[...])
        b = _dot(h, w3_ref[...])
        act_ref[...] = (a * jax.nn.sigmoid(a) * b).astype(BF16)

    def down():
        return _dot(act_ref[...], w2_ref[...])

    @pl.when(j == 0)
    def _():
        h = _rms(x_ref[...], g_ref[...]) * (1.0 + sc_ref[0]) + sh_ref[0]
        h_ref[...] = h.astype(BF16)
        gate()

    @pl.when(j == 1)
    def _():
        o_ref[...] = down()
        gate()

    @pl.when((j > 1) & (j < last))
    def _():
        o_ref[...] += down()
        gate()

    @pl.when(j == last)
    def _():
        prev = down()
        gate()
        y = x_ref[...] + 0.5 * gt_ref[0] * (o_ref[...] + prev + _dot(act_ref[...], w2_last_ref[...]))
        if final_norm:
            y = _rms(y, gf_ref[...])
        o_ref[...] = y


def _ffn(x, mod, chunk0, tiles_per_b, g, w1, w3, w2, gf=None, *, tm, tf=TILES.ffn_cols):
    n = x.shape[0]
    r = mod.shape[1]
    n_ff = D_FF // tf
    assert n_ff >= 3

    def mod_spec(k):
        return pl.BlockSpec((1, r, D_MODEL), lambda i, j: (i // tiles_per_b, 0, k))

    in_specs = [
        pl.BlockSpec((tm, D_MODEL), lambda i, j: (i, 0)),
        mod_spec(chunk0), mod_spec(chunk0 + 1), mod_spec(chunk0 + 2),
        pl.BlockSpec((1, D_MODEL), lambda i, j: (0, 0)),
        pl.BlockSpec((D_MODEL, tf), lambda i, j: (0, j)),
        pl.BlockSpec((D_MODEL, tf), lambda i, j: (0, j)),
        pl.BlockSpec((tf, D_MODEL), lambda i, j: (jnp.maximum(j - 1, 0), 0)),
        pl.BlockSpec((tf, D_MODEL), lambda i, j: (n_ff - 1, 0)),
    ]
    args = [x, mod, mod, mod, g.reshape(1, D_MODEL), w1, w3, w2, w2]
    if gf is not None:
        in_specs.append(pl.BlockSpec((1, D_MODEL), lambda i, j: (0, 0)))
        args.append(gf.reshape(1, D_MODEL))
    return pl.pallas_call(
        functools.partial(_ffn_kernel, final_norm=gf is not None),
        grid=(n // tm, n_ff),
        in_specs=in_specs,
        out_specs=pl.BlockSpec((tm, D_MODEL), lambda i, j: (i, 0)),
        out_shape=jax.ShapeDtypeStruct((n, D_MODEL), F32),
        scratch_shapes=[pltpu.VMEM((tm, D_MODEL), BF16), pltpu.VMEM((tm, tf), BF16)],
        compiler_params=_cparams(("parallel", "arbitrary")),
        name="macaron_ffn",
    )(*args)


def _rope_rows(x, cos, sin):
    half = QK_ROPE // 2
    x1 = x[:half]
    x2 = x[half:]
    return x1 * cos - x2 * sin, x1 * sin + x2 * cos


def _mixin_kernel(x_ref, sh_ref, sc_ref, g_ref, wq_ref, wkv_ref, wkr_ref, wu_ref, gq_ref, wqn_ref, wqr_ref,
                  wuk_ref, gkv_ref, cos_ref, sin_ref,
                  qlat_ref, qrope_ref, ckv_ref, kc_ref, kct_ref, krope_ref, kr_ref, u_ref, *, tq):
    half = QK_ROPE // 2
    n_q = x_ref.shape[0] // tq
    h = (_rms(x_ref[...], g_ref[...]) * (1.0 + sc_ref[0]) + sh_ref[0]).astype(BF16)
    cos = cos_ref[...]
    sin = sin_ref[...]

    u_ref[0] = _dot(h, wu_ref[...])

    ckv = _rms(_dot(h, wkv_ref[...]), gkv_ref[...])
    ckv_ref[0] = ckv
    kc_ref[0] = ckv.astype(BF16)
    kct_ref[0] = ckv.T.astype(BF16)

    k1, k2 = _rope_rows(_dot_nt(wkr_ref[...], h), cos, sin)
    kr_t = jnp.concatenate([k1, k2], axis=0)
    krope_ref[0] = kr_t
    kr_ref[0] = kr_t.T.astype(BF16)

    qn = _rms(_dot(h, wq_ref[...]), gq_ref[...]).astype(BF16)
    q_nope = (_dot(qn, wqn_ref[...]) * SOFTMAX_SCALE).astype(BF16)
    q_rope_t = _dot_nt(wqr_ref[...], qn) * SOFTMAX_SCALE
    for hd in range(MLA_HEADS):
        ql_t = _dot_nt(wuk_ref[hd], q_nope[:, hd * QK_NOPE:(hd + 1) * QK_NOPE]).astype(BF16)
        r1, r2 = _rope_rows(q_rope_t[hd * QK_ROPE:(hd + 1) * QK_ROPE], cos, sin)
        r1 = r1.astype(BF16)
        r2 = r2.astype(BF16)
        for qq in range(n_q):
            src = slice(qq * tq, (qq + 1) * tq)
            dst = slice(hd * tq, (hd + 1) * tq)
            qlat_ref[0, qq, :, dst] = ql_t[:, src]
            qrope_ref[0, qq, :half, dst] = r1[:, src]
            qrope_ref[0, qq, half:, dst] = r2[:, src]


def _mixin(x, mod, tiles_per_b, t_len, p, cos_t, sin_t, *, tm, tq):
    n = x.shape[0]
    nb = n // t_len
    r = mod.shape[1]
    half = QK_ROPE // 2
    full = lambda shape: pl.BlockSpec(shape, lambda i: (0,) * len(shape))
    rows = lambda width: pl.BlockSpec((1, tm, width), lambda i: (i // tiles_per_b, i % tiles_per_b, 0))
    cols = lambda height: pl.BlockSpec((1, height, tm), lambda i: (i // tiles_per_b, 0, i % tiles_per_b))
    head_cols = lambda height: pl.BlockSpec((1, tm // tq, height, MLA_HEADS * tq),
                                            lambda i: (i // tiles_per_b, i % tiles_per_b, 0, 0))
    in_specs = [
        pl.BlockSpec((tm, D_MODEL), lambda i: (i, 0)),
        pl.BlockSpec((1, r, D_MODEL), lambda i: (i // tiles_per_b, 0, 3)),
        pl.BlockSpec((1, r, D_MODEL), lambda i: (i // tiles_per_b, 0, 4)),
        full((1, D_MODEL)),
        full((D_MODEL, Q_LORA)), full((D_MODEL, KV_LORA)), full((QK_ROPE, D_MODEL)), full((D_MODEL, S5_WIDTH)),
        full((1, Q_LORA)), full((Q_LORA, MLA_HEADS * QK_NOPE)), full((MLA_HEADS * QK_ROPE, Q_LORA)),
        full((MLA_HEADS, KV_LORA, QK_NOPE)), full((1, KV_LORA)),
        pl.BlockSpec((half, tm), lambda i: (0, i % tiles_per_b)),
        pl.BlockSpec((half, tm), lambda i: (0, i % tiles_per_b)),
    ]
    out_specs = [head_cols(KV_LORA), head_cols(QK_ROPE), rows(KV_LORA), rows(KV_LORA), cols(KV_LORA),
                 cols(QK_ROPE), rows(QK_ROPE), rows(S5_WIDTH)]
    out_shape = [
        jax.ShapeDtypeStruct((nb, t_len // tq, KV_LORA, MLA_HEADS * tq), BF16),
        jax.ShapeDtypeStruct((nb, t_len // tq, QK_ROPE, MLA_HEADS * tq), BF16),
        jax.ShapeDtypeStruct((nb, t_len, KV_LORA), F32),
        jax.ShapeDtypeStruct((nb, t_len, KV_LORA), BF16),
        jax.ShapeDtypeStruct((nb, KV_LORA, t_len), BF16),
        jax.ShapeDtypeStruct((nb, QK_ROPE, t_len), F32),
        jax.ShapeDtypeStruct((nb, t_len, QK_ROPE), BF16),
        jax.ShapeDtypeStruct((nb, t_len, S5_WIDTH), F32),
    ]
    return pl.pallas_call(
        functools.partial(_mixin_kernel, tq=tq),
        grid=(n // tm,),
        in_specs=in_specs,
        out_specs=out_specs,
        out_shape=out_shape,
        compiler_params=_cparams(("parallel",)),
        name="mixer_in",
    )(x, mod, mod, p["norm_mix"], p["w_q"], p["w_kv"], p["w_kr_t"], p["w_u"], p["norm_q"], p["w_uq_nope"],
      p["w_uq_rope_t"], p["w_uk_h"], p["norm_kv"], cos_t, sin_t)


def _prompt_attn_kernel(ql_ref, qr_ref, kc_ref, kr_ref, kct_ref, wuv_ref, g_ref, o_ref, *scratch, tq, tk):
    qi = pl.program_id(1)
    n_half = len(scratch) // 3
    m_ref, l_ref, acc_ref = scratch[:n_half], scratch[n_half:2 * n_half], scratch[2 * n_half:]
    width = m_ref[0].shape[1]
    heads_per_half = width // tq
    for hf in range(n_half):
        m_ref[hf][...] = jnp.full_like(m_ref[hf], -jnp.inf)
        l_ref[hf][...] = jnp.zeros_like(l_ref[hf])
        acc_ref[hf][...] = jnp.zeros_like(acc_ref[hf])

    def step(ki, diag):
        start = pl.multiple_of(ki * tk, tk)
        kc = kc_ref[0, pl.ds(start, tk), :]
        kr = kr_ref[0, pl.ds(start, tk), :]
        kct = kct_ref[0, :, pl.ds(start, tk)]
        masked = diag is not None
        if masked:
            k_local = lax.broadcasted_iota(jnp.int32, (tk, width), 0) + diag * tk
            t_local = lax.broadcasted_iota(jnp.int32, (tk, width), 1) & (tq - 1)
            keep = k_local <= t_local
        for hf in range(n_half):
            lanes = slice(hf * width, (hf + 1) * width)
            s = _dot(kc, ql_ref[0, 0, :, lanes]) + _dot(kr, qr_ref[0, 0, :, lanes])
            if masked:
                s = jnp.where(keep, s, -jnp.inf)
            m_old = m_ref[hf][...]
            m_new = jnp.maximum(m_old, jnp.max(s, axis=0, keepdims=True))
            alpha = jnp.exp(m_old - m_new)
            pexp = jnp.exp(s - m_new)
            l_ref[hf][...] = alpha * l_ref[hf][...] + jnp.sum(pexp, axis=0, keepdims=True)
            acc_ref[hf][...] = alpha * acc_ref[hf][...] + _dot(kct, pexp.astype(BF16))
            m_ref[hf][...] = m_new

    def body(ki, carry):
        step(ki, None)
        return carry

    blocks_per_tile = tq // tk
    lax.fori_loop(0, qi * blocks_per_tile, body, 0)
    for d in range(blocks_per_tile):
        step(qi * blocks_per_tile + d, d)
    parts = []
    for hd in range(MLA_HEADS):
        hf = hd // heads_per_half
        lanes = slice((hd % heads_per_half) * tq, (hd % heads_per_half + 1) * tq)
        o_t = acc_ref[hf][:, lanes] / l_ref[hf][:, lanes]
        parts.append(_dot(wuv_ref[hd], o_t.astype(BF16)))
    attn_t = _rms_rows(jnp.concatenate(parts, axis=0), g_ref[...])
    o_ref[0] = attn_t.T.astype(BF16)


def _prompt_attn(q_lat_t, q_rope_t, kc, kr, kc_t, w_uv_h, g_col, *, tq, tk, n_half=1):
    nb, n_q, _, _ = q_lat_t.shape
    t_len = n_q * tq
    width = MLA_HEADS * tq // n_half
    assert tq % tk == 0
    return pl.pallas_call(
        functools.partial(_prompt_attn_kernel, tq=tq, tk=tk),
        grid=(nb, n_q),
        in_specs=[
            pl.BlockSpec((1, 1, KV_LORA, MLA_HEADS * tq), lambda b, i: (b, i, 0, 0)),
            pl.BlockSpec((1, 1, QK_ROPE, MLA_HEADS * tq), lambda b, i: (b, i, 0, 0)),
            pl.BlockSpec((1, t_len, KV_LORA), lambda b, i: (b, 0, 0)),
            pl.BlockSpec((1, t_len, QK_ROPE), lambda b, i: (b, 0, 0)),
            pl.BlockSpec((1, KV_LORA, t_len), lambda b, i: (b, 0, 0)),
            pl.BlockSpec((MLA_HEADS, V_HEAD, KV_LORA), lambda b, i: (0, 0, 0)),
            pl.BlockSpec((MLA_WIDTH, 1), lambda b, i: (0, 0)),
        ],
        out_specs=pl.BlockSpec((1, tq, MLA_WIDTH), lambda b, i: (b, i, 0)),
        out_shape=jax.ShapeDtypeStruct((nb, t_len, MLA_WIDTH), BF16),
        scratch_shapes=([pltpu.VMEM((1, width), F32)] * (2 * n_half) + [pltpu.VMEM((KV_LORA, width), F32)] * n_half),
        compiler_params=_cparams(("parallel", "parallel")),
        name="prompt_attention",
    )(q_lat_t, q_rope_t, kc, kr, kc_t, w_uv_h, g_col)


def _sample_attn_kernel(pt_ref, ql_ref, qr_ref, kcn_ref, krn_ref, ckv_hbm, kr_hbm, o_ref, kbuf, rbuf, sem_k, sem_r,
                        *, pages, streams):
    nb, n_pages = pt_ref.shape
    n_chunks = n_pages // pages
    n_groups = nb // streams
    total = n_groups * n_chunks
    n_slots = kbuf.shape[0]
    ahead = n_slots - 1

    def chunk_copies(g, slot):
        grp = g // n_chunks
        c = g % n_chunks
        copies = []
        for st in range(streams):
            b = grp * streams + st
            for i in range(pages):
                page = pt_ref[b, c * pages + i]
                copies.append(pltpu.make_async_copy(
                    ckv_hbm.at[0, page], kbuf.at[slot, st, pl.ds(i * PAGE, PAGE), :], sem_k.at[slot, st]))
                copies.append(pltpu.make_async_copy(
                    kr_hbm.at[0, page], rbuf.at[slot, st, i], sem_r.at[slot, st]))
        return copies

    grp = pl.program_id(0)

    @pl.when(grp == 0)
    def _():
        for g0 in range(ahead):
            for cp in chunk_copies(g0, g0):
                cp.start()

    qs = [(ql_ref[st], qr_ref[st]) for st in range(streams)]

    def chunk_body(c, carry):
        g = grp * n_chunks + c
        slot = g % n_slots

        @pl.when(g + ahead < total)
        def _():
            for cp in chunk_copies(g + ahead, (g + ahead) % n_slots):
                cp.start()

        for cp in chunk_copies(g, slot):
            cp.wait()
        out = []
        for st in range(streams):
            m_old, l_old, acc_old = carry[st]
            ql, qr = qs[st]
            kc = kbuf[slot, st].astype(BF16)
            kr_t = jnp.concatenate([rbuf[slot, st, i].astype(BF16) for i in range(pages)], axis=1)
            s = _dot_nt(ql, kc) + _dot(qr, kr_t)
            m_new = jnp.maximum(m_old, jnp.max(s, axis=1, keepdims=True))
            alpha = jnp.exp(m_old - m_new)
            pexp = jnp.exp(s - m_new)
            l_new = alpha * l_old + jnp.sum(pexp, axis=1, keepdims=True)
            acc_new = alpha * acc_old + _dot(pexp.astype(BF16), kc)
            out.append((m_new, l_new, acc_new))
        return tuple(out)

    init = tuple((jnp.full((MLA_HEADS, 1), -jnp.inf, F32), jnp.zeros((MLA_HEADS, 1), F32),
                  jnp.zeros((MLA_HEADS, KV_LORA), F32)) for _ in range(streams))
    final = lax.fori_loop(0, n_chunks, chunk_body, init)
    for st in range(streams):
        m_old, l_old, acc_old = final[st]
        ql, qr = qs[st]
        kcn = kcn_ref[st].astype(BF16).astype(F32)
        krn = krn_ref[st].astype(BF16).astype(F32)
        s_n = (jnp.sum(ql.astype(F32) * kcn, axis=1, keepdims=True)
               + jnp.sum(qr.astype(F32) * krn, axis=1, keepdims=True))
        m_f = jnp.maximum(m_old, s_n)
        a_f = jnp.exp(m_old - m_f)
        p_n = jnp.exp(s_n - m_f)
        l_f = a_f * l_old + p_n
        acc_f = a_f * acc_old + p_n.astype(BF16).astype(F32) * kcn
        o_ref[st] = acc_f / l_f


def _sample_attn(page_table, q_lat, q_rope, ckv_new, krope_new, cache_ckv, cache_krope_t, *, pages=TILES.pages,
                 streams=TILES.streams, slots=TILES.slots):
    nb, n_pages = page_table.shape
    assert n_pages % pages == 0 and nb % streams == 0 and (nb // streams) * (n_pages // pages) >= slots
    per_group = lambda rows, width: pl.BlockSpec((streams, rows, width), lambda g, pt: (g, 0, 0))
    grid_spec = pltpu.PrefetchScalarGridSpec(
        num_scalar_prefetch=1,
        grid=(nb // streams,),
        in_specs=[
            per_group(MLA_HEADS, KV_LORA), per_group(MLA_HEADS, QK_ROPE),
            per_group(1, KV_LORA), per_group(1, QK_ROPE),
            pl.BlockSpec(memory_space=pl.ANY), pl.BlockSpec(memory_space=pl.ANY),
        ],
        out_specs=per_group(MLA_HEADS, KV_LORA),
        scratch_shapes=[
            pltpu.VMEM((slots, streams, pages * PAGE, KV_LORA), F32),
            pltpu.VMEM((slots, streams, pages, QK_ROPE, PAGE), F32),
            pltpu.SemaphoreType.DMA((slots, streams)),
            pltpu.SemaphoreType.DMA((slots, streams)),
        ],
    )
    return pl.pallas_call(
        functools.partial(_sample_attn_kernel, pages=pages, streams=streams),
        grid_spec=grid_spec,
        out_shape=jax.ShapeDtypeStruct((nb, MLA_HEADS, KV_LORA), F32),
        compiler_params=_cparams(("arbitrary",)),
        name="sample_paged_attention",
    )(page_table, q_lat, q_rope, ckv_new, krope_new, cache_ckv, cache_krope_t)


def _sample_attn_out_kernel(o_ref, wuv_ref, g_ref, out_ref):
    parts = [_dot_nt(o_ref[hd].astype(BF16), wuv_ref[hd]) for hd in range(MLA_HEADS)]
    out_ref[...] = _rms(jnp.concatenate(parts, axis=1), g_ref[...]).astype(BF16)


def _sample_attn_out(o_lat_h, w_uv_h, g):
    nb = o_lat_h.shape[1]
    return pl.pallas_call(
        _sample_attn_out_kernel,
        out_shape=jax.ShapeDtypeStruct((nb, MLA_WIDTH), BF16),
        name="sample_attention_out",
    )(o_lat_h, w_uv_h, g)


def _same_group(shape, row_shift, col_shift):
    rows = lax.broadcasted_iota(jnp.int32, shape, 0)
    cols = lax.broadcasted_iota(jnp.int32, shape, 1)
    mask = GROUPS_PER_SLAB - 1
    return ((rows >> row_shift) & mask) == ((cols >> col_shift) & mask)


def _s5_prompt_kernel(u_ref, wr_ref, we_ref, wf_ref, tr_ref, te_ref, tf_ref, are_ref, aim_ref, d_ref,
                      y_ref, hre_ref, him_ref, r_ref, e_ref, f_ref, uc_ref, s_ref, hp_ref, yc_ref):
    n_chunks = uc_ref.shape[0]
    n_blk = CHUNK_COLS // MXU_TILE
    ch_bits = S5_GROUP.bit_length() - 1
    st_bits = S5_STATE.bit_length() - 1

    @pl.when(pl.program_id(1) == 0)
    def _():
        zero = jnp.zeros((), F32)
        r_ref[...] = jnp.where(_same_group(r_ref.shape, ch_bits, ch_bits),
                               _dot(wr_ref[0], tr_ref[...]), zero).astype(BF16)
        e_ref[...] = jnp.where(_same_group(e_ref.shape, ch_bits, st_bits),
                               _dot(we_ref[0], te_ref[...]), zero).astype(BF16)
        f_ref[...] = jnp.where(_same_group(f_ref.shape, st_bits, ch_bits),
                               _dot(tf_ref[...], wf_ref[0]), zero).astype(BF16)

    for s in range(S5_CHUNK):
        uc_ref[:, s * LANES:(s + 1) * LANES] = u_ref[0, pl.ds(s, n_chunks, stride=S5_CHUNK), :]
    uc = uc_ref[...]
    ub = uc.astype(BF16)

    s_ref[...] = _dot(ub, e_ref[...])

    a_re = are_ref[0]
    a_im = aim_ref[0]

    def scan(k, carry):
        h_re, h_im = carry
        hp_ref[pl.ds(k, 1), :] = jnp.concatenate([h_re, h_im], axis=1)
        row = s_ref[pl.ds(k, 1), :]
        n_re = a_re * h_re - a_im * h_im + row[:, :SLAB_STATE]
        n_im = a_re * h_im + a_im * h_re + row[:, SLAB_STATE:]
        return n_re, n_im

    zero_state = jnp.zeros((1, SLAB_STATE), F32)
    h_re, h_im = lax.fori_loop(0, n_chunks, scan, (zero_state, zero_state), unroll=8)
    hre_ref[0, 0] = h_re
    him_ref[0, 0] = h_im

    yc_ref[...] = _dot(hp_ref[...].astype(BF16), f_ref[...]) + uc * d_ref[0]
    for tb in range(n_blk):
        yc_ref[:, tb * MXU_TILE:(tb + 1) * MXU_TILE] += _dot(ub[:, :(tb + 1) * MXU_TILE],
                                                             r_ref[(n_blk - 1 - tb) * MXU_TILE:, :])
    y = jax.nn.gelu(yc_ref[...])
    for t in range(S5_CHUNK):
        y_ref[0, pl.ds(t, n_chunks, stride=S5_CHUNK), :] = y[:, t * LANES:(t + 1) * LANES]


def _s5_prompt(u, ops):
    nb, t_len, _ = u.shape
    n_chunks = t_len // S5_CHUNK
    slab = lambda shape: pl.BlockSpec((1,) + shape, lambda j, b: (j,) + (0,) * len(shape))
    full = lambda shape: pl.BlockSpec(shape, lambda j, b: (0,) * len(shape))
    y, hre, him = pl.pallas_call(
        _s5_prompt_kernel,
        grid=(N_SLABS, nb),
        in_specs=[
            pl.BlockSpec((1, t_len, LANES), lambda j, b: (b, 0, j)),
            slab((CHUNK_COLS, 2 * S5_GROUP)), slab((CHUNK_COLS, 2 * S5_STATE)), slab((2 * S5_STATE, CHUNK_COLS)),
            full((2 * S5_GROUP, MXU_TILE)), full((2 * S5_STATE, 2 * SLAB_STATE)), full((2 * SLAB_STATE, 2 * S5_STATE)),
            slab((1, SLAB_STATE)), slab((1, SLAB_STATE)), slab((1, CHUNK_COLS)),
        ],
        out_specs=[
            pl.BlockSpec((1, t_len, LANES), lambda j, b: (b, 0, j)),
            pl.BlockSpec((1, 1, 1, SLAB_STATE), lambda j, b: (j, b, 0, 0)),
            pl.BlockSpec((1, 1, 1, SLAB_STATE), lambda j, b: (j, b, 0, 0)),
        ],
        out_shape=[
            jax.ShapeDtypeStruct((nb, t_len, S5_WIDTH), F32),
            jax.ShapeDtypeStruct((N_SLABS, nb, 1, SLAB_STATE), F32),
            jax.ShapeDtypeStruct((N_SLABS, nb, 1, SLAB_STATE), F32),
        ],
        scratch_shapes=[
            pltpu.VMEM((CHUNK_COLS, MXU_TILE), BF16),
            pltpu.VMEM((CHUNK_COLS, 2 * SLAB_STATE), BF16),
            pltpu.VMEM((2 * SLAB_STATE, CHUNK_COLS), BF16),
            pltpu.VMEM((n_chunks, CHUNK_COLS), F32),
            pltpu.VMEM((n_chunks, 2 * SLAB_STATE), F32),
            pltpu.VMEM((n_chunks, 2 * SLAB_STATE), F32),
            pltpu.VMEM((n_chunks, CHUNK_COLS), F32),
        ],
        compiler_params=_cparams(("arbitrary", "arbitrary")),
        name="s5_prompt",
    )(u, ops["wr"], ops["we"], ops["wf"], ops["tile_r"], ops["tile_e"], ops["tile_f"], ops["a_re"], ops["a_im"],
      ops["d_chunk"])

    def to_state(h):
        return h.reshape(N_SLABS, nb, GROUPS_PER_SLAB, S5_STATE).transpose(1, 0, 2, 3).reshape(nb, S5_GROUPS, S5_STATE)

    return y, to_state(hre), to_state(him)


def _split_bf16(x):
    hi = x.astype(BF16)
    return hi, (x - hi.astype(F32)).astype(BF16)


def _s5_sample_kernel(u_ref, h0re_ref, h0im_ref, bd_hi_ref, bd_lo_ref, cd_ref, lre_ref, lim_ref, d_ref,
                      y_ref, hre_ref, him_ref):
    u = u_ref[...]
    u_hi, u_lo = _split_bf16(u)
    bu = _dot(u_hi, bd_hi_ref[0]) + (_dot(u_hi, bd_lo_ref[0]) + _dot(u_lo, bd_hi_ref[0]))
    l_re = lre_ref[0]
    l_im = lim_ref[0]
    h0_re = h0re_ref[...]
    h0_im = h0im_ref[...]
    h_re = l_re * h0_re - l_im * h0_im + bu[:, :SLAB_STATE]
    h_im = l_re * h0_im + l_im * h0_re + bu[:, SLAB_STATE:]
    hre_ref[...] = h_re
    him_ref[...] = h_im
    h = jnp.concatenate([h_re, h_im], axis=1).astype(BF16)
    y_ref[...] = jax.nn.gelu(_dot(h, cd_ref[0]) + u * d_ref[0])


def _s5_sample(u, h0_re, h0_im, ops):
    nb = u.shape[0]
    n_state = S5_GROUPS * S5_STATE
    slab = lambda shape: pl.BlockSpec((1,) + shape, lambda j: (j,) + (0,) * len(shape))
    col = lambda width: pl.BlockSpec((nb, width), lambda j: (0, j))
    y, hre, him = pl.pallas_call(
        _s5_sample_kernel,
        grid=(N_SLABS,),
        in_specs=[
            col(LANES), col(SLAB_STATE), col(SLAB_STATE),
            slab((LANES, 2 * SLAB_STATE)), slab((LANES, 2 * SLAB_STATE)), slab((2 * SLAB_STATE, LANES)),
            slab((1, SLAB_STATE)), slab((1, SLAB_STATE)), slab((1, LANES)),
        ],
        out_specs=[col(LANES), col(SLAB_STATE), col(SLAB_STATE)],
        out_shape=[
            jax.ShapeDtypeStruct((nb, S5_WIDTH), F32),
            jax.ShapeDtypeStruct((nb, n_state), F32),
            jax.ShapeDtypeStruct((nb, n_state), F32),
        ],
        compiler_params=_cparams(("parallel",)),
        name="s5_sample",
    )(u, h0_re.reshape(nb, n_state), h0_im.reshape(nb, n_state), ops["bd_hi"], ops["bd_lo"], ops["cd"],
      ops["l_re"], ops["l_im"], ops["d_slab"])
    return y, hre.reshape(nb, S5_GROUPS, S5_STATE), him.reshape(nb, S5_GROUPS, S5_STATE)


def _mixout_kernel(x_ref, attn_ref, y_ref, gt_ref, wglu_ref, bglu_ref, gs_ref, woa_ref, woy_ref, o_ref):
    y = y_ref[...]
    z = _dot(y.astype(BF16), wglu_ref[...]) + bglu_ref[...]
    yn = _rms(y * jax.nn.sigmoid(z), gs_ref[...]).astype(BF16)
    mix = _dot(attn_ref[...], woa_ref[...]) + _dot(yn, woy_ref[...])
    o_ref[...] = x_ref[...] + gt_ref[0] * mix


def _mixout(x, attn, y, mod, tiles_per_b, p, *, tm):
    n = x.shape[0]
    r = mod.shape[1]
    full = lambda shape: pl.BlockSpec(shape, lambda i: (0,) * len(shape))
    return pl.pallas_call(
        _mixout_kernel,
        grid=(n // tm,),
        in_specs=[
            pl.BlockSpec((tm, D_MODEL), lambda i: (i, 0)),
            pl.BlockSpec((tm, MLA_WIDTH), lambda i: (i, 0)),
            pl.BlockSpec((tm, S5_WIDTH), lambda i: (i, 0)),
            pl.BlockSpec((1, r, D_MODEL), lambda i: (i // tiles_per_b, 0, 5)),
            full((S5_WIDTH, S5_WIDTH)), full((1, S5_WIDTH)), full((1, S5_WIDTH)),
            full((MLA_WIDTH, D_MODEL)), full((S5_WIDTH, D_MODEL)),
        ],
        out_specs=pl.BlockSpec((tm, D_MODEL), lambda i: (i, 0)),
        out_shape=jax.ShapeDtypeStruct((n, D_MODEL), F32),
        compiler_params=_cparams(("parallel",)),
        name="mixer_out",
    )(x, attn, y, mod, p["w_glu"], p["b_glu"], p["norm_ssm_out"], p["w_out_attn"], p["w_out_ssm"])


def _rope_tables(pos):
    half = QK_ROPE // 2
    inv_freq = ROPE_THETA ** (-jnp.arange(half, dtype=F32) / half)
    ang = pos.astype(F32)[:, None] * inv_freq[None, :]
    return jnp.cos(ang).T, jnp.sin(ang).T


def _group_diag(x, g_axis, new_axis):
    x = jnp.expand_dims(x, new_axis)
    shape = [1] * x.ndim
    shape[g_axis if g_axis < new_axis else g_axis + 1] = GROUPS_PER_SLAB
    shape[new_axis] = GROUPS_PER_SLAB
    return x * jnp.eye(GROUPS_PER_SLAB, dtype=x.dtype).reshape(shape)


def _by_slab(x, g_axis):
    return x.reshape(x.shape[:g_axis] + (N_SLABS, GROUPS_PER_SLAB) + x.shape[g_axis + 1:])


def _replicate_over_groups(n_outer, n_inner):
    eo = jnp.eye(n_outer, dtype=F32)[:, None, :, None, None]
    ei = jnp.eye(n_inner, dtype=F32)[None, :, None, None, :]
    ones = jnp.ones((1, 1, 1, GROUPS_PER_SLAB, 1), F32)
    return (eo * ei * ones).reshape(n_outer * n_inner, n_outer * GROUPS_PER_SLAB * n_inner)


def _s5_operators(a_re, a_im, log_dt, b_re, b_im, c_re, c_im, d_skip):
    hp = lax.Precision.HIGHEST
    a_re, a_im, b_re, b_im, c_re, c_im = (v.astype(F32) for v in (a_re, a_im, b_re, b_im, c_re, c_im))
    dt = jnp.exp(log_dt.astype(F32))[:, None]
    z_re = a_re * dt
    z_im = a_im * dt

    def lam_pow(n):
        mag = jnp.exp(z_re[None] * n[:, None, None])
        ang = z_im[None] * n[:, None, None]
        return mag * jnp.cos(ang), mag * jnp.sin(ang)

    steps = jnp.arange(S5_CHUNK + 1, dtype=F32)
    pw_re, pw_im = lam_pow(steps)
    lb_re, lb_im = pw_re[1], pw_im[1]
    den = a_re * a_re + a_im * a_im
    q_re = ((lb_re - 1.0) * a_re + lb_im * a_im) / den
    q_im = (lb_im * a_re - (lb_re - 1.0) * a_im) / den
    bb_re = q_re[:, :, None] * b_re - q_im[:, :, None] * b_im
    bb_im = q_re[:, :, None] * b_im + q_im[:, :, None] * b_re

    cp_re = c_re[None] * pw_re[:S5_CHUNK, :, None, :] - c_im[None] * pw_im[:S5_CHUNK, :, None, :]
    cp_im = c_re[None] * pw_im[:S5_CHUNK, :, None, :] + c_im[None] * pw_re[:S5_CHUNK, :, None, :]
    m = jnp.einsum("ngcq,gqd->ngcd", jnp.concatenate([cp_re, -cp_im], axis=3),
                   jnp.concatenate([bb_re, bb_im], axis=1), precision=hp)
    m_ext = jnp.concatenate([m, jnp.zeros_like(m[:1])], axis=0)
    n_blk = S5_CHUNK // 2
    d_i = (n_blk - 1 - jnp.arange(n_blk))[:, None, None]
    s_i = jnp.arange(2)[None, :, None]
    t_i = jnp.arange(2)[None, None, :]
    lag = 2 * d_i + t_i - s_i
    mg = _by_slab(m_ext[lag], 3)
    wr = mg.transpose(3, 0, 1, 4, 6, 2, 5).reshape(N_SLABS, CHUNK_COLS, 2 * S5_GROUP)

    pr_re, pr_im = lam_pow(S5_CHUNK - 1 - steps[:S5_CHUNK])
    w_re = pr_re[:, :, :, None] * bb_re[None] - pr_im[:, :, :, None] * bb_im[None]
    w_im = pr_re[:, :, :, None] * bb_im[None] + pr_im[:, :, :, None] * bb_re[None]

    def e_half(v):
        return _by_slab(v, 1).transpose(1, 0, 2, 4, 3).reshape(N_SLABS, CHUNK_COLS, S5_STATE)

    we = jnp.concatenate([e_half(w_re), e_half(w_im)], axis=2)

    g_re = c_re[None] * pw_re[1:, :, None, :] - c_im[None] * pw_im[1:, :, None, :]
    g_im = c_re[None] * pw_im[1:, :, None, :] + c_im[None] * pw_re[1:, :, None, :]

    def f_half(v):
        return _by_slab(v, 1).transpose(1, 4, 0, 2, 3).reshape(N_SLABS, S5_STATE, CHUNK_COLS)

    wf = jnp.concatenate([f_half(g_re), f_half(-g_im)], axis=1)

    def slab_vec(v):
        return v.reshape(N_SLABS, 1, SLAB_STATE)

    d_slab = d_skip.astype(F32).reshape(N_SLABS, 1, LANES)
    tile_e = _replicate_over_groups(2, S5_STATE)

    def bd_half(v):
        return _group_diag(_by_slab(v, 0).transpose(0, 1, 3, 2), 1, 3).reshape(N_SLABS, LANES, SLAB_STATE)

    def cd_half(v):
        return _group_diag(_by_slab(v, 0).transpose(0, 1, 3, 2), 1, 3).reshape(N_SLABS, SLAB_STATE, LANES)

    bd = jnp.concatenate([bd_half(bb_re), bd_half(bb_im)], axis=2)
    cd = jnp.concatenate([cd_half(c_re), cd_half(-c_im)], axis=1)
    bd_hi = bd.astype(BF16)
    bd_lo = (bd - bd_hi.astype(F32)).astype(BF16)
    return {
        "wr": wr.astype(BF16), "we": we.astype(BF16), "wf": wf.astype(BF16),
        "tile_r": _replicate_over_groups(2, S5_GROUP).astype(BF16),
        "tile_e": tile_e.astype(BF16), "tile_f": tile_e.T.astype(BF16),
        "a_re": slab_vec(pw_re[S5_CHUNK]), "a_im": slab_vec(pw_im[S5_CHUNK]),
        "d_chunk": jnp.tile(d_slab, (1, 1, S5_CHUNK)), "d_slab": d_slab,
        "bd_hi": bd_hi, "bd_lo": bd_lo, "cd": cd.astype(BF16),
        "l_re": slab_vec(lb_re), "l_im": slab_vec(lb_im),
    }


def _layer_params(w_in, w_uq, w_uk, w_uv, w_glu, w_out, norm_mix, norm_q, norm_kv, norm_attn_out, norm_ssm_out,
                  b_glu):
    c0, c1, c2 = Q_LORA, Q_LORA + KV_LORA, Q_LORA + KV_LORA + QK_ROPE
    w_uq_h = w_uq.reshape(Q_LORA, MLA_HEADS, QK_NOPE + QK_ROPE)
    return {
        "w_q": w_in[:, :c0].astype(BF16),
        "w_kv": w_in[:, c0:c1].astype(BF16),
        "w_kr_t": w_in[:, c1:c2].T.astype(BF16),
        "w_u": w_in[:, c2:].astype(BF16),
        "w_uq_nope": w_uq_h[:, :, :QK_NOPE].reshape(Q_LORA, MLA_HEADS * QK_NOPE).astype(BF16),
        "w_uq_rope_t": w_uq_h[:, :, QK_NOPE:].reshape(Q_LORA, MLA_HEADS * QK_ROPE).T.astype(BF16),
        "w_uk_h": w_uk.transpose(1, 0, 2).astype(BF16),
        "w_uv_h": w_uv.transpose(1, 2, 0).astype(BF16),
        "w_glu": w_glu.astype(BF16),
        "w_out_attn": w_out[:MLA_WIDTH].astype(BF16),
        "w_out_ssm": w_out[MLA_WIDTH:].astype(BF16),
        "norm_mix": norm_mix.reshape(1, D_MODEL),
        "norm_q": norm_q.reshape(1, Q_LORA),
        "norm_kv": norm_kv.reshape(1, KV_LORA),
        "norm_attn_out": norm_attn_out.reshape(1, MLA_WIDTH),
        "norm_attn_out_col": norm_attn_out.reshape(MLA_WIDTH, 1),
        "norm_ssm_out": norm_ssm_out.reshape(1, S5_WIDTH),
        "b_glu": b_glu.reshape(1, S5_WIDTH),
    }


def kernel(x_prompt, x_sample, c_prompt, c_sample, cache_ckv, cache_krope, state_s5_re, state_s5_im, page_table, w_ada, b_ada, norm_ffn1, ffn1_w1, ffn1_w3, ffn1_w2, norm_mix, w_in, norm_q, w_uq, norm_kv, w_uk, w_uv, s5_a_re, s5_a_im, s5_log_dt, s5_b_re, s5_b_im, s5_c_re, s5_c_im, s5_d, w_glu, b_glu, norm_attn_out, norm_ssm_out, w_out, norm_ffn2, ffn2_w1, ffn2_w3, ffn2_w2, norm_final):
    bp, seq, _ = x_prompt.shape
    bs = x_sample.shape[0]
    depth = w_ada.shape[0]
    assert depth == 1 and x_sample.shape[1] == 1
    n_pages = page_table.shape[1]
    past_len = n_pages * PAGE
    l = 0

    pad = (-(bs + bp)) % 8
    c_all = jnp.concatenate([c_sample, c_prompt, jnp.zeros((pad, D_MODEL), F32)], axis=0)
    mod = _ada(c_all, w_ada[l], b_ada[l])
    mod_s = mod[:bs].reshape(1, bs, ADA_CHUNKS * D_MODEL)
    mod_p = mod[bs:bs + bp].reshape(bp, 1, ADA_CHUNKS * D_MODEL)

    p = _layer_params(w_in[l], w_uq[l], w_uk[l], w_uv[l], w_glu[l], w_out[l], norm_mix[l], norm_q[l], norm_kv[l],
                      norm_attn_out[l], norm_ssm_out[l], b_glu[l])
    ops = _s5_operators(s5_a_re[l], s5_a_im[l], s5_log_dt[l], s5_b_re[l], s5_b_im[l], s5_c_re[l], s5_c_im[l],
                        s5_d[l])
    f1 = (ffn1_w1[l].astype(BF16), ffn1_w3[l].astype(BF16), ffn1_w2[l].astype(BF16))
    f2 = (ffn2_w1[l].astype(BF16), ffn2_w3[l].astype(BF16), ffn2_w2[l].astype(BF16))
    cos_p, sin_p = _rope_tables(jnp.arange(seq))
    cos_s, sin_s = _rope_tables(jnp.full((bs,), past_len))

    tm_p = TILES.ffn_rows
    xp = x_prompt.reshape(bp * seq, D_MODEL)
    xp = _ffn(xp, mod_p, 0, seq // tm_p, norm_ffn1[l], *f1, tm=tm_p)
    tm_mix = TILES.mix_rows
    tq_p = TILES.attn_q
    q_lat_t, q_rope_t, ckv_p, kc_p, kct_p, krope_t_p, kr_p, u_p = _mixin(xp, mod_p, seq // tm_mix, seq, p, cos_p,
                                                                         sin_p, tm=tm_mix, tq=tq_p)
    attn_p = _prompt_attn(q_lat_t, q_rope_t, kc_p, kr_p, kct_p, p["w_uv_h"], p["norm_attn_out_col"], tq=tq_p,
                          tk=TILES.attn_k)
    y_p, hre_p, him_p = _s5_prompt(u_p, ops)
    xp = _mixout(xp, attn_p.reshape(bp * seq, MLA_WIDTH), y_p.reshape(bp * seq, S5_WIDTH), mod_p, seq // tm_mix, p,
                 tm=tm_mix)
    y_prompt = _ffn(xp, mod_p, 6, seq // tm_p, norm_ffn2[l], *f2, norm_final, tm=tm_p).reshape(bp, seq, D_MODEL)

    xs = x_sample.reshape(bs, D_MODEL)
    xs = _ffn(xs, mod_s, 0, 1, norm_ffn1[l], *f1, tm=bs)
    q_lat_ts, q_rope_ts, ckv_s, _, _, krope_t_s, _, u_s = _mixin(xs, mod_s, 1, bs, p, cos_s, sin_s, tm=bs, tq=bs)
    krope_s = krope_t_s[0].T
    q_lat_s = q_lat_ts.reshape(KV_LORA, MLA_HEADS, bs).transpose(2, 1, 0)
    q_rope_s = q_rope_ts.reshape(QK_ROPE, MLA_HEADS, bs).transpose(2, 1, 0)
    o_lat = _sample_attn(page_table, q_lat_s, q_rope_s,
                         ckv_s.reshape(bs, 1, KV_LORA), krope_s.reshape(bs, 1, QK_ROPE), cache_ckv[l:l + 1],
                         cache_krope[l:l + 1].transpose(0, 1, 3, 2))
    attn_s = _sample_attn_out(o_lat.transpose(1, 0, 2), p["w_uv_h"], p["norm_attn_out"])
    y_s, hre_s, him_s = _s5_sample(u_s.reshape(bs, S5_WIDTH), state_s5_re[l], state_s5_im[l], ops)
    xs = _mixout(xs, attn_s, y_s, mod_s, 1, p, tm=bs)
    y_sample = _ffn(xs, mod_s, 6, 1, norm_ffn2[l], *f2, norm_final, tm=bs).reshape(bs, 1, D_MODEL)

    return (y_prompt, y_sample,
            ckv_p[None], krope_t_p.transpose(0, 2, 1)[None],
            ckv_s.reshape(1, bs, 1, KV_LORA), krope_s.reshape(1, bs, 1, QK_ROPE),
            hre_p[None], him_p[None], hre_s[None], him_s[None])
```

```python
import functools
from typing import NamedTuple

import jax
import jax.numpy as jnp
from jax import lax
from jax.experimental import pallas as pl
from jax.experimental.pallas import tpu as pltpu

F32 = jnp.float32
BF16 = jnp.bfloat16

D_MODEL = 2048
D_FF = 5632
MLA_HEADS = 8
QK_NOPE = 128
QK_ROPE = 64
V_HEAD = 128
Q_LORA = 512
KV_LORA = 256
MLA_WIDTH = 1024
S5_WIDTH = 1024
S5_GROUP = 16
S5_GROUPS = 64
S5_STATE = 64
ADA_CHUNKS = 9
PAGE = 128
ROPE_THETA = 10000.0
SOFTMAX_SCALE = (QK_NOPE + QK_ROPE) ** -0.5
EPS = 1e-6

LANES = 128
S5_CHUNK = 16
GROUPS_PER_SLAB = LANES // S5_GROUP
N_SLABS = S5_WIDTH // LANES
SLAB_STATE = GROUPS_PER_SLAB * S5_STATE
CHUNK_COLS = S5_CHUNK * LANES
MXU_TILE = 2 * LANES
VMEM_LIMIT = 56 * 1024 * 1024


class _Tiles(NamedTuple):
    ada_cols: int = 1024
    ffn_rows: int = 512
    ffn_cols: int = 512
    mix_rows: int = 512
    attn_q: int = 512
    attn_k: int = 512
    pages: int = 16
    streams: int = 2
    slots: int = 3


TILES = _Tiles()


def _cparams(sem):
    return pltpu.CompilerParams(dimension_semantics=sem, vmem_limit_bytes=VMEM_LIMIT)


def _rms(x, g):
    return x * lax.rsqrt(jnp.mean(x * x, axis=-1, keepdims=True) + EPS) * g


def _rms_rows(x, g):
    return x * lax.rsqrt(jnp.mean(x * x, axis=0, keepdims=True) + EPS) * g


def _dot(a, b):
    return jnp.dot(a, b, preferred_element_type=F32)


def _dot_nt(a, b):
    return lax.dot_general(a, b, (((1,), (1,)), ((), ())), preferred_element_type=F32)


def _ada_kernel(c_ref, w_ref, b_ref, o_ref):
    c = c_ref[...]
    a = (c * jax.nn.sigmoid(c)).astype(BF16)
    o_ref[...] = _dot(a, w_ref[...].astype(BF16)) + b_ref[...]


def _ada(c, w_ada, b_ada):
    rows = c.shape[0]
    n = w_ada.shape[1]
    tn = TILES.ada_cols
    return pl.pallas_call(
        _ada_kernel,
        grid=(n // tn,),
        in_specs=[
            pl.BlockSpec((rows, D_MODEL), lambda j: (0, 0)),
            pl.BlockSpec((D_MODEL, tn), lambda j: (0, j)),
            pl.BlockSpec((1, tn), lambda j: (0, j)),
        ],
        out_specs=pl.BlockSpec((rows, tn), lambda j: (0, j)),
        out_shape=jax.ShapeDtypeStruct((rows, n), F32),
        compiler_params=_cparams(("arbitrary",)),
        name="ada_modulation",
    )(c, w_ada, b_ada.reshape(1, n))


def _ffn_kernel(x_ref, sh_ref, sc_ref, gt_ref, g_ref, w1_ref, w3_ref, w2_ref, w2_last_ref, *rest, final_norm):
    if final_norm:
        gf_ref, o_ref, h_ref, act_ref = rest
    else:
        o_ref, h_ref, act_ref = rest
    j = pl.program_id(1)
    last = pl.num_programs(1) - 1

    def gate():
        h = h_ref[...]
        a = _dot(h, w1_ref[...])
        b = _dot(h, w3_ref[...])
        act_ref[...] = (a * jax.nn.sigmoid(a) * b).astype(BF16)

    def down():
        return _dot(act_ref[...], w2_ref[...])

    @pl.when(j == 0)
    def _():
        h = _rms(x_ref[...], g_ref[...]) * (1.0 + sc_ref[0]) + sh_ref[0]
        h_ref[...] = h.astype(BF16)
        gate()

    @pl.when(j == 1)
    def _():
        o_ref[...] = down()
        gate()

    @pl.when((j > 1) & (j < last))
    def _():
        o_ref[...] += down()
        gate()

    @pl.when(j == last)
    def _():
        prev = down()
        gate()
        y = x_ref[...] + 0.5 * gt_ref[0] * (o_ref[...] + prev + _dot(act_ref[...], w2_last_ref[...]))
        if final_norm:
            y = _rms(y, gf_ref[...])
        o_ref[...] = y


def _ffn_cast_kernel(x_ref, sh_ref, sc_ref, gt_ref, g_ref, w1_ref, w3_ref, w2_ref, *rest, final_norm):
    if final_norm:
        gf_ref, o_ref, w1b_ref, w3b_ref, w2b_ref, h_ref = rest
    else:
        o_ref, w1b_ref, w3b_ref, w2b_ref, h_ref = rest
    j = pl.program_id(0)

    @pl.when(j == 0)
    def _():
        h = _rms(x_ref[...], g_ref[...]) * (1.0 + sc_ref[0]) + sh_ref[0]
        h_ref[...] = h.astype(BF16)
        o_ref[...] = jnp.zeros_like(o_ref)

    w1b = w1_ref[...].astype(BF16)
    w3b = w3_ref[...].astype(BF16)
    w2b = w2_ref[...].astype(BF16)
    w1b_ref[...] = w1b
    w3b_ref[...] = w3b
    w2b_ref[...] = w2b
    h = h_ref[...]
    a = _dot(h, w1b)
    act = (a * jax.nn.sigmoid(a) * _dot(h, w3b)).astype(BF16)
    o_ref[...] += _dot(act, w2b)

    @pl.when(j == pl.num_programs(0) - 1)
    def _():
        y = x_ref[...] + 0.5 * gt_ref[0] * o_ref[...]
        if final_norm:
            y = _rms(y, gf_ref[...])
        o_ref[...] = y


def _ffn_cast(x, mod, chunk0, g, w1, w3, w2, gf=None, *, tf=TILES.ffn_cols):
    rows = x.shape[0]
    const = lambda shape: pl.BlockSpec(shape, lambda j: (0,) * len(shape))
    mod_spec = lambda k: pl.BlockSpec((1, rows, D_MODEL), lambda j: (0, 0, k))
    up = pl.BlockSpec((D_MODEL, tf), lambda j: (0, j))
    down = pl.BlockSpec((tf, D_MODEL), lambda j: (j, 0))
    in_specs = [const((rows, D_MODEL)), mod_spec(chunk0), mod_spec(chunk0 + 1), mod_spec(chunk0 + 2),
                const((1, D_MODEL)), up, up, down]
    args = [x, mod, mod, mod, g.reshape(1, D_MODEL), w1, w3, w2]
    if gf is not None:
        in_specs.append(const((1, D_MODEL)))
        args.append(gf.reshape(1, D_MODEL))
    return pl.pallas_call(
        functools.partial(_ffn_cast_kernel, final_norm=gf is not None),
        grid=(D_FF // tf,),
        in_specs=in_specs,
        out_specs=[const((rows, D_MODEL)), up, up, down],
        out_shape=[jax.ShapeDtypeStruct((rows, D_MODEL), F32), jax.ShapeDtypeStruct(w1.shape, BF16),
                   jax.ShapeDtypeStruct(w3.shape, BF16), jax.ShapeDtypeStruct(w2.shape, BF16)],
        scratch_shapes=[pltpu.VMEM((rows, D_MODEL), BF16)],
        compiler_params=_cparams(("arbitrary",)),
        name="sample_ffn_cast",
    )(*args)


def _ffn(x, mod, chunk0, tiles_per_b, g, w1, w3, w2, gf=None, *, tm, tf=TILES.ffn_cols):
    n = x.shape[0]
    r = mod.shape[1]
    n_ff = D_FF // tf
    assert n_ff >= 3

    def mod_spec(k):
        return pl.BlockSpec((1, r, D_MODEL), lambda i, j: (i // tiles_per_b, 0, k))

    in_specs = [
        pl.BlockSpec((tm, D_MODEL), lambda i, j: (i, 0)),
        mod_spec(chunk0), mod_spec(chunk0 + 1), mod_spec(chunk0 + 2),
        pl.BlockSpec((1, D_MODEL), lambda i, j: (0, 0)),
        pl.BlockSpec((D_MODEL, tf), lambda i, j: (0, j)),
        pl.BlockSpec((D_MODEL, tf), lambda i, j: (0, j)),
        pl.BlockSpec((tf, D_MODEL), lambda i, j: (jnp.maximum(j - 1, 0), 0)),
        pl.BlockSpec((tf, D_MODEL), lambda i, j: (n_ff - 1, 0)),
    ]
    args = [x, mod, mod, mod, g.reshape(1, D_MODEL), w1, w3, w2, w2]
    if gf is not None:
        in_specs.append(pl.BlockSpec((1, D_MODEL), lambda i, j: (0, 0)))
        args.append(gf.reshape(1, D_MODEL))
    return pl.pallas_call(
        functools.partial(_ffn_kernel, final_norm=gf is not None),
        grid=(n // tm, n_ff),
        in_specs=in_specs,
        out_specs=pl.BlockSpec((tm, D_MODEL), lambda i, j: (i, 0)),
        out_shape=jax.ShapeDtypeStruct((n, D_MODEL), F32),
        scratch_shapes=[pltpu.VMEM((tm, D_MODEL), BF16), pltpu.VMEM((tm, tf), BF16)],
        compiler_params=_cparams(("parallel", "arbitrary")),
        name="macaron_ffn",
    )(*args)


def _rope_rows(x, cos, sin):
    half = QK_ROPE // 2
    x1 = x[:half]
    x2 = x[half:]
    return x1 * cos - x2 * sin, x1 * sin + x2 * cos


def _mixin_kernel(x_ref, sh_ref, sc_ref, g_ref, wq_ref, wkv_ref, wkr_ref, wu_ref, gq_ref, wqn_ref, wqr_ref,
                  wuk_ref, gkv_ref, cos_ref, sin_ref,
                  qlat_ref, qrope_ref, ckv_ref, kc_ref, kct_ref, krope_ref, kr_ref, u_ref, *, tq):
    half = QK_ROPE // 2
    n_q = x_ref.shape[0] // tq
    h = (_rms(x_ref[...], g_ref[...]) * (1.0 + sc_ref[0]) + sh_ref[0]).astype(BF16)
    cos = cos_ref[...]
    sin = sin_ref[...]

    u_ref[0] = _dot(h, wu_ref[...])

    ckv = _rms(_dot(h, wkv_ref[...]), gkv_ref[...])
    ckv_ref[0] = ckv
    kc_ref[0] = ckv.astype(BF16)
    kct_ref[0] = ckv.T.astype(BF16)

    k1, k2 = _rope_rows(_dot_nt(wkr_ref[...], h), cos, sin)
    kr_t = jnp.concatenate([k1, k2], axis=0)
    krope_ref[0] = kr_t
    kr_ref[0] = kr_t.T.astype(BF16)

    qn = _rms(_dot(h, wq_ref[...]), gq_ref[...]).astype(BF16)
    q_nope = (_dot(qn, wqn_ref[...]) * SOFTMAX_SCALE).astype(BF16)
    q_rope_t = _dot_nt(wqr_ref[...], qn) * SOFTMAX_SCALE
    for hd in range(MLA_HEADS):
        ql_t = _dot_nt(wuk_ref[hd], q_nope[:, hd * QK_NOPE:(hd + 1) * QK_NOPE]).astype(BF16)
        r1, r2 = _rope_rows(q_rope_t[hd * QK_ROPE:(hd + 1) * QK_ROPE], cos, sin)
        r1 = r1.astype(BF16)
        r2 = r2.astype(BF16)
        for qq in range(n_q):
            src = slice(qq * tq, (qq + 1) * tq)
            dst = slice(hd * tq, (hd + 1) * tq)
            qlat_ref[0, qq, :, dst] = ql_t[:, src]
            qrope_ref[0, qq, :half, dst] = r1[:, src]
            qrope_ref[0, qq, half:, dst] = r2[:, src]


def _mixin(x, mod, tiles_per_b, t_len, p, cos_t, sin_t, *, tm, tq):
    n = x.shape[0]
    nb = n // t_len
    r = mod.shape[1]
    half = QK_ROPE // 2
    full = lambda shape: pl.BlockSpec(shape, lambda i: (0,) * len(shape))
    rows = lambda width: pl.BlockSpec((1, tm, width), lambda i: (i // tiles_per_b, i % tiles_per_b, 0))
    cols = lambda height: pl.BlockSpec((1, height, tm), lambda i: (i // tiles_per_b, 0, i % tiles_per_b))
    head_cols = lambda height: pl.BlockSpec((1, tm // tq, height, MLA_HEADS * tq),
                                            lambda i: (i // tiles_per_b, i % tiles_per_b, 0, 0))
    in_specs = [
        pl.BlockSpec((tm, D_MODEL), lambda i: (i, 0)),
        pl.BlockSpec((1, r, D_MODEL), lambda i: (i // tiles_per_b, 0, 3)),
        pl.BlockSpec((1, r, D_MODEL), lambda i: (i // tiles_per_b, 0, 4)),
        full((1, D_MODEL)),
        full((D_MODEL, Q_LORA)), full((D_MODEL, KV_LORA)), full((QK_ROPE, D_MODEL)), full((D_MODEL, S5_WIDTH)),
        full((1, Q_LORA)), full((Q_LORA, MLA_HEADS * QK_NOPE)), full((MLA_HEADS * QK_ROPE, Q_LORA)),
        full((MLA_HEADS, KV_LORA, QK_NOPE)), full((1, KV_LORA)),
        pl.BlockSpec((half, tm), lambda i: (0, i % tiles_per_b)),
        pl.BlockSpec((half, tm), lambda i: (0, i % tiles_per_b)),
    ]
    out_specs = [head_cols(KV_LORA), head_cols(QK_ROPE), rows(KV_LORA), rows(KV_LORA), cols(KV_LORA),
                 cols(QK_ROPE), rows(QK_ROPE), rows(S5_WIDTH)]
    out_shape = [
        jax.ShapeDtypeStruct((nb, t_len // tq, KV_LORA, MLA_HEADS * tq), BF16),
        jax.ShapeDtypeStruct((nb, t_len // tq, QK_ROPE, MLA_HEADS * tq), BF16),
        jax.ShapeDtypeStruct((nb, t_len, KV_LORA), F32),
        jax.ShapeDtypeStruct((nb, t_len, KV_LORA), BF16),
        jax.ShapeDtypeStruct((nb, KV_LORA, t_len), BF16),
        jax.ShapeDtypeStruct((nb, QK_ROPE, t_len), F32),
        jax.ShapeDtypeStruct((nb, t_len, QK_ROPE), BF16),
        jax.ShapeDtypeStruct((nb, t_len, S5_WIDTH), F32),
    ]
    return pl.pallas_call(
        functools.partial(_mixin_kernel, tq=tq),
        grid=(n // tm,),
        in_specs=in_specs,
        out_specs=out_specs,
        out_shape=out_shape,
        compiler_params=_cparams(("parallel",)),
        name="mixer_in",
    )(x, mod, mod, p["norm_mix"], p["w_q"], p["w_kv"], p["w_kr_t"], p["w_u"], p["norm_q"], p["w_uq_nope"],
      p["w_uq_rope_t"], p["w_uk_h"], p["norm_kv"], cos_t, sin_t)


def _prompt_attn_kernel(ql_ref, qr_ref, kc_ref, kr_ref, kct_ref, wuv_ref, g_ref, o_ref, *scratch, tq, tk):
    qi = pl.program_id(1)
    n_half = len(scratch) // 3
    m_ref, l_ref, acc_ref = scratch[:n_half], scratch[n_half:2 * n_half], scratch[2 * n_half:]
    width = m_ref[0].shape[1]
    heads_per_half = width // tq
    for hf in range(n_half):
        m_ref[hf][...] = jnp.full_like(m_ref[hf], -jnp.inf)
        l_ref[hf][...] = jnp.zeros_like(l_ref[hf])
        acc_ref[hf][...] = jnp.zeros_like(acc_ref[hf])

    def step(ki, diag):
        start = pl.multiple_of(ki * tk, tk)
        kc = kc_ref[0, pl.ds(start, tk), :]
        kr = kr_ref[0, pl.ds(start, tk), :]
        kct = kct_ref[0, :, pl.ds(start, tk)]
        masked = diag is not None
        if masked:
            k_local = lax.broadcasted_iota(jnp.int32, (tk, width), 0) + diag * tk
            t_local = lax.broadcasted_iota(jnp.int32, (tk, width), 1) & (tq - 1)
            keep = k_local <= t_local
        for hf in range(n_half):
            lanes = slice(hf * width, (hf + 1) * width)
            s = _dot(kc, ql_ref[0, 0, :, lanes]) + _dot(kr, qr_ref[0, 0, :, lanes])
            if masked:
                s = jnp.where(keep, s, -jnp.inf)
            m_old = m_ref[hf][...]
            m_new = jnp.maximum(m_old, jnp.max(s, axis=0, keepdims=True))
            alpha = jnp.exp(m_old - m_new)
            pexp = jnp.exp(s - m_new)
            l_ref[hf][...] = alpha * l_ref[hf][...] + jnp.sum(pexp, axis=0, keepdims=True)
            acc_ref[hf][...] = alpha * acc_ref[hf][...] + _dot(kct, pexp.astype(BF16))
            m_ref[hf][...] = m_new

    def body(ki, carry):
        step(ki, None)
        return carry

    blocks_per_tile = tq // tk
    lax.fori_loop(0, qi * blocks_per_tile, body, 0)
    for d in range(blocks_per_tile):
        step(qi * blocks_per_tile + d, d)
    parts = []
    for hd in range(MLA_HEADS):
        hf = hd // heads_per_half
        lanes = slice((hd % heads_per_half) * tq, (hd % heads_per_half + 1) * tq)
        o_t = acc_ref[hf][:, lanes] / l_ref[hf][:, lanes]
        parts.append(_dot(wuv_ref[hd], o_t.astype(BF16)))
    attn_t = _rms_rows(jnp.concatenate(parts, axis=0), g_ref[...])
    o_ref[0] = attn_t.T.astype(BF16)


def _prompt_attn(q_lat_t, q_rope_t, kc, kr, kc_t, w_uv_h, g_col, *, tq, tk, n_half=1):
    nb, n_q, _, _ = q_lat_t.shape
    t_len = n_q * tq
    width = MLA_HEADS * tq // n_half
    assert tq % tk == 0
    return pl.pallas_call(
        functools.partial(_prompt_attn_kernel, tq=tq, tk=tk),
        grid=(nb, n_q),
        in_specs=[
            pl.BlockSpec((1, 1, KV_LORA, MLA_HEADS * tq), lambda b, i: (b, i, 0, 0)),
            pl.BlockSpec((1, 1, QK_ROPE, MLA_HEADS * tq), lambda b, i: (b, i, 0, 0)),
            pl.BlockSpec((1, t_len, KV_LORA), lambda b, i: (b, 0, 0)),
            pl.BlockSpec((1, t_len, QK_ROPE), lambda b, i: (b, 0, 0)),
            pl.BlockSpec((1, KV_LORA, t_len), lambda b, i: (b, 0, 0)),
            pl.BlockSpec((MLA_HEADS, V_HEAD, KV_LORA), lambda b, i: (0, 0, 0)),
            pl.BlockSpec((MLA_WIDTH, 1), lambda b, i: (0, 0)),
        ],
        out_specs=pl.BlockSpec((1, tq, MLA_WIDTH), lambda b, i: (b, i, 0)),
        out_shape=jax.ShapeDtypeStruct((nb, t_len, MLA_WIDTH), BF16),
        scratch_shapes=([pltpu.VMEM((1, width), F32)] * (2 * n_half) + [pltpu.VMEM((KV_LORA, width), F32)] * n_half),
        compiler_params=_cparams(("parallel", "parallel")),
        name="prompt_attention",
    )(q_lat_t, q_rope_t, kc, kr, kc_t, w_uv_h, g_col)


def _sample_attn_kernel(pt_ref, ql_ref, qr_ref, kcn_ref, krn_ref, ckv_hbm, kr_hbm, o_ref, kbuf, rbuf, sem_k, sem_r,
                        *, pages, streams):
    nb, n_pages = pt_ref.shape
    n_chunks = n_pages // pages
    n_groups = nb // streams
    total = n_groups * n_chunks
    n_slots = kbuf.shape[0]
    ahead = n_slots - 1

    def chunk_copies(g, slot):
        grp = g // n_chunks
        c = g % n_chunks
        copies = []
        for st in range(streams):
            b = grp * streams + st
            for i in range(pages):
                page = pt_ref[b, c * pages + i]
                copies.append(pltpu.make_async_copy(
                    ckv_hbm.at[0, page], kbuf.at[slot, st, pl.ds(i * PAGE, PAGE), :], sem_k.at[slot, st]))
                copies.append(pltpu.make_async_copy(
                    kr_hbm.at[0, page], rbuf.at[slot, st, i], sem_r.at[slot, st]))
        return copies

    grp = pl.program_id(0)

    @pl.when(grp == 0)
    def _():
        for g0 in range(ahead):
            for cp in chunk_copies(g0, g0):
                cp.start()

    qs = [(ql_ref[st], qr_ref[st]) for st in range(streams)]

    def chunk_body(c, carry):
        g = grp * n_chunks + c
        slot = g % n_slots

        @pl.when(g + ahead < total)
        def _():
            for cp in chunk_copies(g + ahead, (g + ahead) % n_slots):
                cp.start()

        for cp in chunk_copies(g, slot):
            cp.wait()
        out = []
        for st in range(streams):
            m_old, l_old, acc_old = carry[st]
            ql, qr = qs[st]
            kc = kbuf[slot, st].astype(BF16)
            kr_t = jnp.concatenate([rbuf[slot, st, i].astype(BF16) for i in range(pages)], axis=1)
            s = _dot_nt(ql, kc) + _dot(qr, kr_t)
            m_new = jnp.maximum(m_old, jnp.max(s, axis=1, keepdims=True))
            alpha = jnp.exp(m_old - m_new)
            pexp = jnp.exp(s - m_new)
            l_new = alpha * l_old + jnp.sum(pexp, axis=1, keepdims=True)
            acc_new = alpha * acc_old + _dot(pexp.astype(BF16), kc)
            out.append((m_new, l_new, acc_new))
        return tuple(out)

    init = tuple((jnp.full((MLA_HEADS, 1), -jnp.inf, F32), jnp.zeros((MLA_HEADS, 1), F32),
                  jnp.zeros((MLA_HEADS, KV_LORA), F32)) for _ in range(streams))
    final = lax.fori_loop(0, n_chunks, chunk_body, init)
    for st in range(streams):
        m_old, l_old, acc_old = final[st]
        ql, qr = qs[st]
        kcn = kcn_ref[st].astype(BF16).astype(F32)
        krn = krn_ref[st].astype(BF16).astype(F32)
        s_n = (jnp.sum(ql.astype(F32) * kcn, axis=1, keepdims=True)
               + jnp.sum(qr.astype(F32) * krn, axis=1, keepdims=True))
        m_f = jnp.maximum(m_old, s_n)
        a_f = jnp.exp(m_old - m_f)
        p_n = jnp.exp(s_n - m_f)
        l_f = a_f * l_old + p_n
        acc_f = a_f * acc_old + p_n.astype(BF16).astype(F32) * kcn
        o_ref[st] = acc_f / l_f


def _sample_attn(page_table, q_lat, q_rope, ckv_new, krope_new, cache_ckv, cache_krope_t, *, pages=TILES.pages,
                 streams=TILES.streams, slots=TILES.slots):
    nb, n_pages = page_table.shape
    assert n_pages % pages == 0 and nb % streams == 0 and (nb // streams) * (n_pages // pages) >= slots
    per_group = lambda rows, width: pl.BlockSpec((streams, rows, width), lambda g, pt: (g, 0, 0))
    grid_spec = pltpu.PrefetchScalarGridSpec(
        num_scalar_prefetch=1,
        grid=(nb // streams,),
        in_specs=[
            per_group(MLA_HEADS, KV_LORA), per_group(MLA_HEADS, QK_ROPE),
            per_group(1, KV_LORA), per_group(1, QK_ROPE),
            pl.BlockSpec(memory_space=pl.ANY), pl.BlockSpec(memory_space=pl.ANY),
        ],
        out_specs=per_group(MLA_HEADS, KV_LORA),
        scratch_shapes=[
            pltpu.VMEM((slots, streams, pages * PAGE, KV_LORA), F32),
            pltpu.VMEM((slots, streams, pages, QK_ROPE, PAGE), F32),
            pltpu.SemaphoreType.DMA((slots, streams)),
            pltpu.SemaphoreType.DMA((slots, streams)),
        ],
    )
    return pl.pallas_call(
        functools.partial(_sample_attn_kernel, pages=pages, streams=streams),
        grid_spec=grid_spec,
        out_shape=jax.ShapeDtypeStruct((nb, MLA_HEADS, KV_LORA), F32),
        compiler_params=_cparams(("arbitrary",)),
        name="sample_paged_attention",
    )(page_table, q_lat, q_rope, ckv_new, krope_new, cache_ckv, cache_krope_t)


def _sample_attn_out_kernel(o_ref, wuv_ref, g_ref, out_ref):
    parts = [_dot_nt(o_ref[hd].astype(BF16), wuv_ref[hd]) for hd in range(MLA_HEADS)]
    out_ref[...] = _rms(jnp.concatenate(parts, axis=1), g_ref[...]).astype(BF16)


def _sample_attn_out(o_lat_h, w_uv_h, g):
    nb = o_lat_h.shape[1]
    return pl.pallas_call(
        _sample_attn_out_kernel,
        out_shape=jax.ShapeDtypeStruct((nb, MLA_WIDTH), BF16),
        name="sample_attention_out",
    )(o_lat_h, w_uv_h, g)


def _same_group(shape, row_shift, col_shift):
    rows = lax.broadcasted_iota(jnp.int32, shape, 0)
    cols = lax.broadcasted_iota(jnp.int32, shape, 1)
    mask = GROUPS_PER_SLAB - 1
    return ((rows >> row_shift) & mask) == ((cols >> col_shift) & mask)


def _s5_prompt_kernel(u_ref, wr_ref, we_ref, wf_ref, tr_ref, te_ref, tf_ref, are_ref, aim_ref, d_ref,
                      y_ref, hre_ref, him_ref, r_ref, e_ref, f_ref, uc_ref, s_ref, hp_ref, yc_ref):
    n_chunks = uc_ref.shape[0]
    n_blk = CHUNK_COLS // MXU_TILE
    ch_bits = S5_GROUP.bit_length() - 1
    st_bits = S5_STATE.bit_length() - 1

    @pl.when(pl.program_id(1) == 0)
    def _():
        zero = jnp.zeros((), F32)
        r_ref[...] = jnp.where(_same_group(r_ref.shape, ch_bits, ch_bits),
                               _dot(wr_ref[0], tr_ref[...]), zero).astype(BF16)
        e_ref[...] = jnp.where(_same_group(e_ref.shape, ch_bits, st_bits),
                               _dot(we_ref[0], te_ref[...]), zero).astype(BF16)
        f_ref[...] = jnp.where(_same_group(f_ref.shape, st_bits, ch_bits),
                               _dot(tf_ref[...], wf_ref[0]), zero).astype(BF16)

    for s in range(S5_CHUNK):
        uc_ref[:, s * LANES:(s + 1) * LANES] = u_ref[0, pl.ds(s, n_chunks, stride=S5_CHUNK), :]
    uc = uc_ref[...]
    ub = uc.astype(BF16)

    s_ref[...] = _dot(ub, e_ref[...])

    a_re = are_ref[0]
    a_im = aim_ref[0]

    def scan(k, carry):
        h_re, h_im = carry
        hp_ref[pl.ds(k, 1), :] = jnp.concatenate([h_re, h_im], axis=1)
        row = s_ref[pl.ds(k, 1), :]
        n_re = a_re * h_re - a_im * h_im + row[:, :SLAB_STATE]
        n_im = a_re * h_im + a_im * h_re + row[:, SLAB_STATE:]
        return n_re, n_im

    zero_state = jnp.zeros((1, SLAB_STATE), F32)
    h_re, h_im = lax.fori_loop(0, n_chunks, scan, (zero_state, zero_state), unroll=8)
    hre_ref[0, 0] = h_re
    him_ref[0, 0] = h_im

    yc_ref[...] = _dot(hp_ref[...].astype(BF16), f_ref[...]) + uc * d_ref[0]
    for tb in range(n_blk):
        yc_ref[:, tb * MXU_TILE:(tb + 1) * MXU_TILE] += _dot(ub[:, :(tb + 1) * MXU_TILE],
                                                             r_ref[(n_blk - 1 - tb) * MXU_TILE:, :])
    y = jax.nn.gelu(yc_ref[...])
    for t in range(S5_CHUNK):
        y_ref[0, pl.ds(t, n_chunks, stride=S5_CHUNK), :] = y[:, t * LANES:(t + 1) * LANES]


def _s5_prompt(u, ops):
    nb, t_len, _ = u.shape
    n_chunks = t_len // S5_CHUNK
    slab = lambda shape: pl.BlockSpec((1,) + shape, lambda j, b: (j,) + (0,) * len(shape))
    full = lambda shape: pl.BlockSpec(shape, lambda j, b: (0,) * len(shape))
    y, hre, him = pl.pallas_call(
        _s5_prompt_kernel,
        grid=(N_SLABS, nb),
        in_specs=[
            pl.BlockSpec((1, t_len, LANES), lambda j, b: (b, 0, j)),
            slab((CHUNK_COLS, 2 * S5_GROUP)), slab((CHUNK_COLS, 2 * S5_STATE)), slab((2 * S5_STATE, CHUNK_COLS)),
            full((2 * S5_GROUP, MXU_TILE)), full((2 * S5_STATE, 2 * SLAB_STATE)), full((2 * SLAB_STATE, 2 * S5_STATE)),
            slab((1, SLAB_STATE)), slab((1, SLAB_STATE)), slab((1, CHUNK_COLS)),
        ],
        out_specs=[
            pl.BlockSpec((1, t_len, LANES), lambda j, b: (b, 0, j)),
            pl.BlockSpec((1, 1, 1, SLAB_STATE), lambda j, b: (j, b, 0, 0)),
            pl.BlockSpec((1, 1, 1, SLAB_STATE), lambda j, b: (j, b, 0, 0)),
        ],
        out_shape=[
            jax.ShapeDtypeStruct((nb, t_len, S5_WIDTH), F32),
            jax.ShapeDtypeStruct((N_SLABS, nb, 1, SLAB_STATE), F32),
            jax.ShapeDtypeStruct((N_SLABS, nb, 1, SLAB_STATE), F32),
        ],
        scratch_shapes=[
            pltpu.VMEM((CHUNK_COLS, MXU_TILE), BF16),
            pltpu.VMEM((CHUNK_COLS, 2 * SLAB_STATE), BF16),
            pltpu.VMEM((2 * SLAB_STATE, CHUNK_COLS), BF16),
            pltpu.VMEM((n_chunks, CHUNK_COLS), F32),
            pltpu.VMEM((n_chunks, 2 * SLAB_STATE), F32),
            pltpu.VMEM((n_chunks, 2 * SLAB_STATE), F32),
            pltpu.VMEM((n_chunks, CHUNK_COLS), F32),
        ],
        compiler_params=_cparams(("arbitrary", "arbitrary")),
        name="s5_prompt",
    )(u, ops["wr"], ops["we"], ops["wf"], ops["tile_r"], ops["tile_e"], ops["tile_f"], ops["a_re"], ops["a_im"],
      ops["d_chunk"])

    def to_state(h):
        return h.reshape(N_SLABS, nb, GROUPS_PER_SLAB, S5_STATE).transpose(1, 0, 2, 3).reshape(nb, S5_GROUPS, S5_STATE)

    return y, to_state(hre), to_state(him)


def _split_bf16(x):
    hi = x.astype(BF16)
    return hi, (x - hi.astype(F32)).astype(BF16)


def _s5_sample_kernel(u_ref, h0re_ref, h0im_ref, bd_hi_ref, bd_lo_ref, cd_ref, lre_ref, lim_ref, d_ref,
                      y_ref, hre_ref, him_ref):
    u = u_ref[...]
    u_hi, u_lo = _split_bf16(u)
    bu = _dot(u_hi, bd_hi_ref[0]) + (_dot(u_hi, bd_lo_ref[0]) + _dot(u_lo, bd_hi_ref[0]))
    l_re = lre_ref[0]
    l_im = lim_ref[0]
    h0_re = h0re_ref[...]
    h0_im = h0im_ref[...]
    h_re = l_re * h0_re - l_im * h0_im + bu[:, :SLAB_STATE]
    h_im = l_re * h0_im + l_im * h0_re + bu[:, SLAB_STATE:]
    hre_ref[...] = h_re
    him_ref[...] = h_im
    h = jnp.concatenate([h_re, h_im], axis=1).astype(BF16)
    y_ref[...] = jax.nn.gelu(_dot(h, cd_ref[0]) + u * d_ref[0])


def _s5_sample(u, h0_re, h0_im, ops):
    nb = u.shape[0]
    n_state = S5_GROUPS * S5_STATE
    slab = lambda shape: pl.BlockSpec((1,) + shape, lambda j: (j,) + (0,) * len(shape))
    col = lambda width: pl.BlockSpec((nb, width), lambda j: (0, j))
    y, hre, him = pl.pallas_call(
        _s5_sample_kernel,
        grid=(N_SLABS,),
        in_specs=[
            col(LANES), col(SLAB_STATE), col(SLAB_STATE),
            slab((LANES, 2 * SLAB_STATE)), slab((LANES, 2 * SLAB_STATE)), slab((2 * SLAB_STATE, LANES)),
            slab((1, SLAB_STATE)), slab((1, SLAB_STATE)), slab((1, LANES)),
        ],
        out_specs=[col(LANES), col(SLAB_STATE), col(SLAB_STATE)],
        out_shape=[
            jax.ShapeDtypeStruct((nb, S5_WIDTH), F32),
            jax.ShapeDtypeStruct((nb, n_state), F32),
            jax.ShapeDtypeStruct((nb, n_state), F32),
        ],
        compiler_params=_cparams(("parallel",)),
        name="s5_sample",
    )(u, h0_re.reshape(nb, n_state), h0_im.reshape(nb, n_state), ops["bd_hi"], ops["bd_lo"], ops["cd"],
      ops["l_re"], ops["l_im"], ops["d_slab"])
    return y, hre.reshape(nb, S5_GROUPS, S5_STATE), him.reshape(nb, S5_GROUPS, S5_STATE)


def _mixout_kernel(x_ref, attn_ref, y_ref, gt_ref, wglu_ref, bglu_ref, gs_ref, woa_ref, woy_ref, o_ref):
    y = y_ref[...]
    z = _dot(y.astype(BF16), wglu_ref[...]) + bglu_ref[...]
    yn = _rms(y * jax.nn.sigmoid(z), gs_ref[...]).astype(BF16)
    mix = _dot(attn_ref[...], woa_ref[...]) + _dot(yn, woy_ref[...])
    o_ref[...] = x_ref[...] + gt_ref[0] * mix


def _mixout(x, attn, y, mod, tiles_per_b, p, *, tm):
    n = x.shape[0]
    r = mod.shape[1]
    full = lambda shape: pl.BlockSpec(shape, lambda i: (0,) * len(shape))
    return pl.pallas_call(
        _mixout_kernel,
        grid=(n // tm,),
        in_specs=[
            pl.BlockSpec((tm, D_MODEL), lambda i: (i, 0)),
            pl.BlockSpec((tm, MLA_WIDTH), lambda i: (i, 0)),
            pl.BlockSpec((tm, S5_WIDTH), lambda i: (i, 0)),
            pl.BlockSpec((1, r, D_MODEL), lambda i: (i // tiles_per_b, 0, 5)),
            full((S5_WIDTH, S5_WIDTH)), full((1, S5_WIDTH)), full((1, S5_WIDTH)),
            full((MLA_WIDTH, D_MODEL)), full((S5_WIDTH, D_MODEL)),
        ],
        out_specs=pl.BlockSpec((tm, D_MODEL), lambda i: (i, 0)),
        out_shape=jax.ShapeDtypeStruct((n, D_MODEL), F32),
        compiler_params=_cparams(("parallel",)),
        name="mixer_out",
    )(x, attn, y, mod, p["w_glu"], p["b_glu"], p["norm_ssm_out"], p["w_out_attn"], p["w_out_ssm"])


def _rope_tables(pos):
    half = QK_ROPE // 2
    inv_freq = ROPE_THETA ** (-jnp.arange(half, dtype=F32) / half)
    ang = pos.astype(F32)[:, None] * inv_freq[None, :]
    return jnp.cos(ang).T, jnp.sin(ang).T


def _group_diag(x, g_axis, new_axis):
    x = jnp.expand_dims(x, new_axis)
    shape = [1] * x.ndim
    shape[g_axis if g_axis < new_axis else g_axis + 1] = GROUPS_PER_SLAB
    shape[new_axis] = GROUPS_PER_SLAB
    return x * jnp.eye(GROUPS_PER_SLAB, dtype=x.dtype).reshape(shape)


def _by_slab(x, g_axis):
    return x.reshape(x.shape[:g_axis] + (N_SLABS, GROUPS_PER_SLAB) + x.shape[g_axis + 1:])


def _replicate_over_groups(n_outer, n_inner):
    eo = jnp.eye(n_outer, dtype=F32)[:, None, :, None, None]
    ei = jnp.eye(n_inner, dtype=F32)[None, :, None, None, :]
    ones = jnp.ones((1, 1, 1, GROUPS_PER_SLAB, 1), F32)
    return (eo * ei * ones).reshape(n_outer * n_inner, n_outer * GROUPS_PER_SLAB * n_inner)


def _s5_operators(a_re, a_im, log_dt, b_re, b_im, c_re, c_im, d_skip):
    hp = lax.Precision.HIGHEST
    a_re, a_im, b_re, b_im, c_re, c_im = (v.astype(F32) for v in (a_re, a_im, b_re, b_im, c_re, c_im))
    dt = jnp.exp(log_dt.astype(F32))[:, None]
    z_re = a_re * dt
    z_im = a_im * dt

    def lam_pow(n):
        mag = jnp.exp(z_re[None] * n[:, None, None])
        ang = z_im[None] * n[:, None, None]
        return mag * jnp.cos(ang), mag * jnp.sin(ang)

    steps = jnp.arange(S5_CHUNK + 1, dtype=F32)
    pw_re, pw_im = lam_pow(steps)
    lb_re, lb_im = pw_re[1], pw_im[1]
    den = a_re * a_re + a_im * a_im
    q_re = ((lb_re - 1.0) * a_re + lb_im * a_im) / den
    q_im = (lb_im * a_re - (lb_re - 1.0) * a_im) / den
    bb_re = q_re[:, :, None] * b_re - q_im[:, :, None] * b_im
    bb_im = q_re[:, :, None] * b_im + q_im[:, :, None] * b_re

    cp_re = c_re[None] * pw_re[:S5_CHUNK, :, None, :] - c_im[None] * pw_im[:S5_CHUNK, :, None, :]
    cp_im = c_re[None] * pw_im[:S5_CHUNK, :, None, :] + c_im[None] * pw_re[:S5_CHUNK, :, None, :]
    m = jnp.einsum("ngcq,gqd->ngcd", jnp.concatenate([cp_re, -cp_im], axis=3),
                   jnp.concatenate([bb_re, bb_im], axis=1), precision=hp)
    m_ext = jnp.concatenate([m, jnp.zeros_like(m[:1])], axis=0)
    n_blk = S5_CHUNK // 2
    d_i = (n_blk - 1 - jnp.arange(n_blk))[:, None, None]
    s_i = jnp.arange(2)[None, :, None]
    t_i = jnp.arange(2)[None, None, :]
    lag = 2 * d_i + t_i - s_i
    mg = _by_slab(m_ext[lag], 3)
    wr = mg.transpose(3, 0, 1, 4, 6, 2, 5).reshape(N_SLABS, CHUNK_COLS, 2 * S5_GROUP)

    pr_re, pr_im = lam_pow(S5_CHUNK - 1 - steps[:S5_CHUNK])
    w_re = pr_re[:, :, :, None] * bb_re[None] - pr_im[:, :, :, None] * bb_im[None]
    w_im = pr_re[:, :, :, None] * bb_im[None] + pr_im[:, :, :, None] * bb_re[None]

    def e_half(v):
        return _by_slab(v, 1).transpose(1, 0, 2, 4, 3).reshape(N_SLABS, CHUNK_COLS, S5_STATE)

    we = jnp.concatenate([e_half(w_re), e_half(w_im)], axis=2)

    g_re = c_re[None] * pw_re[1:, :, None, :] - c_im[None] * pw_im[1:, :, None, :]
    g_im = c_re[None] * pw_im[1:, :, None, :] + c_im[None] * pw_re[1:, :, None, :]

    def f_half(v):
        return _by_slab(v, 1).transpose(1, 4, 0, 2, 3).reshape(N_SLABS, S5_STATE, CHUNK_COLS)

    wf = jnp.concatenate([f_half(g_re), f_half(-g_im)], axis=1)

    def slab_vec(v):
        return v.reshape(N_SLABS, 1, SLAB_STATE)

    d_slab = d_skip.astype(F32).reshape(N_SLABS, 1, LANES)
    tile_e = _replicate_over_groups(2, S5_STATE)

    def bd_half(v):
        return _group_diag(_by_slab(v, 0).transpose(0, 1, 3, 2), 1, 3).reshape(N_SLABS, LANES, SLAB_STATE)

    def cd_half(v):
        return _group_diag(_by_slab(v, 0).transpose(0, 1, 3, 2), 1, 3).reshape(N_SLABS, SLAB_STATE, LANES)

    bd = jnp.concatenate([bd_half(bb_re), bd_half(bb_im)], axis=2)
    cd = jnp.concatenate([cd_half(c_re), cd_half(-c_im)], axis=1)
    bd_hi = bd.astype(BF16)
    bd_lo = (bd - bd_hi.astype(F32)).astype(BF16)
    return {
        "wr": wr.astype(BF16), "we": we.astype(BF16), "wf": wf.astype(BF16),
        "tile_r": _replicate_over_groups(2, S5_GROUP).astype(BF16),
        "tile_e": tile_e.astype(BF16), "tile_f": tile_e.T.astype(BF16),
        "a_re": slab_vec(pw_re[S5_CHUNK]), "a_im": slab_vec(pw_im[S5_CHUNK]),
        "d_chunk": jnp.tile(d_slab, (1, 1, S5_CHUNK)), "d_slab": d_slab,
        "bd_hi": bd_hi, "bd_lo": bd_lo, "cd": cd.astype(BF16),
        "l_re": slab_vec(lb_re), "l_im": slab_vec(lb_im),
    }


def _layer_params(w_in, w_uq, w_uk, w_uv, w_glu, w_out, norm_mix, norm_q, norm_kv, norm_attn_out, norm_ssm_out,
                  b_glu):
    c0, c1, c2 = Q_LORA, Q_LORA + KV_LORA, Q_LORA + KV_LORA + QK_ROPE
    w_uq_h = w_uq.reshape(Q_LORA, MLA_HEADS, QK_NOPE + QK_ROPE)
    return {
        "w_q": w_in[:, :c0].astype(BF16),
        "w_kv": w_in[:, c0:c1].astype(BF16),
        "w_kr_t": w_in[:, c1:c2].T.astype(BF16),
        "w_u": w_in[:, c2:].astype(BF16),
        "w_uq_nope": w_uq_h[:, :, :QK_NOPE].reshape(Q_LORA, MLA_HEADS * QK_NOPE).astype(BF16),
        "w_uq_rope_t": w_uq_h[:, :, QK_NOPE:].reshape(Q_LORA, MLA_HEADS * QK_ROPE).T.astype(BF16),
        "w_uk_h": w_uk.transpose(1, 0, 2).astype(BF16),
        "w_uv_h": w_uv.transpose(1, 2, 0).astype(BF16),
        "w_glu": w_glu.astype(BF16),
        "w_out_attn": w_out[:MLA_WIDTH].astype(BF16),
        "w_out_ssm": w_out[MLA_WIDTH:].astype(BF16),
        "norm_mix": norm_mix.reshape(1, D_MODEL),
        "norm_q": norm_q.reshape(1, Q_LORA),
        "norm_kv": norm_kv.reshape(1, KV_LORA),
        "norm_attn_out": norm_attn_out.reshape(1, MLA_WIDTH),
        "norm_attn_out_col": norm_attn_out.reshape(MLA_WIDTH, 1),
        "norm_ssm_out": norm_ssm_out.reshape(1, S5_WIDTH),
        "b_glu": b_glu.reshape(1, S5_WIDTH),
    }


def kernel(x_prompt, x_sample, c_prompt, c_sample, cache_ckv, cache_krope, state_s5_re, state_s5_im, page_table, w_ada, b_ada, norm_ffn1, ffn1_w1, ffn1_w3, ffn1_w2, norm_mix, w_in, norm_q, w_uq, norm_kv, w_uk, w_uv, s5_a_re, s5_a_im, s5_log_dt, s5_b_re, s5_b_im, s5_c_re, s5_c_im, s5_d, w_glu, b_glu, norm_attn_out, norm_ssm_out, w_out, norm_ffn2, ffn2_w1, ffn2_w3, ffn2_w2, norm_final):
    bp, seq, _ = x_prompt.shape
    bs = x_sample.shape[0]
    depth = w_ada.shape[0]
    assert depth == 1 and x_sample.shape[1] == 1
    n_pages = page_table.shape[1]
    past_len = n_pages * PAGE
    l = 0

    pad = (-(bs + bp)) % 8
    c_all = jnp.concatenate([c_sample, c_prompt, jnp.zeros((pad, D_MODEL), F32)], axis=0)
    mod = _ada(c_all, w_ada[l], b_ada[l])
    mod_s = mod[:bs].reshape(1, bs, ADA_CHUNKS * D_MODEL)
    mod_p = mod[bs:bs + bp].reshape(bp, 1, ADA_CHUNKS * D_MODEL)

    p = _layer_params(w_in[l], w_uq[l], w_uk[l], w_uv[l], w_glu[l], w_out[l], norm_mix[l], norm_q[l], norm_kv[l],
                      norm_attn_out[l], norm_ssm_out[l], b_glu[l])
    ops = _s5_operators(s5_a_re[l], s5_a_im[l], s5_log_dt[l], s5_b_re[l], s5_b_im[l], s5_c_re[l], s5_c_im[l],
                        s5_d[l])
    cos_p, sin_p = _rope_tables(jnp.arange(seq))
    cos_s, sin_s = _rope_tables(jnp.full((bs,), past_len))

    xs = x_sample.reshape(bs, D_MODEL)
    xs, *f1 = _ffn_cast(xs, mod_s, 0, norm_ffn1[l], ffn1_w1[l], ffn1_w3[l], ffn1_w2[l])
    q_lat_ts, q_rope_ts, ckv_s, _, _, krope_t_s, _, u_s = _mixin(xs, mod_s, 1, bs, p, cos_s, sin_s, tm=bs, tq=bs)
    krope_s = krope_t_s[0].T
    q_lat_s = q_lat_ts.reshape(KV_LORA, MLA_HEADS, bs).transpose(2, 1, 0)
    q_rope_s = q_rope_ts.reshape(QK_ROPE, MLA_HEADS, bs).transpose(2, 1, 0)
    o_lat = _sample_attn(page_table, q_lat_s, q_rope_s,
                         ckv_s.reshape(bs, 1, KV_LORA), krope_s.reshape(bs, 1, QK_ROPE), cache_ckv[l:l + 1],
                         cache_krope[l:l + 1].transpose(0, 1, 3, 2))
    attn_s = _sample_attn_out(o_lat.transpose(1, 0, 2), p["w_uv_h"], p["norm_attn_out"])
    y_s, hre_s, him_s = _s5_sample(u_s.reshape(bs, S5_WIDTH), state_s5_re[l], state_s5_im[l], ops)
    xs = _mixout(xs, attn_s, y_s, mod_s, 1, p, tm=bs)
    y_sample, *f2 = _ffn_cast(xs, mod_s, 6, norm_ffn2[l], ffn2_w1[l], ffn2_w3[l], ffn2_w2[l], norm_final)
    y_sample = y_sample.reshape(bs, 1, D_MODEL)

    tm_p = TILES.ffn_rows
    xp = x_prompt.reshape(bp * seq, D_MODEL)
    xp = _ffn(xp, mod_p, 0, seq // tm_p, norm_ffn1[l], *f1, tm=tm_p)
    tm_mix = TILES.mix_rows
    tq_p = TILES.attn_q
    q_lat_t, q_rope_t, ckv_p, kc_p, kct_p, krope_t_p, kr_p, u_p = _mixin(xp, mod_p, seq // tm_mix, seq, p, cos_p,
                                                                         sin_p, tm=tm_mix, tq=tq_p)
    attn_p = _prompt_attn(q_lat_t, q_rope_t, kc_p, kr_p, kct_p, p["w_uv_h"], p["norm_attn_out_col"], tq=tq_p,
                          tk=TILES.attn_k)
    y_p, hre_p, him_p = _s5_prompt(u_p, ops)
    xp = _mixout(xp, attn_p.reshape(bp * seq, MLA_WIDTH), y_p.reshape(bp * seq, S5_WIDTH), mod_p, seq // tm_mix, p,
                 tm=tm_mix)
    y_prompt = _ffn(xp, mod_p, 6, seq // tm_p, norm_ffn2[l], *f2, norm_final, tm=tm_p).reshape(bp, seq, D_MODEL)

    return (y_prompt, y_sample,
            ckv_p[None], krope_t_p.transpose(0, 2, 1)[None],
            ckv_s.reshape(1, bs, 1, KV_LORA), krope_s.reshape(1, bs, 1, QK_ROPE),
            hre_p[None], him_p[None], hre_s[None], him_s[None])
```

```python
import functools
from typing import NamedTuple

import jax
import jax.numpy as jnp
from jax import lax
from jax.experimental import pallas as pl
from jax.experimental.pallas import tpu as pltpu

F32 = jnp.float32
BF16 = jnp.bfloat16

D_MODEL = 2048
D_FF = 5632
MLA_HEADS = 8
QK_NOPE = 128
QK_ROPE = 64
V_HEAD = 128
Q_LORA = 512
KV_LORA = 256
MLA_WIDTH = 1024
S5_WIDTH = 1024
S5_GROUP = 16
S5_GROUPS = 64
S5_STATE = 64
ADA_CHUNKS = 9
PAGE = 128
ROPE_THETA = 10000.0
SOFTMAX_SCALE = (QK_NOPE + QK_ROPE) ** -0.5
EPS = 1e-6

LANES = 128
S5_CHUNK = 16
GROUPS_PER_SLAB = LANES // S5_GROUP
N_SLABS = S5_WIDTH // LANES
SLAB_STATE = GROUPS_PER_SLAB * S5_STATE
CHUNK_COLS = S5_CHUNK * LANES
MXU_TILE = 2 * LANES
VMEM_LIMIT = 60 * 1024 * 1024


class _Tiles(NamedTuple):
    ada_cols: int = 1024
    ffn_rows: int = 1024
    ffn_cols: int = 512
    mix_rows: int = 512
    attn_q: int = 512
    attn_k: int = 512
    pages: int = 16
    streams: int = 2
    slots: int = 3


TILES = _Tiles()


def _cparams(sem):
    return pltpu.CompilerParams(dimension_semantics=sem, vmem_limit_bytes=VMEM_LIMIT)


def _rms(x, g):
    return x * lax.rsqrt(jnp.mean(x * x, axis=-1, keepdims=True) + EPS) * g


def _rms_rows(x, g):
    return x * lax.rsqrt(jnp.mean(x * x, axis=0, keepdims=True) + EPS) * g


def _dot(a, b):
    return jnp.dot(a, b, preferred_element_type=F32)


def _dot_nt(a, b):
    return lax.dot_general(a, b, (((1,), (1,)), ((), ())), preferred_element_type=F32)


def _ada_kernel(c_ref, w_ref, b_ref, o_ref):
    c = c_ref[...]
    a = (c * jax.nn.sigmoid(c)).astype(BF16)
    o_ref[...] = _dot(a, w_ref[...].astype(BF16)) + b_ref[...]


def _ada(c, w_ada, b_ada):
    rows = c.shape[0]
    n = w_ada.shape[1]
    tn = TILES.ada_cols
    return pl.pallas_call(
        _ada_kernel,
        grid=(n // tn,),
        in_specs=[
            pl.BlockSpec((rows, D_MODEL), lambda j: (0, 0)),
            pl.BlockSpec((D_MODEL, tn), lambda j: (0, j)),
            pl.BlockSpec((1, tn), lambda j: (0, j)),
        ],
        out_specs=pl.BlockSpec((rows, tn), lambda j: (0, j)),
        out_shape=jax.ShapeDtypeStruct((rows, n), F32),
        compiler_params=_cparams(("arbitrary",)),
        name="ada_modulation",
    )(c, w_ada, b_ada.reshape(1, n))


def _ffn_kernel(x_ref, sh_ref, sc_ref, gt_ref, g_ref, w1_ref, w3_ref, w2_ref, w2_last_ref, *rest, final_norm):
    if final_norm:
        gf_ref, o_ref, h_ref, act_ref = rest
    else:
        o_ref, h_ref, act_ref = rest
    j = pl.program_id(1)
    last = pl.num_programs(1) - 1

    def gate():
        h = h_ref[...]
        a = _dot(h, w1_ref[...])
        b = _dot(h, w3_ref[...])
        act_ref[...] = (a * jax.nn.sigmoid(a) * b).astype(BF16)

    def down():
        return _dot(act_ref[...], w2_ref[...])

    @pl.when(j == 0)
    def _():
        h = _rms(x_ref[...], g_ref[...]) * (1.0 + sc_ref[0]) + sh_ref[0]
        h_ref[...] = h.astype(BF16)
        gate()

    @pl.when(j == 1)
    def _():
        o_ref[...] = down()
        gate()

    @pl.when((j > 1) & (j < last))
    def _():
        o_ref[...] += down()
        gate()

    @pl.when(j == last)
    def _():
        prev = down()
        gate()
        y = x_ref[...] + 0.5 * gt_ref[0] * (o_ref[...] + prev + _dot(act_ref[...], w2_last_ref[...]))
        if final_norm:
            y = _rms(y, gf_ref[...])
        o_ref[...] = y


def _ffn_cast_kernel(x_ref, sh_ref, sc_ref, gt_ref, g_ref, w1_ref, w3_ref, w2_ref, *rest, final_norm):
    if final_norm:
        gf_ref, o_ref, w1b_ref, w3b_ref, w2b_ref, h_ref = rest
    else:
        o_ref, w1b_ref, w3b_ref, w2b_ref, h_ref = rest
    j = pl.program_id(0)

    @pl.when(j == 0)
    def _():
        h = _rms(x_ref[...], g_ref[...]) * (1.0 + sc_ref[0]) + sh_ref[0]
        h_ref[...] = h.astype(BF16)
        o_ref[...] = jnp.zeros_like(o_ref)

    w1b = w1_ref[...].astype(BF16)
    w3b = w3_ref[...].astype(BF16)
    w2b = w2_ref[...].astype(BF16)
    w1b_ref[...] = w1b
    w3b_ref[...] = w3b
    w2b_ref[...] = w2b
    h = h_ref[...]
    a = _dot(h, w1b)
    act = (a * jax.nn.sigmoid(a) * _dot(h, w3b)).astype(BF16)
    o_ref[...] += _dot(act, w2b)

    @pl.when(j == pl.num_programs(0) - 1)
    def _():
        y = x_ref[...] + 0.5 * gt_ref[0] * o_ref[...]
        if final_norm:
            y = _rms(y, gf_ref[...])
        o_ref[...] = y


def _ffn_cast(x, mod, chunk0, g, w1, w3, w2, gf=None, *, tf=TILES.ffn_cols):
    rows = x.shape[0]
    const = lambda shape: pl.BlockSpec(shape, lambda j: (0,) * len(shape))
    mod_spec = lambda k: pl.BlockSpec((1, rows, D_MODEL), lambda j: (0, 0, k))
    up = pl.BlockSpec((D_MODEL, tf), lambda j: (0, j))
    down = pl.BlockSpec((tf, D_MODEL), lambda j: (j, 0))
    in_specs = [const((rows, D_MODEL)), mod_spec(chunk0), mod_spec(chunk0 + 1), mod_spec(chunk0 + 2),
                const((1, D_MODEL)), up, up, down]
    args = [x, mod, mod, mod, g.reshape(1, D_MODEL), w1, w3, w2]
    if gf is not None:
        in_specs.append(const((1, D_MODEL)))
        args.append(gf.reshape(1, D_MODEL))
    return pl.pallas_call(
        functools.partial(_ffn_cast_kernel, final_norm=gf is not None),
        grid=(D_FF // tf,),
        in_specs=in_specs,
        out_specs=[const((rows, D_MODEL)), up, up, down],
        out_shape=[jax.ShapeDtypeStruct((rows, D_MODEL), F32), jax.ShapeDtypeStruct(w1.shape, BF16),
                   jax.ShapeDtypeStruct(w3.shape, BF16), jax.ShapeDtypeStruct(w2.shape, BF16)],
        scratch_shapes=[pltpu.VMEM((rows, D_MODEL), BF16)],
        compiler_params=_cparams(("arbitrary",)),
        name="sample_ffn_cast",
    )(*args)


def _ffn(x, mod, chunk0, tiles_per_b, g, w1, w3, w2, gf=None, *, tm, tf=TILES.ffn_cols):
    n = x.shape[0]
    r = mod.shape[1]
    n_ff = D_FF // tf
    assert n_ff >= 3

    def mod_spec(k):
        return pl.BlockSpec((1, r, D_MODEL), lambda i, j: (i // tiles_per_b, 0, k))

    in_specs = [
        pl.BlockSpec((tm, D_MODEL), lambda i, j: (i, 0), pipeline_mode=pl.Buffered(1)),
        mod_spec(chunk0), mod_spec(chunk0 + 1), mod_spec(chunk0 + 2),
        pl.BlockSpec((1, D_MODEL), lambda i, j: (0, 0)),
        pl.BlockSpec((D_MODEL, tf), lambda i, j: (0, j)),
        pl.BlockSpec((D_MODEL, tf), lambda i, j: (0, j)),
        pl.BlockSpec((tf, D_MODEL), lambda i, j: (jnp.maximum(j - 1, 0), 0)),
        pl.BlockSpec((tf, D_MODEL), lambda i, j: (n_ff - 1, 0), pipeline_mode=pl.Buffered(1)),
    ]
    args = [x, mod, mod, mod, g.reshape(1, D_MODEL), w1, w3, w2, w2]
    if gf is not None:
        in_specs.append(pl.BlockSpec((1, D_MODEL), lambda i, j: (0, 0)))
        args.append(gf.reshape(1, D_MODEL))
    return pl.pallas_call(
        functools.partial(_ffn_kernel, final_norm=gf is not None),
        grid=(n // tm, n_ff),
        in_specs=in_specs,
        out_specs=pl.BlockSpec((tm, D_MODEL), lambda i, j: (i, 0)),
        out_shape=jax.ShapeDtypeStruct((n, D_MODEL), F32),
        scratch_shapes=[pltpu.VMEM((tm, D_MODEL), BF16), pltpu.VMEM((tm, tf), BF16)],
        compiler_params=_cparams(("parallel", "arbitrary")),
        name="macaron_ffn",
    )(*args)


def _rope_rows(x, cos, sin):
    half = QK_ROPE // 2
    x1 = x[:half]
    x2 = x[half:]
    return x1 * cos - x2 * sin, x1 * sin + x2 * cos


def _mixin_kernel(x_ref, sh_ref, sc_ref, g_ref, wq_ref, wkv_ref, wkr_ref, wu_ref, gq_ref, wqn_ref, wqr_ref,
                  wuk_ref, gkv_ref, cos_ref, sin_ref,
                  qlat_ref, qrope_ref, ckv_ref, kc_ref, kct_ref, krope_ref, kr_ref, u_ref, *, tq):
    half = QK_ROPE // 2
    n_q = x_ref.shape[0] // tq
    h = (_rms(x_ref[...], g_ref[...]) * (1.0 + sc_ref[0]) + sh_ref[0]).astype(BF16)
    cos = cos_ref[...]
    sin = sin_ref[...]

    u_ref[0] = _dot(h, wu_ref[...])

    ckv = _rms(_dot(h, wkv_ref[...]), gkv_ref[...])
    ckv_ref[0] = ckv
    kc_ref[0] = ckv.astype(BF16)
    kct_ref[0] = ckv.T.astype(BF16)

    k1, k2 = _rope_rows(_dot_nt(wkr_ref[...], h), cos, sin)
    kr_t = jnp.concatenate([k1, k2], axis=0)
    krope_ref[0] = kr_t
    kr_ref[0] = kr_t.T.astype(BF16)

    qn = _rms(_dot(h, wq_ref[...]), gq_ref[...]).astype(BF16)
    q_nope = (_dot(qn, wqn_ref[...]) * SOFTMAX_SCALE).astype(BF16)
    q_rope_t = _dot_nt(wqr_ref[...], qn) * SOFTMAX_SCALE
    for hd in range(MLA_HEADS):
        ql_t = _dot_nt(wuk_ref[hd], q_nope[:, hd * QK_NOPE:(hd + 1) * QK_NOPE]).astype(BF16)
        r1, r2 = _rope_rows(q_rope_t[hd * QK_ROPE:(hd + 1) * QK_ROPE], cos, sin)
        r1 = r1.astype(BF16)
        r2 = r2.astype(BF16)
        for qq in range(n_q):
            src = slice(qq * tq, (qq + 1) * tq)
            dst = slice(hd * tq, (hd + 1) * tq)
            qlat_ref[0, qq, :, dst] = ql_t[:, src]
            qrope_ref[0, qq, :half, dst] = r1[:, src]
            qrope_ref[0, qq, half:, dst] = r2[:, src]


def _mixin(x, mod, tiles_per_b, t_len, p, cos_t, sin_t, *, tm, tq):
    n = x.shape[0]
    nb = n // t_len
    r = mod.shape[1]
    half = QK_ROPE // 2
    full = lambda shape: pl.BlockSpec(shape, lambda i: (0,) * len(shape))
    rows = lambda width: pl.BlockSpec((1, tm, width), lambda i: (i // tiles_per_b, i % tiles_per_b, 0))
    cols = lambda height: pl.BlockSpec((1, height, tm), lambda i: (i // tiles_per_b, 0, i % tiles_per_b))
    head_cols = lambda height: pl.BlockSpec((1, tm // tq, height, MLA_HEADS * tq),
                                            lambda i: (i // tiles_per_b, i % tiles_per_b, 0, 0))
    in_specs = [
        pl.BlockSpec((tm, D_MODEL), lambda i: (i, 0)),
        pl.BlockSpec((1, r, D_MODEL), lambda i: (i // tiles_per_b, 0, 3)),
        pl.BlockSpec((1, r, D_MODEL), lambda i: (i // tiles_per_b, 0, 4)),
        full((1, D_MODEL)),
        full((D_MODEL, Q_LORA)), full((D_MODEL, KV_LORA)), full((QK_ROPE, D_MODEL)), full((D_MODEL, S5_WIDTH)),
        full((1, Q_LORA)), full((Q_LORA, MLA_HEADS * QK_NOPE)), full((MLA_HEADS * QK_ROPE, Q_LORA)),
        full((MLA_HEADS, KV_LORA, QK_NOPE)), full((1, KV_LORA)),
        pl.BlockSpec((half, tm), lambda i: (0, i % tiles_per_b)),
        pl.BlockSpec((half, tm), lambda i: (0, i % tiles_per_b)),
    ]
    out_specs = [head_cols(KV_LORA), head_cols(QK_ROPE), rows(KV_LORA), rows(KV_LORA), cols(KV_LORA),
                 cols(QK_ROPE), rows(QK_ROPE), rows(S5_WIDTH)]
    out_shape = [
        jax.ShapeDtypeStruct((nb, t_len // tq, KV_LORA, MLA_HEADS * tq), BF16),
        jax.ShapeDtypeStruct((nb, t_len // tq, QK_ROPE, MLA_HEADS * tq), BF16),
        jax.ShapeDtypeStruct((nb, t_len, KV_LORA), F32),
        jax.ShapeDtypeStruct((nb, t_len, KV_LORA), BF16),
        jax.ShapeDtypeStruct((nb, KV_LORA, t_len), BF16),
        jax.ShapeDtypeStruct((nb, QK_ROPE, t_len), F32),
        jax.ShapeDtypeStruct((nb, t_len, QK_ROPE), BF16),
        jax.ShapeDtypeStruct((nb, t_len, S5_WIDTH), F32),
    ]
    return pl.pallas_call(
        functools.partial(_mixin_kernel, tq=tq),
        grid=(n // tm,),
        in_specs=in_specs,
        out_specs=out_specs,
        out_shape=out_shape,
        compiler_params=_cparams(("parallel",)),
        name="mixer_in",
    )(x, mod, mod, p["norm_mix"], p["w_q"], p["w_kv"], p["w_kr_t"], p["w_u"], p["norm_q"], p["w_uq_nope"],
      p["w_uq_rope_t"], p["w_uk_h"], p["norm_kv"], cos_t, sin_t)


def _prompt_attn_kernel(ql_ref, qr_ref, kc_ref, kr_ref, kct_ref, wuv_ref, g_ref, o_ref, *scratch, tq, tk):
    qi = pl.program_id(1)
    n_half = len(scratch) // 3
    m_ref, l_ref, acc_ref = scratch[:n_half], scratch[n_half:2 * n_half], scratch[2 * n_half:]
    width = m_ref[0].shape[1]
    heads_per_half = width // tq
    for hf in range(n_half):
        m_ref[hf][...] = jnp.full_like(m_ref[hf], -jnp.inf)
        l_ref[hf][...] = jnp.zeros_like(l_ref[hf])
        acc_ref[hf][...] = jnp.zeros_like(acc_ref[hf])

    def step(ki, diag):
        start = pl.multiple_of(ki * tk, tk)
        kc = kc_ref[0, pl.ds(start, tk), :]
        kr = kr_ref[0, pl.ds(start, tk), :]
        kct = kct_ref[0, :, pl.ds(start, tk)]
        masked = diag is not None
        if masked:
            k_local = lax.broadcasted_iota(jnp.int32, (tk, width), 0) + diag * tk
            t_local = lax.broadcasted_iota(jnp.int32, (tk, width), 1) & (tq - 1)
            keep = k_local <= t_local
        for hf in range(n_half):
            lanes = slice(hf * width, (hf + 1) * width)
            s = _dot(kc, ql_ref[0, 0, :, lanes]) + _dot(kr, qr_ref[0, 0, :, lanes])
            if masked:
                s = jnp.where(keep, s, -jnp.inf)
            m_old = m_ref[hf][...]
            m_new = jnp.maximum(m_old, jnp.max(s, axis=0, keepdims=True))
            alpha = jnp.exp(m_old - m_new)
            pexp = jnp.exp(s - m_new)
            l_ref[hf][...] = alpha * l_ref[hf][...] + jnp.sum(pexp, axis=0, keepdims=True)
            acc_ref[hf][...] = alpha * acc_ref[hf][...] + _dot(kct, pexp.astype(BF16))
            m_ref[hf][...] = m_new

    def body(ki, carry):
        step(ki, None)
        return carry

    blocks_per_tile = tq // tk
    lax.fori_loop(0, qi * blocks_per_tile, body, 0)
    for d in range(blocks_per_tile):
        step(qi * blocks_per_tile + d, d)
    parts = []
    for hd in range(MLA_HEADS):
        hf = hd // heads_per_half
        lanes = slice((hd % heads_per_half) * tq, (hd % heads_per_half + 1) * tq)
        o_t = acc_ref[hf][:, lanes] / l_ref[hf][:, lanes]
        parts.append(_dot(wuv_ref[hd], o_t.astype(BF16)))
    attn_t = _rms_rows(jnp.concatenate(parts, axis=0), g_ref[...])
    o_ref[0] = attn_t.T.astype(BF16)


def _prompt_attn(q_lat_t, q_rope_t, kc, kr, kc_t, w_uv_h, g_col, *, tq, tk, n_half=1):
    nb, n_q, _, _ = q_lat_t.shape
    t_len = n_q * tq
    width = MLA_HEADS * tq // n_half
    assert tq % tk == 0
    return pl.pallas_call(
        functools.partial(_prompt_attn_kernel, tq=tq, tk=tk),
        grid=(nb, n_q),
        in_specs=[
            pl.BlockSpec((1, 1, KV_LORA, MLA_HEADS * tq), lambda b, i: (b, i, 0, 0)),
            pl.BlockSpec((1, 1, QK_ROPE, MLA_HEADS * tq), lambda b, i: (b, i, 0, 0)),
            pl.BlockSpec((1, t_len, KV_LORA), lambda b, i: (b, 0, 0)),
            pl.BlockSpec((1, t_len, QK_ROPE), lambda b, i: (b, 0, 0)),
            pl.BlockSpec((1, KV_LORA, t_len), lambda b, i: (b, 0, 0)),
            pl.BlockSpec((MLA_HEADS, V_HEAD, KV_LORA), lambda b, i: (0, 0, 0)),
            pl.BlockSpec((MLA_WIDTH, 1), lambda b, i: (0, 0)),
        ],
        out_specs=pl.BlockSpec((1, tq, MLA_WIDTH), lambda b, i: (b, i, 0)),
        out_shape=jax.ShapeDtypeStruct((nb, t_len, MLA_WIDTH), BF16),
        scratch_shapes=([pltpu.VMEM((1, width), F32)] * (2 * n_half) + [pltpu.VMEM((KV_LORA, width), F32)] * n_half),
        compiler_params=_cparams(("parallel", "parallel")),
        name="prompt_attention",
    )(q_lat_t, q_rope_t, kc, kr, kc_t, w_uv_h, g_col)


def _sample_attn_kernel(pt_ref, ql_ref, qr_ref, kcn_ref, krn_ref, ckv_hbm, kr_hbm, o_ref, kbuf, rbuf, sem_k, sem_r,
                        *, pages, streams):
    nb, n_pages = pt_ref.shape
    n_chunks = n_pages // pages
    n_groups = nb // streams
    total = n_groups * n_chunks
    n_slots = kbuf.shape[0]
    ahead = n_slots - 1

    def chunk_copies(g, slot):
        grp = g // n_chunks
        c = g % n_chunks
        copies = []
        for st in range(streams):
            b = grp * streams + st
            for i in range(pages):
                page = pt_ref[b, c * pages + i]
                copies.append(pltpu.make_async_copy(
                    ckv_hbm.at[0, page], kbuf.at[slot, st, pl.ds(i * PAGE, PAGE), :], sem_k.at[slot, st]))
                copies.append(pltpu.make_async_copy(
                    kr_hbm.at[0, page], rbuf.at[slot, st, i], sem_r.at[slot, st]))
        return copies

    grp = pl.program_id(0)

    @pl.when(grp == 0)
    def _():
        for g0 in range(ahead):
            for cp in chunk_copies(g0, g0):
                cp.start()

    qs = [(ql_ref[st], qr_ref[st]) for st in range(streams)]

    def chunk_body(c, carry):
        g = grp * n_chunks + c
        slot = g % n_slots

        @pl.when(g + ahead < total)
        def _():
            for cp in chunk_copies(g + ahead, (g + ahead) % n_slots):
                cp.start()

        for cp in chunk_copies(g, slot):
            cp.wait()
        out = []
        for st in range(streams):
            m_old, l_old, acc_old = carry[st]
            ql, qr = qs[st]
            kc = kbuf[slot, st].astype(BF16)
            kr_t = jnp.concatenate([rbuf[slot, st, i].astype(BF16) for i in range(pages)], axis=1)
            s = _dot_nt(ql, kc) + _dot(qr, kr_t)
            m_new = jnp.maximum(m_old, jnp.max(s, axis=1, keepdims=True))
            alpha = jnp.exp(m_old - m_new)
            pexp = jnp.exp(s - m_new)
            l_new = alpha * l_old + jnp.sum(pexp, axis=1, keepdims=True)
            acc_new = alpha * acc_old + _dot(pexp.astype(BF16), kc)
            out.append((m_new, l_new, acc_new))
        return tuple(out)

    init = tuple((jnp.full((MLA_HEADS, 1), -jnp.inf, F32), jnp.zeros((MLA_HEADS, 1), F32),
                  jnp.zeros((MLA_HEADS, KV_LORA), F32)) for _ in range(streams))
    final = lax.fori_loop(0, n_chunks, chunk_body, init)
    for st in range(streams):
        m_old, l_old, acc_old = final[st]
        ql, qr = qs[st]
        kcn = kcn_ref[st].astype(BF16).astype(F32)
        krn = krn_ref[st].astype(BF16).astype(F32)
        s_n = (jnp.sum(ql.astype(F32) * kcn, axis=1, keepdims=True)
               + jnp.sum(qr.astype(F32) * krn, axis=1, keepdims=True))
        m_f = jnp.maximum(m_old, s_n)
        a_f = jnp.exp(m_old - m_f)
        p_n = jnp.exp(s_n - m_f)
        l_f = a_f * l_old + p_n
        acc_f = a_f * acc_old + p_n.astype(BF16).astype(F32) * kcn
        o_ref[st] = acc_f / l_f


def _sample_attn(page_table, q_lat, q_rope, ckv_new, krope_new, cache_ckv, cache_krope_t, *, pages=TILES.pages,
                 streams=TILES.streams, slots=TILES.slots):
    nb, n_pages = page_table.shape
    assert n_pages % pages == 0 and nb % streams == 0 and (nb // streams) * (n_pages // pages) >= slots
    per_group = lambda rows, width: pl.BlockSpec((streams, rows, width), lambda g, pt: (g, 0, 0))
    grid_spec = pltpu.PrefetchScalarGridSpec(
        num_scalar_prefetch=1,
        grid=(nb // streams,),
        in_specs=[
            per_group(MLA_HEADS, KV_LORA), per_group(MLA_HEADS, QK_ROPE),
            per_group(1, KV_LORA), per_group(1, QK_ROPE),
            pl.BlockSpec(memory_space=pl.ANY), pl.BlockSpec(memory_space=pl.ANY),
        ],
        out_specs=per_group(MLA_HEADS, KV_LORA),
        scratch_shapes=[
            pltpu.VMEM((slots, streams, pages * PAGE, KV_LORA), F32),
            pltpu.VMEM((slots, streams, pages, QK_ROPE, PAGE), F32),
            pltpu.SemaphoreType.DMA((slots, streams)),
            pltpu.SemaphoreType.DMA((slots, streams)),
        ],
    )
    return pl.pallas_call(
        functools.partial(_sample_attn_kernel, pages=pages, streams=streams),
        grid_spec=grid_spec,
        out_shape=jax.ShapeDtypeStruct((nb, MLA_HEADS, KV_LORA), F32),
        compiler_params=_cparams(("arbitrary",)),
        name="sample_paged_attention",
    )(page_table, q_lat, q_rope, ckv_new, krope_new, cache_ckv, cache_krope_t)


def _sample_attn_out_kernel(o_ref, wuv_ref, g_ref, out_ref):
    parts = [_dot_nt(o_ref[hd].astype(BF16), wuv_ref[hd]) for hd in range(MLA_HEADS)]
    out_ref[...] = _rms(jnp.concatenate(parts, axis=1), g_ref[...]).astype(BF16)


def _sample_attn_out(o_lat_h, w_uv_h, g):
    nb = o_lat_h.shape[1]
    return pl.pallas_call(
        _sample_attn_out_kernel,
        out_shape=jax.ShapeDtypeStruct((nb, MLA_WIDTH), BF16),
        name="sample_attention_out",
    )(o_lat_h, w_uv_h, g)


def _same_group(shape, row_shift, col_shift):
    rows = lax.broadcasted_iota(jnp.int32, shape, 0)
    cols = lax.broadcasted_iota(jnp.int32, shape, 1)
    mask = GROUPS_PER_SLAB - 1
    return ((rows >> row_shift) & mask) == ((cols >> col_shift) & mask)


def _s5_prompt_kernel(u_ref, wr_ref, we_ref, wf_ref, tr_ref, te_ref, tf_ref, are_ref, aim_ref, d_ref,
                      y_ref, hre_ref, him_ref, r_ref, e_ref, f_ref, uc_ref, s_ref, hp_ref, yc_ref):
    n_chunks = uc_ref.shape[0]
    n_blk = CHUNK_COLS // MXU_TILE
    ch_bits = S5_GROUP.bit_length() - 1
    st_bits = S5_STATE.bit_length() - 1

    @pl.when(pl.program_id(1) == 0)
    def _():
        zero = jnp.zeros((), F32)
        r_ref[...] = jnp.where(_same_group(r_ref.shape, ch_bits, ch_bits),
                               _dot(wr_ref[0], tr_ref[...]), zero).astype(BF16)
        e_ref[...] = jnp.where(_same_group(e_ref.shape, ch_bits, st_bits),
                               _dot(we_ref[0], te_ref[...]), zero).astype(BF16)
        f_ref[...] = jnp.where(_same_group(f_ref.shape, st_bits, ch_bits),
                               _dot(tf_ref[...], wf_ref[0]), zero).astype(BF16)

    for s in range(S5_CHUNK):
        uc_ref[:, s * LANES:(s + 1) * LANES] = u_ref[0, pl.ds(s, n_chunks, stride=S5_CHUNK), :]
    uc = uc_ref[...]
    ub = uc.astype(BF16)

    s_ref[...] = _dot(ub, e_ref[...])

    a_re = are_ref[0]
    a_im = aim_ref[0]

    def scan(k, carry):
        h_re, h_im = carry
        hp_ref[pl.ds(k, 1), :] = jnp.concatenate([h_re, h_im], axis=1)
        row = s_ref[pl.ds(k, 1), :]
        n_re = a_re * h_re - a_im * h_im + row[:, :SLAB_STATE]
        n_im = a_re * h_im + a_im * h_re + row[:, SLAB_STATE:]
        return n_re, n_im

    zero_state = jnp.zeros((1, SLAB_STATE), F32)
    h_re, h_im = lax.fori_loop(0, n_chunks, scan, (zero_state, zero_state), unroll=8)
    hre_ref[0, 0] = h_re
    him_ref[0, 0] = h_im

    yc_ref[...] = _dot(hp_ref[...].astype(BF16), f_ref[...]) + uc * d_ref[0]
    for tb in range(n_blk):
        yc_ref[:, tb * MXU_TILE:(tb + 1) * MXU_TILE] += _dot(ub[:, :(tb + 1) * MXU_TILE],
                                                             r_ref[(n_blk - 1 - tb) * MXU_TILE:, :])
    y = jax.nn.gelu(yc_ref[...])
    for t in range(S5_CHUNK):
        y_ref[0, pl.ds(t, n_chunks, stride=S5_CHUNK), :] = y[:, t * LANES:(t + 1) * LANES]


def _s5_prompt(u, ops):
    nb, t_len, _ = u.shape
    n_chunks = t_len // S5_CHUNK
    slab = lambda shape: pl.BlockSpec((1,) + shape, lambda j, b: (j,) + (0,) * len(shape))
    full = lambda shape: pl.BlockSpec(shape, lambda j, b: (0,) * len(shape))
    y, hre, him = pl.pallas_call(
        _s5_prompt_kernel,
        grid=(N_SLABS, nb),
        in_specs=[
            pl.BlockSpec((1, t_len, LANES), lambda j, b: (b, 0, j)),
            slab((CHUNK_COLS, 2 * S5_GROUP)), slab((CHUNK_COLS, 2 * S5_STATE)), slab((2 * S5_STATE, CHUNK_COLS)),
            full((2 * S5_GROUP, MXU_TILE)), full((2 * S5_STATE, 2 * SLAB_STATE)), full((2 * SLAB_STATE, 2 * S5_STATE)),
            slab((1, SLAB_STATE)), slab((1, SLAB_STATE)), slab((1, CHUNK_COLS)),
        ],
        out_specs=[
            pl.BlockSpec((1, t_len, LANES), lambda j, b: (b, 0, j)),
            pl.BlockSpec((1, 1, 1, SLAB_STATE), lambda j, b: (j, b, 0, 0)),
            pl.BlockSpec((1, 1, 1, SLAB_STATE), lambda j, b: (j, b, 0, 0)),
        ],
        out_shape=[
            jax.ShapeDtypeStruct((nb, t_len, S5_WIDTH), F32),
            jax.ShapeDtypeStruct((N_SLABS, nb, 1, SLAB_STATE), F32),
            jax.ShapeDtypeStruct((N_SLABS, nb, 1, SLAB_STATE), F32),
        ],
        scratch_shapes=[
            pltpu.VMEM((CHUNK_COLS, MXU_TILE), BF16),
            pltpu.VMEM((CHUNK_COLS, 2 * SLAB_STATE), BF16),
            pltpu.VMEM((2 * SLAB_STATE, CHUNK_COLS), BF16),
            pltpu.VMEM((n_chunks, CHUNK_COLS), F32),
            pltpu.VMEM((n_chunks, 2 * SLAB_STATE), F32),
            pltpu.VMEM((n_chunks, 2 * SLAB_STATE), F32),
            pltpu.VMEM((n_chunks, CHUNK_COLS), F32),
        ],
        compiler_params=_cparams(("arbitrary", "arbitrary")),
        name="s5_prompt",
    )(u, ops["wr"], ops["we"], ops["wf"], ops["tile_r"], ops["tile_e"], ops["tile_f"], ops["a_re"], ops["a_im"],
      ops["d_chunk"])

    def to_state(h):
        return h.reshape(N_SLABS, nb, GROUPS_PER_SLAB, S5_STATE).transpose(1, 0, 2, 3).reshape(nb, S5_GROUPS, S5_STATE)

    return y, to_state(hre), to_state(him)


def _split_bf16(x):
    hi = x.astype(BF16)
    return hi, (x - hi.astype(F32)).astype(BF16)


def _s5_sample_kernel(u_ref, h0re_ref, h0im_ref, bd_hi_ref, bd_lo_ref, cd_ref, lre_ref, lim_ref, d_ref,
                      y_ref, hre_ref, him_ref):
    u = u_ref[...]
    u_hi, u_lo = _split_bf16(u)
    bu = _dot(u_hi, bd_hi_ref[0]) + (_dot(u_hi, bd_lo_ref[0]) + _dot(u_lo, bd_hi_ref[0]))
    l_re = lre_ref[0]
    l_im = lim_ref[0]
    h0_re = h0re_ref[...]
    h0_im = h0im_ref[...]
    h_re = l_re * h0_re - l_im * h0_im + bu[:, :SLAB_STATE]
    h_im = l_re * h0_im + l_im * h0_re + bu[:, SLAB_STATE:]
    hre_ref[...] = h_re
    him_ref[...] = h_im
    h = jnp.concatenate([h_re, h_im], axis=1).astype(BF16)
    y_ref[...] = jax.nn.gelu(_dot(h, cd_ref[0]) + u * d_ref[0])


def _s5_sample(u, h0_re, h0_im, ops):
    nb = u.shape[0]
    n_state = S5_GROUPS * S5_STATE
    slab = lambda shape: pl.BlockSpec((1,) + shape, lambda j: (j,) + (0,) * len(shape))
    col = lambda width: pl.BlockSpec((nb, width), lambda j: (0, j))
    y, hre, him = pl.pallas_call(
        _s5_sample_kernel,
        grid=(N_SLABS,),
        in_specs=[
            col(LANES), col(SLAB_STATE), col(SLAB_STATE),
            slab((LANES, 2 * SLAB_STATE)), slab((LANES, 2 * SLAB_STATE)), slab((2 * SLAB_STATE, LANES)),
            slab((1, SLAB_STATE)), slab((1, SLAB_STATE)), slab((1, LANES)),
        ],
        out_specs=[col(LANES), col(SLAB_STATE), col(SLAB_STATE)],
        out_shape=[
            jax.ShapeDtypeStruct((nb, S5_WIDTH), F32),
            jax.ShapeDtypeStruct((nb, n_state), F32),
            jax.ShapeDtypeStruct((nb, n_state), F32),
        ],
        compiler_params=_cparams(("parallel",)),
        name="s5_sample",
    )(u, h0_re.reshape(nb, n_state), h0_im.reshape(nb, n_state), ops["bd_hi"], ops["bd_lo"], ops["cd"],
      ops["l_re"], ops["l_im"], ops["d_slab"])
    return y, hre.reshape(nb, S5_GROUPS, S5_STATE), him.reshape(nb, S5_GROUPS, S5_STATE)


def _mixout_kernel(x_ref, attn_ref, y_ref, gt_ref, wglu_ref, bglu_ref, gs_ref, woa_ref, woy_ref, o_ref):
    y = y_ref[...]
    z = _dot(y.astype(BF16), wglu_ref[...]) + bglu_ref[...]
    yn = _rms(y * jax.nn.sigmoid(z), gs_ref[...]).astype(BF16)
    mix = _dot(attn_ref[...], woa_ref[...]) + _dot(yn, woy_ref[...])
    o_ref[...] = x_ref[...] + gt_ref[0] * mix


def _mixout(x, attn, y, mod, tiles_per_b, p, *, tm):
    n = x.shape[0]
    r = mod.shape[1]
    full = lambda shape: pl.BlockSpec(shape, lambda i: (0,) * len(shape))
    return pl.pallas_call(
        _mixout_kernel,
        grid=(n // tm,),
        in_specs=[
            pl.BlockSpec((tm, D_MODEL), lambda i: (i, 0)),
            pl.BlockSpec((tm, MLA_WIDTH), lambda i: (i, 0)),
            pl.BlockSpec((tm, S5_WIDTH), lambda i: (i, 0)),
            pl.BlockSpec((1, r, D_MODEL), lambda i: (i // tiles_per_b, 0, 5)),
            full((S5_WIDTH, S5_WIDTH)), full((1, S5_WIDTH)), full((1, S5_WIDTH)),
            full((MLA_WIDTH, D_MODEL)), full((S5_WIDTH, D_MODEL)),
        ],
        out_specs=pl.BlockSpec((tm, D_MODEL), lambda i: (i, 0)),
        out_shape=jax.ShapeDtypeStruct((n, D_MODEL), F32),
        compiler_params=_cparams(("parallel",)),
        name="mixer_out",
    )(x, attn, y, mod, p["w_glu"], p["b_glu"], p["norm_ssm_out"], p["w_out_attn"], p["w_out_ssm"])


def _rope_tables(pos):
    half = QK_ROPE // 2
    inv_freq = ROPE_THETA ** (-jnp.arange(half, dtype=F32) / half)
    ang = pos.astype(F32)[:, None] * inv_freq[None, :]
    return jnp.cos(ang).T, jnp.sin(ang).T


def _group_diag(x, g_axis, new_axis):
    x = jnp.expand_dims(x, new_axis)
    shape = [1] * x.ndim
    shape[g_axis if g_axis < new_axis else g_axis + 1] = GROUPS_PER_SLAB
    shape[new_axis] = GROUPS_PER_SLAB
    return x * jnp.eye(GROUPS_PER_SLAB, dtype=x.dtype).reshape(shape)


def _by_slab(x, g_axis):
    return x.reshape(x.shape[:g_axis] + (N_SLABS, GROUPS_PER_SLAB) + x.shape[g_axis + 1:])


def _replicate_over_groups(n_outer, n_inner):
    eo = jnp.eye(n_outer, dtype=F32)[:, None, :, None, None]
    ei = jnp.eye(n_inner, dtype=F32)[None, :, None, None, :]
    ones = jnp.ones((1, 1, 1, GROUPS_PER_SLAB, 1), F32)
    return (eo * ei * ones).reshape(n_outer * n_inner, n_outer * GROUPS_PER_SLAB * n_inner)


def _s5_operators(a_re, a_im, log_dt, b_re, b_im, c_re, c_im, d_skip):
    hp = lax.Precision.HIGHEST
    a_re, a_im, b_re, b_im, c_re, c_im = (v.astype(F32) for v in (a_re, a_im, b_re, b_im, c_re, c_im))
    dt = jnp.exp(log_dt.astype(F32))[:, None]
    z_re = a_re * dt
    z_im = a_im * dt

    def lam_pow(n):
        mag = jnp.exp(z_re[None] * n[:, None, None])
        ang = z_im[None] * n[:, None, None]
        return mag * jnp.cos(ang), mag * jnp.sin(ang)

    steps = jnp.arange(S5_CHUNK + 1, dtype=F32)
    pw_re, pw_im = lam_pow(steps)
    lb_re, lb_im = pw_re[1], pw_im[1]
    den = a_re * a_re + a_im * a_im
    q_re = ((lb_re - 1.0) * a_re + lb_im * a_im) / den
    q_im = (lb_im * a_re - (lb_re - 1.0) * a_im) / den
    bb_re = q_re[:, :, None] * b_re - q_im[:, :, None] * b_im
    bb_im = q_re[:, :, None] * b_im + q_im[:, :, None] * b_re

    cp_re = c_re[None] * pw_re[:S5_CHUNK, :, None, :] - c_im[None] * pw_im[:S5_CHUNK, :, None, :]
    cp_im = c_re[None] * pw_im[:S5_CHUNK, :, None, :] + c_im[None] * pw_re[:S5_CHUNK, :, None, :]
    m = jnp.einsum("ngcq,gqd->ngcd", jnp.concatenate([cp_re, -cp_im], axis=3),
                   jnp.concatenate([bb_re, bb_im], axis=1), precision=hp)
    m_ext = jnp.concatenate([m, jnp.zeros_like(m[:1])], axis=0)
    n_blk = S5_CHUNK // 2
    d_i = (n_blk - 1 - jnp.arange(n_blk))[:, None, None]
    s_i = jnp.arange(2)[None, :, None]
    t_i = jnp.arange(2)[None, None, :]
    lag = 2 * d_i + t_i - s_i
    mg = _by_slab(m_ext[lag], 3)
    wr = mg.transpose(3, 0, 1, 4, 6, 2, 5).reshape(N_SLABS, CHUNK_COLS, 2 * S5_GROUP)

    pr_re, pr_im = lam_pow(S5_CHUNK - 1 - steps[:S5_CHUNK])
    w_re = pr_re[:, :, :, None] * bb_re[None] - pr_im[:, :, :, None] * bb_im[None]
    w_im = pr_re[:, :, :, None] * bb_im[None] + pr_im[:, :, :, None] * bb_re[None]

    def e_half(v):
        return _by_slab(v, 1).transpose(1, 0, 2, 4, 3).reshape(N_SLABS, CHUNK_COLS, S5_STATE)

    we = jnp.concatenate([e_half(w_re), e_half(w_im)], axis=2)

    g_re = c_re[None] * pw_re[1:, :, None, :] - c_im[None] * pw_im[1:, :, None, :]
    g_im = c_re[None] * pw_im[1:, :, None, :] + c_im[None] * pw_re[1:, :, None, :]

    def f_half(v):
        return _by_slab(v, 1).transpose(1, 4, 0, 2, 3).reshape(N_SLABS, S5_STATE, CHUNK_COLS)

    wf = jnp.concatenate([f_half(g_re), f_half(-g_im)], axis=1)

    def slab_vec(v):
        return v.reshape(N_SLABS, 1, SLAB_STATE)

    d_slab = d_skip.astype(F32).reshape(N_SLABS, 1, LANES)
    tile_e = _replicate_over_groups(2, S5_STATE)

    def bd_half(v):
        return _group_diag(_by_slab(v, 0).transpose(0, 1, 3, 2), 1, 3).reshape(N_SLABS, LANES, SLAB_STATE)

    def cd_half(v):
        return _group_diag(_by_slab(v, 0).transpose(0, 1, 3, 2), 1, 3).reshape(N_SLABS, SLAB_STATE, LANES)

    bd = jnp.concatenate([bd_half(bb_re), bd_half(bb_im)], axis=2)
    cd = jnp.concatenate([cd_half(c_re), cd_half(-c_im)], axis=1)
    bd_hi = bd.astype(BF16)
    bd_lo = (bd - bd_hi.astype(F32)).astype(BF16)
    return {
        "wr": wr.astype(BF16), "we": we.astype(BF16), "wf": wf.astype(BF16),
        "tile_r": _replicate_over_groups(2, S5_GROUP).astype(BF16),
        "tile_e": tile_e.astype(BF16), "tile_f": tile_e.T.astype(BF16),
        "a_re": slab_vec(pw_re[S5_CHUNK]), "a_im": slab_vec(pw_im[S5_CHUNK]),
        "d_chunk": jnp.tile(d_slab, (1, 1, S5_CHUNK)), "d_slab": d_slab,
        "bd_hi": bd_hi, "bd_lo": bd_lo, "cd": cd.astype(BF16),
        "l_re": slab_vec(lb_re), "l_im": slab_vec(lb_im),
    }


def _layer_params(w_in, w_uq, w_uk, w_uv, w_glu, w_out, norm_mix, norm_q, norm_kv, norm_attn_out, norm_ssm_out,
                  b_glu):
    c0, c1, c2 = Q_LORA, Q_LORA + KV_LORA, Q_LORA + KV_LORA + QK_ROPE
    w_uq_h = w_uq.reshape(Q_LORA, MLA_HEADS, QK_NOPE + QK_ROPE)
    return {
        "w_q": w_in[:, :c0].astype(BF16),
        "w_kv": w_in[:, c0:c1].astype(BF16),
        "w_kr_t": w_in[:, c1:c2].T.astype(BF16),
        "w_u": w_in[:, c2:].astype(BF16),
        "w_uq_nope": w_uq_h[:, :, :QK_NOPE].reshape(Q_LORA, MLA_HEADS * QK_NOPE).astype(BF16),
        "w_uq_rope_t": w_uq_h[:, :, QK_NOPE:].reshape(Q_LORA, MLA_HEADS * QK_ROPE).T.astype(BF16),
        "w_uk_h": w_uk.transpose(1, 0, 2).astype(BF16),
        "w_uv_h": w_uv.transpose(1, 2, 0).astype(BF16),
        "w_glu": w_glu.astype(BF16),
        "w_out_attn": w_out[:MLA_WIDTH].astype(BF16),
        "w_out_ssm": w_out[MLA_WIDTH:].astype(BF16),
        "norm_mix": norm_mix.reshape(1, D_MODEL),
        "norm_q": norm_q.reshape(1, Q_LORA),
        "norm_kv": norm_kv.reshape(1, KV_LORA),
        "norm_attn_out": norm_attn_out.reshape(1, MLA_WIDTH),
        "norm_attn_out_col": norm_attn_out.reshape(MLA_WIDTH, 1),
        "norm_ssm_out": norm_ssm_out.reshape(1, S5_WIDTH),
        "b_glu": b_glu.reshape(1, S5_WIDTH),
    }


def kernel(x_prompt, x_sample, c_prompt, c_sample, cache_ckv, cache_krope, state_s5_re, state_s5_im, page_table, w_ada, b_ada, norm_ffn1, ffn1_w1, ffn1_w3, ffn1_w2, norm_mix, w_in, norm_q, w_uq, norm_kv, w_uk, w_uv, s5_a_re, s5_a_im, s5_log_dt, s5_b_re, s5_b_im, s5_c_re, s5_c_im, s5_d, w_glu, b_glu, norm_attn_out, norm_ssm_out, w_out, norm_ffn2, ffn2_w1, ffn2_w3, ffn2_w2, norm_final):
    bp, seq, _ = x_prompt.shape
    bs = x_sample.shape[0]
    depth = w_ada.shape[0]
    assert depth == 1 and x_sample.shape[1] == 1
    n_pages = page_table.shape[1]
    past_len = n_pages * PAGE
    l = 0

    pad = (-(bs + bp)) % 8
    c_all = jnp.concatenate([c_sample, c_prompt, jnp.zeros((pad, D_MODEL), F32)], axis=0)
    mod = _ada(c_all, w_ada[l], b_ada[l])
    mod_s = mod[:bs].reshape(1, bs, ADA_CHUNKS * D_MODEL)
    mod_p = mod[bs:bs + bp].reshape(bp, 1, ADA_CHUNKS * D_MODEL)

    p = _layer_params(w_in[l], w_uq[l], w_uk[l], w_uv[l], w_glu[l], w_out[l], norm_mix[l], norm_q[l], norm_kv[l],
                      norm_attn_out[l], norm_ssm_out[l], b_glu[l])
    ops = _s5_operators(s5_a_re[l], s5_a_im[l], s5_log_dt[l], s5_b_re[l], s5_b_im[l], s5_c_re[l], s5_c_im[l],
                        s5_d[l])
    cos_p, sin_p = _rope_tables(jnp.arange(seq))
    cos_s, sin_s = _rope_tables(jnp.full((bs,), past_len))

    xs = x_sample.reshape(bs, D_MODEL)
    xs, *f1 = _ffn_cast(xs, mod_s, 0, norm_ffn1[l], ffn1_w1[l], ffn1_w3[l], ffn1_w2[l])
    q_lat_ts, q_rope_ts, ckv_s, _, _, krope_t_s, _, u_s = _mixin(xs, mod_s, 1, bs, p, cos_s, sin_s, tm=bs, tq=bs)
    krope_s = krope_t_s[0].T
    q_lat_s = q_lat_ts.reshape(KV_LORA, MLA_HEADS, bs).transpose(2, 1, 0)
    q_rope_s = q_rope_ts.reshape(QK_ROPE, MLA_HEADS, bs).transpose(2, 1, 0)
    o_lat = _sample_attn(page_table, q_lat_s, q_rope_s,
                         ckv_s.reshape(bs, 1, KV_LORA), krope_s.reshape(bs, 1, QK_ROPE), cache_ckv[l:l + 1],
                         cache_krope[l:l + 1].transpose(0, 1, 3, 2))
    attn_s = _sample_attn_out(o_lat.transpose(1, 0, 2), p["w_uv_h"], p["norm_attn_out"])
    y_s, hre_s, him_s = _s5_sample(u_s.reshape(bs, S5_WIDTH), state_s5_re[l], state_s5_im[l], ops)
    xs = _mixout(xs, attn_s, y_s, mod_s, 1, p, tm=bs)
    y_sample, *f2 = _ffn_cast(xs, mod_s, 6, norm_ffn2[l], ffn2_w1[l], ffn2_w3[l], ffn2_w2[l], norm_final)
    y_sample = y_sample.reshape(bs, 1, D_MODEL)

    tm_p = TILES.ffn_rows
    xp = x_prompt.reshape(bp * seq, D_MODEL)
    xp = _ffn(xp, mod_p, 0, seq // tm_p, norm_ffn1[l], *f1, tm=tm_p)
    tm_mix = TILES.mix_rows
    tq_p = TILES.attn_q
    q_lat_t, q_rope_t, ckv_p, kc_p, kct_p, krope_t_p, kr_p, u_p = _mixin(xp, mod_p, seq // tm_mix, seq, p, cos_p,
                                                                         sin_p, tm=tm_mix, tq=tq_p)
    attn_p = _prompt_attn(q_lat_t, q_rope_t, kc_p, kr_p, kct_p, p["w_uv_h"], p["norm_attn_out_col"], tq=tq_p,
                          tk=TILES.attn_k)
    y_p, hre_p, him_p = _s5_prompt(u_p, ops)
    xp = _mixout(xp, attn_p.reshape(bp * seq, MLA_WIDTH), y_p.reshape(bp * seq, S5_WIDTH), mod_p, seq // tm_mix, p,
                 tm=tm_mix)
    y_prompt = _ffn(xp, mod_p, 6, seq // tm_p, norm_ffn2[l], *f2, norm_final, tm=tm_p).reshape(bp, seq, D_MODEL)

    return (y_prompt, y_sample,
            ckv_p[None], krope_t_p.transpose(0, 2, 1)[None],
            ckv_s.reshape(1, bs, 1, KV_LORA), krope_s.reshape(1, bs, 1, QK_ROPE),
            hre_p[None], him_p[None], hre_s[None], him_s[None])
```

```python
import functools
from typing import NamedTuple

import jax
import jax.numpy as jnp
from jax import lax
from jax.experimental import pallas as pl
from jax.experimental.pallas import tpu as pltpu

F32 = jnp.float32
BF16 = jnp.bfloat16

D_MODEL = 2048
D_FF = 5632
MLA_HEADS = 8
QK_NOPE = 128
QK_ROPE = 64
V_HEAD = 128
Q_LORA = 512
KV_LORA = 256
MLA_WIDTH = 1024
S5_WIDTH = 1024
S5_GROUP = 16
S5_GROUPS = 64
S5_STATE = 64
ADA_CHUNKS = 9
PAGE = 128
ROPE_THETA = 10000.0
SOFTMAX_SCALE = (QK_NOPE + QK_ROPE) ** -0.5
EPS = 1e-6

LANES = 128
S5_CHUNK = 16
GROUPS_PER_SLAB = LANES // S5_GROUP
N_SLABS = S5_WIDTH // LANES
SLAB_STATE = GROUPS_PER_SLAB * S5_STATE
CHUNK_COLS = S5_CHUNK * LANES
MXU_TILE = 2 * LANES
VMEM_LIMIT = 56 * 1024 * 1024


class _Tiles(NamedTuple):
    ada_cols: int = 1024
    ffn_rows: int = 512
    ffn_cols: int = 512
    mix_rows: int = 512
    attn_q: int = 512
    attn_k: int = 512
    pages: int = 32
    streams: int = 2
    slots: int = 3


TILES = _Tiles()


def _cparams(sem):
    return pltpu.CompilerParams(dimension_semantics=sem, vmem_limit_bytes=VMEM_LIMIT)


def _rms(x, g):
    return x * lax.rsqrt(jnp.mean(x * x, axis=-1, keepdims=True) + EPS) * g


def _rms_rows(x, g):
    return x * lax.rsqrt(jnp.mean(x * x, axis=0, keepdims=True) + EPS) * g


def _dot(a, b):
    return jnp.dot(a, b, preferred_element_type=F32)


def _dot_nt(a, b):
    return lax.dot_general(a, b, (((1,), (1,)), ((), ())), preferred_element_type=F32)


def _ada_kernel(c_ref, w_ref, b_ref, o_ref):
    c = c_ref[...]
    a = (c * jax.nn.sigmoid(c)).astype(BF16)
    o_ref[...] = _dot(a, w_ref[...].astype(BF16)) + b_ref[...]


def _ada(c, w_ada, b_ada):
    rows = c.shape[0]
    n = w_ada.shape[1]
    tn = TILES.ada_cols
    return pl.pallas_call(
        _ada_kernel,
        grid=(n // tn,),
        in_specs=[
            pl.BlockSpec((rows, D_MODEL), lambda j: (0, 0)),
            pl.BlockSpec((D_MODEL, tn), lambda j: (0, j)),
            pl.BlockSpec((1, tn), lambda j: (0, j)),
        ],
        out_specs=pl.BlockSpec((rows, tn), lambda j: (0, j)),
        out_shape=jax.ShapeDtypeStruct((rows, n), F32),
        compiler_params=_cparams(("arbitrary",)),
        name="ada_modulation",
    )(c, w_ada, b_ada.reshape(1, n))


def _ffn_kernel(x_ref, sh_ref, sc_ref, gt_ref, g_ref, w1_ref, w3_ref, w2_ref, w2_last_ref, *rest, final_norm):
    if final_norm:
        gf_ref, o_ref, h_ref, act_ref = rest
    else:
        o_ref, h_ref, act_ref = rest
    j = pl.program_id(1)
    last = pl.num_programs(1) - 1

    def gate():
        h = h_ref[...]
        a = _dot(h, w1_ref[...])
        b = _dot(h, w3_ref[...])
        act_ref[...] = (a * jax.nn.sigmoid(a) * b).astype(BF16)

    def down():
        return _dot(act_ref[...], w2_ref[...])

    @pl.when(j == 0)
    def _():
        h = _rms(x_ref[...], g_ref[...]) * (1.0 + sc_ref[0]) + sh_ref[0]
        h_ref[...] = h.astype(BF16)
        gate()

    @pl.when(j == 1)
    def _():
        o_ref[...] = down()
        gate()

    @pl.when((j > 1) & (j < last))
    def _():
        o_ref[...] += down()
        gate()

    @pl.when(j == last)
    def _():
        prev = down()
        gate()
        y = x_ref[...] + 0.5 * gt_ref[0] * (o_ref[...] + prev + _dot(act_ref[...], w2_last_ref[...]))
        if final_norm:
            y = _rms(y, gf_ref[...])
        o_ref[...] = y


def _ffn_cast_kernel(x_ref, sh_ref, sc_ref, gt_ref, g_ref, w1_ref, w3_ref, w2_ref, *rest, final_norm):
    if final_norm:
        gf_ref, o_ref, w1b_ref, w3b_ref, w2b_ref, h_ref = rest
    else:
        o_ref, w1b_ref, w3b_ref, w2b_ref, h_ref = rest
    j = pl.program_id(0)

    @pl.when(j == 0)
    def _():
        h = _rms(x_ref[...], g_ref[...]) * (1.0 + sc_ref[0]) + sh_ref[0]
        h_ref[...] = h.astype(BF16)
        o_ref[...] = jnp.zeros_like(o_ref)

    w1b = w1_ref[...].astype(BF16)
    w3b = w3_ref[...].astype(BF16)
    w2b = w2_ref[...].astype(BF16)
    w1b_ref[...] = w1b
    w3b_ref[...] = w3b
    w2b_ref[...] = w2b
    h = h_ref[...]
    a = _dot(h, w1b)
    act = (a * jax.nn.sigmoid(a) * _dot(h, w3b)).astype(BF16)
    o_ref[...] += _dot(act, w2b)

    @pl.when(j == pl.num_programs(0) - 1)
    def _():
        y = x_ref[...] + 0.5 * gt_ref[0] * o_ref[...]
        if final_norm:
            y = _rms(y, gf_ref[...])
        o_ref[...] = y


def _ffn_cast(x, mod, chunk0, g, w1, w3, w2, gf=None, *, tf=TILES.ffn_cols):
    rows = x.shape[0]
    const = lambda shape: pl.BlockSpec(shape, lambda j: (0,) * len(shape))
    mod_spec = lambda k: pl.BlockSpec((1, rows, D_MODEL), lambda j: (0, 0, k))
    up = pl.BlockSpec((D_MODEL, tf), lambda j: (0, j))
    down = pl.BlockSpec((tf, D_MODEL), lambda j: (j, 0))
    in_specs = [const((rows, D_MODEL)), mod_spec(chunk0), mod_spec(chunk0 + 1), mod_spec(chunk0 + 2),
                const((1, D_MODEL)), up, up, down]
    args = [x, mod, mod, mod, g.reshape(1, D_MODEL), w1, w3, w2]
    if gf is not None:
        in_specs.append(const((1, D_MODEL)))
        args.append(gf.reshape(1, D_MODEL))
    return pl.pallas_call(
        functools.partial(_ffn_cast_kernel, final_norm=gf is not None),
        grid=(D_FF // tf,),
        in_specs=in_specs,
        out_specs=[const((rows, D_MODEL)), up, up, down],
        out_shape=[jax.ShapeDtypeStruct((rows, D_MODEL), F32), jax.ShapeDtypeStruct(w1.shape, BF16),
                   jax.ShapeDtypeStruct(w3.shape, BF16), jax.ShapeDtypeStruct(w2.shape, BF16)],
        scratch_shapes=[pltpu.VMEM((rows, D_MODEL), BF16)],
        compiler_params=_cparams(("arbitrary",)),
        name="sample_ffn_cast",
    )(*args)


def _ffn(x, mod, chunk0, tiles_per_b, g, w1, w3, w2, gf=None, *, tm, tf=TILES.ffn_cols):
    n = x.shape[0]
    r = mod.shape[1]
    n_ff = D_FF // tf
    assert n_ff >= 3

    def mod_spec(k):
        return pl.BlockSpec((1, r, D_MODEL), lambda i, j: (i // tiles_per_b, 0, k))

    in_specs = [
        pl.BlockSpec((tm, D_MODEL), lambda i, j: (i, 0)),
        mod_spec(chunk0), mod_spec(chunk0 + 1), mod_spec(chunk0 + 2),
        pl.BlockSpec((1, D_MODEL), lambda i, j: (0, 0)),
        pl.BlockSpec((D_MODEL, tf), lambda i, j: (0, j)),
        pl.BlockSpec((D_MODEL, tf), lambda i, j: (0, j)),
        pl.BlockSpec((tf, D_MODEL), lambda i, j: (jnp.maximum(j - 1, 0), 0)),
        pl.BlockSpec((tf, D_MODEL), lambda i, j: (n_ff - 1, 0)),
    ]
    args = [x, mod, mod, mod, g.reshape(1, D_MODEL), w1, w3, w2, w2]
    if gf is not None:
        in_specs.append(pl.BlockSpec((1, D_MODEL), lambda i, j: (0, 0)))
        args.append(gf.reshape(1, D_MODEL))
    return pl.pallas_call(
        functools.partial(_ffn_kernel, final_norm=gf is not None),
        grid=(n // tm, n_ff),
        in_specs=in_specs,
        out_specs=pl.BlockSpec((tm, D_MODEL), lambda i, j: (i, 0)),
        out_shape=jax.ShapeDtypeStruct((n, D_MODEL), F32),
        scratch_shapes=[pltpu.VMEM((tm, D_MODEL), BF16), pltpu.VMEM((tm, tf), BF16)],
        compiler_params=_cparams(("parallel", "arbitrary")),
        name="macaron_ffn",
    )(*args)


def _rope_rows(x, cos, sin):
    half = QK_ROPE // 2
    x1 = x[:half]
    x2 = x[half:]
    return x1 * cos - x2 * sin, x1 * sin + x2 * cos


def _mixin_kernel(x_ref, sh_ref, sc_ref, g_ref, wq_ref, wkv_ref, wkr_ref, wu_ref, gq_ref, wqn_ref, wqr_ref,
                  wuk_ref, gkv_ref, cos_ref, sin_ref,
                  qlat_ref, qrope_ref, ckv_ref, kc_ref, kct_ref, krope_ref, kr_ref, u_ref, *, tq):
    half = QK_ROPE // 2
    n_q = x_ref.shape[0] // tq
    h = (_rms(x_ref[...], g_ref[...]) * (1.0 + sc_ref[0]) + sh_ref[0]).astype(BF16)
    cos = cos_ref[...]
    sin = sin_ref[...]

    u_ref[0] = _dot(h, wu_ref[...])

    ckv = _rms(_dot(h, wkv_ref[...]), gkv_ref[...])
    ckv_ref[0] = ckv
    kc_ref[0] = ckv.astype(BF16)
    kct_ref[0] = ckv.T.astype(BF16)

    k1, k2 = _rope_rows(_dot_nt(wkr_ref[...], h), cos, sin)
    kr_t = jnp.concatenate([k1, k2], axis=0)
    krope_ref[0] = kr_t
    kr_ref[0] = kr_t.T.astype(BF16)

    qn = _rms(_dot(h, wq_ref[...]), gq_ref[...]).astype(BF16)
    q_nope = (_dot(qn, wqn_ref[...]) * SOFTMAX_SCALE).astype(BF16)
    q_rope_t = _dot_nt(wqr_ref[...], qn) * SOFTMAX_SCALE
    for hd in range(MLA_HEADS):
        ql_t = _dot_nt(wuk_ref[hd], q_nope[:, hd * QK_NOPE:(hd + 1) * QK_NOPE]).astype(BF16)
        r1, r2 = _rope_rows(q_rope_t[hd * QK_ROPE:(hd + 1) * QK_ROPE], cos, sin)
        r1 = r1.astype(BF16)
        r2 = r2.astype(BF16)
        for qq in range(n_q):
            src = slice(qq * tq, (qq + 1) * tq)
            dst = slice(hd * tq, (hd + 1) * tq)
            qlat_ref[0, qq, :, dst] = ql_t[:, src]
            qrope_ref[0, qq, :half, dst] = r1[:, src]
            qrope_ref[0, qq, half:, dst] = r2[:, src]


def _mixin(x, mod, tiles_per_b, t_len, p, cos_t, sin_t, *, tm, tq):
    n = x.shape[0]
    nb = n // t_len
    r = mod.shape[1]
    half = QK_ROPE // 2
    full = lambda shape: pl.BlockSpec(shape, lambda i: (0,) * len(shape))
    rows = lambda width: pl.BlockSpec((1, tm, width), lambda i: (i // tiles_per_b, i % tiles_per_b, 0))
    cols = lambda height: pl.BlockSpec((1, height, tm), lambda i: (i // tiles_per_b, 0, i % tiles_per_b))
    head_cols = lambda height: pl.BlockSpec((1, tm // tq, height, MLA_HEADS * tq),
                                            lambda i: (i // tiles_per_b, i % tiles_per_b, 0, 0))
    in_specs = [
        pl.BlockSpec((tm, D_MODEL), lambda i: (i, 0)),
        pl.BlockSpec((1, r, D_MODEL), lambda i: (i // tiles_per_b, 0, 3)),
        pl.BlockSpec((1, r, D_MODEL), lambda i: (i // tiles_per_b, 0, 4)),
        full((1, D_MODEL)),
        full((D_MODEL, Q_LORA)), full((D_MODEL, KV_LORA)), full((QK_ROPE, D_MODEL)), full((D_MODEL, S5_WIDTH)),
        full((1, Q_LORA)), full((Q_LORA, MLA_HEADS * QK_NOPE)), full((MLA_HEADS * QK_ROPE, Q_LORA)),
        full((MLA_HEADS, KV_LORA, QK_NOPE)), full((1, KV_LORA)),
        pl.BlockSpec((half, tm), lambda i: (0, i % tiles_per_b)),
        pl.BlockSpec((half, tm), lambda i: (0, i % tiles_per_b)),
    ]
    out_specs = [head_cols(KV_LORA), head_cols(QK_ROPE), rows(KV_LORA), rows(KV_LORA), cols(KV_LORA),
                 cols(QK_ROPE), rows(QK_ROPE), rows(S5_WIDTH)]
    out_shape = [
        jax.ShapeDtypeStruct((nb, t_len // tq, KV_LORA, MLA_HEADS * tq), BF16),
        jax.ShapeDtypeStruct((nb, t_len // tq, QK_ROPE, MLA_HEADS * tq), BF16),
        jax.ShapeDtypeStruct((nb, t_len, KV_LORA), F32),
        jax.ShapeDtypeStruct((nb, t_len, KV_LORA), BF16),
        jax.ShapeDtypeStruct((nb, KV_LORA, t_len), BF16),
        jax.ShapeDtypeStruct((nb, QK_ROPE, t_len), F32),
        jax.ShapeDtypeStruct((nb, t_len, QK_ROPE), BF16),
        jax.ShapeDtypeStruct((nb, t_len, S5_WIDTH), F32),
    ]
    return pl.pallas_call(
        functools.partial(_mixin_kernel, tq=tq),
        grid=(n // tm,),
        in_specs=in_specs,
        out_specs=out_specs,
        out_shape=out_shape,
        compiler_params=_cparams(("parallel",)),
        name="mixer_in",
    )(x, mod, mod, p["norm_mix"], p["w_q"], p["w_kv"], p["w_kr_t"], p["w_u"], p["norm_q"], p["w_uq_nope"],
      p["w_uq_rope_t"], p["w_uk_h"], p["norm_kv"], cos_t, sin_t)


def _prompt_attn_kernel(ql_ref, qr_ref, kc_ref, kr_ref, kct_ref, wuv_ref, g_ref, o_ref, *scratch, tq, tk):
    qi = pl.program_id(1)
    n_half = len(scratch) // 3
    m_ref, l_ref, acc_ref = scratch[:n_half], scratch[n_half:2 * n_half], scratch[2 * n_half:]
    width = m_ref[0].shape[1]
    heads_per_half = width // tq
    for hf in range(n_half):
        m_ref[hf][...] = jnp.full_like(m_ref[hf], -jnp.inf)
        l_ref[hf][...] = jnp.zeros_like(l_ref[hf])
        acc_ref[hf][...] = jnp.zeros_like(acc_ref[hf])

    def step(ki, diag):
        start = pl.multiple_of(ki * tk, tk)
        kc = kc_ref[0, pl.ds(start, tk), :]
        kr = kr_ref[0, pl.ds(start, tk), :]
        kct = kct_ref[0, :, pl.ds(start, tk)]
        masked = diag is not None
        if masked:
            k_local = lax.broadcasted_iota(jnp.int32, (tk, width), 0) + diag * tk
            t_local = lax.broadcasted_iota(jnp.int32, (tk, width), 1) & (tq - 1)
            keep = k_local <= t_local
        for hf in range(n_half):
            lanes = slice(hf * width, (hf + 1) * width)
            s = _dot(kc, ql_ref[0, 0, :, lanes]) + _dot(kr, qr_ref[0, 0, :, lanes])
            if masked:
                s = jnp.where(keep, s, -jnp.inf)
            m_old = m_ref[hf][...]
            m_new = jnp.maximum(m_old, jnp.max(s, axis=0, keepdims=True))
            alpha = jnp.exp(m_old - m_new)
            pexp = jnp.exp(s - m_new)
            l_ref[hf][...] = alpha * l_ref[hf][...] + jnp.sum(pexp, axis=0, keepdims=True)
            acc_ref[hf][...] = alpha * acc_ref[hf][...] + _dot(kct, pexp.astype(BF16))
            m_ref[hf][...] = m_new

    def body(ki, carry):
        step(ki, None)
        return carry

    blocks_per_tile = tq // tk
    lax.fori_loop(0, qi * blocks_per_tile, body, 0)
    for d in range(blocks_per_tile):
        step(qi * blocks_per_tile + d, d)
    parts = []
    for hd in range(MLA_HEADS):
        hf = hd // heads_per_half
        lanes = slice((hd % heads_per_half) * tq, (hd % heads_per_half + 1) * tq)
        o_t = acc_ref[hf][:, lanes] / l_ref[hf][:, lanes]
        parts.append(_dot(wuv_ref[hd], o_t.astype(BF16)))
    attn_t = _rms_rows(jnp.concatenate(parts, axis=0), g_ref[...])
    o_ref[0] = attn_t.T.astype(BF16)


def _prompt_attn(q_lat_t, q_rope_t, kc, kr, kc_t, w_uv_h, g_col, *, tq, tk, n_half=1):
    nb, n_q, _, _ = q_lat_t.shape
    t_len = n_q * tq
    width = MLA_HEADS * tq // n_half
    assert tq % tk == 0
    return pl.pallas_call(
        functools.partial(_prompt_attn_kernel, tq=tq, tk=tk),
        grid=(nb, n_q),
        in_specs=[
            pl.BlockSpec((1, 1, KV_LORA, MLA_HEADS * tq), lambda b, i: (b, i, 0, 0)),
            pl.BlockSpec((1, 1, QK_ROPE, MLA_HEADS * tq), lambda b, i: (b, i, 0, 0)),
            pl.BlockSpec((1, t_len, KV_LORA), lambda b, i: (b, 0, 0)),
            pl.BlockSpec((1, t_len, QK_ROPE), lambda b, i: (b, 0, 0)),
            pl.BlockSpec((1, KV_LORA, t_len), lambda b, i: (b, 0, 0)),
            pl.BlockSpec((MLA_HEADS, V_HEAD, KV_LORA), lambda b, i: (0, 0, 0)),
            pl.BlockSpec((MLA_WIDTH, 1), lambda b, i: (0, 0)),
        ],
        out_specs=pl.BlockSpec((1, tq, MLA_WIDTH), lambda b, i: (b, i, 0)),
        out_shape=jax.ShapeDtypeStruct((nb, t_len, MLA_WIDTH), BF16),
        scratch_shapes=([pltpu.VMEM((1, width), F32)] * (2 * n_half) + [pltpu.VMEM((KV_LORA, width), F32)] * n_half),
        compiler_params=_cparams(("parallel", "parallel")),
        name="prompt_attention",
    )(q_lat_t, q_rope_t, kc, kr, kc_t, w_uv_h, g_col)


def _sample_attn_kernel(pt_ref, ql_ref, qr_ref, kcn_ref, krn_ref, ckv_hbm, kr_hbm, o_ref, kbuf, rbuf, sem_k, sem_r,
                        *, pages, streams):
    nb, n_pages = pt_ref.shape
    n_chunks = n_pages // pages
    n_groups = nb // streams
    total = n_groups * n_chunks
    n_slots = kbuf.shape[0]
    ahead = n_slots - 1

    def chunk_copies(g, slot):
        grp = g // n_chunks
        c = g % n_chunks
        copies = []
        for st in range(streams):
            b = grp * streams + st
            for i in range(pages):
                page = pt_ref[b, c * pages + i]
                copies.append(pltpu.make_async_copy(
                    ckv_hbm.at[0, page], kbuf.at[slot, st, pl.ds(i * PAGE, PAGE), :], sem_k.at[slot, st]))
                copies.append(pltpu.make_async_copy(
                    kr_hbm.at[0, page], rbuf.at[slot, st, i], sem_r.at[slot, st]))
        return copies

    grp = pl.program_id(0)

    @pl.when(grp == 0)
    def _():
        for g0 in range(ahead):
            for cp in chunk_copies(g0, g0):
                cp.start()

    qs = [(ql_ref[st], qr_ref[st]) for st in range(streams)]

    def chunk_body(c, carry):
        g = grp * n_chunks + c
        slot = g % n_slots

        @pl.when(g + ahead < total)
        def _():
            for cp in chunk_copies(g + ahead, (g + ahead) % n_slots):
                cp.start()

        for cp in chunk_copies(g, slot):
            cp.wait()
        out = []
        for st in range(streams):
            m_old, l_old, acc_old = carry[st]
            ql, qr = qs[st]
            kc = kbuf[slot, st].astype(BF16)
            kr_t = jnp.concatenate([rbuf[slot, st, i].astype(BF16) for i in range(pages)], axis=1)
            s = _dot_nt(ql, kc) + _dot(qr, kr_t)
            m_new = jnp.maximum(m_old, jnp.max(s, axis=1, keepdims=True))
            alpha = jnp.exp(m_old - m_new)
            pexp = jnp.exp(s - m_new)
            l_new = alpha * l_old + jnp.sum(pexp, axis=1, keepdims=True)
            acc_new = alpha * acc_old + _dot(pexp.astype(BF16), kc)
            out.append((m_new, l_new, acc_new))
        return tuple(out)

    init = tuple((jnp.full((MLA_HEADS, 1), -jnp.inf, F32), jnp.zeros((MLA_HEADS, 1), F32),
                  jnp.zeros((MLA_HEADS, KV_LORA), F32)) for _ in range(streams))
    final = lax.fori_loop(0, n_chunks, chunk_body, init)
    for st in range(streams):
        m_old, l_old, acc_old = final[st]
        ql, qr = qs[st]
        kcn = kcn_ref[st].astype(BF16).astype(F32)
        krn = krn_ref[st].astype(BF16).astype(F32)
        s_n = (jnp.sum(ql.astype(F32) * kcn, axis=1, keepdims=True)
               + jnp.sum(qr.astype(F32) * krn, axis=1, keepdims=True))
        m_f = jnp.maximum(m_old, s_n)
        a_f = jnp.exp(m_old - m_f)
        p_n = jnp.exp(s_n - m_f)
        l_f = a_f * l_old + p_n
        acc_f = a_f * acc_old + p_n.astype(BF16).astype(F32) * kcn
        o_ref[st] = acc_f / l_f


def _sample_attn(page_table, q_lat, q_rope, ckv_new, krope_new, cache_ckv, cache_krope_t, *, pages=TILES.pages,
                 streams=TILES.streams, slots=TILES.slots):
    nb, n_pages = page_table.shape
    assert n_pages % pages == 0 and nb % streams == 0 and (nb // streams) * (n_pages // pages) >= slots
    per_group = lambda rows, width: pl.BlockSpec((streams, rows, width), lambda g, pt: (g, 0, 0))
    grid_spec = pltpu.PrefetchScalarGridSpec(
        num_scalar_prefetch=1,
        grid=(nb // streams,),
        in_specs=[
            per_group(MLA_HEADS, KV_LORA), per_group(MLA_HEADS, QK_ROPE),
            per_group(1, KV_LORA), per_group(1, QK_ROPE),
            pl.BlockSpec(memory_space=pl.ANY), pl.BlockSpec(memory_space=pl.ANY),
        ],
        out_specs=per_group(MLA_HEADS, KV_LORA),
        scratch_shapes=[
            pltpu.VMEM((slots, streams, pages * PAGE, KV_LORA), F32),
            pltpu.VMEM((slots, streams, pages, QK_ROPE, PAGE), F32),
            pltpu.SemaphoreType.DMA((slots, streams)),
            pltpu.SemaphoreType.DMA((slots, streams)),
        ],
    )
    return pl.pallas_call(
        functools.partial(_sample_attn_kernel, pages=pages, streams=streams),
        grid_spec=grid_spec,
        out_shape=jax.ShapeDtypeStruct((nb, MLA_HEADS, KV_LORA), F32),
        compiler_params=_cparams(("arbitrary",)),
        name="sample_paged_attention",
    )(page_table, q_lat, q_rope, ckv_new, krope_new, cache_ckv, cache_krope_t)


def _sample_attn_out_kernel(o_ref, wuv_ref, g_ref, out_ref):
    parts = [_dot_nt(o_ref[hd].astype(BF16), wuv_ref[hd]) for hd in range(MLA_HEADS)]
    out_ref[...] = _rms(jnp.concatenate(parts, axis=1), g_ref[...]).astype(BF16)


def _sample_attn_out(o_lat_h, w_uv_h, g):
    nb = o_lat_h.shape[1]
    return pl.pallas_call(
        _sample_attn_out_kernel,
        out_shape=jax.ShapeDtypeStruct((nb, MLA_WIDTH), BF16),
        name="sample_attention_out",
    )(o_lat_h, w_uv_h, g)


def _same_group(shape, row_shift, col_shift):
    rows = lax.broadcasted_iota(jnp.int32, shape, 0)
    cols = lax.broadcasted_iota(jnp.int32, shape, 1)
    mask = GROUPS_PER_SLAB - 1
    return ((rows >> row_shift) & mask) == ((cols >> col_shift) & mask)


def _s5_prompt_kernel(u_ref, wr_ref, we_ref, wf_ref, tr_ref, te_ref, tf_ref, are_ref, aim_ref, d_ref,
                      y_ref, hre_ref, him_ref, r_ref, e_ref, f_ref, uc_ref, s_ref, hp_ref, yc_ref):
    n_chunks = uc_ref.shape[0]
    n_blk = CHUNK_COLS // MXU_TILE
    ch_bits = S5_GROUP.bit_length() - 1
    st_bits = S5_STATE.bit_length() - 1

    @pl.when(pl.program_id(1) == 0)
    def _():
        zero = jnp.zeros((), F32)
        r_ref[...] = jnp.where(_same_group(r_ref.shape, ch_bits, ch_bits),
                               _dot(wr_ref[0], tr_ref[...]), zero).astype(BF16)
        e_ref[...] = jnp.where(_same_group(e_ref.shape, ch_bits, st_bits),
                               _dot(we_ref[0], te_ref[...]), zero).astype(BF16)
        f_ref[...] = jnp.where(_same_group(f_ref.shape, st_bits, ch_bits),
                               _dot(tf_ref[...], wf_ref[0]), zero).astype(BF16)

    for s in range(S5_CHUNK):
        uc_ref[:, s * LANES:(s + 1) * LANES] = u_ref[0, pl.ds(s, n_chunks, stride=S5_CHUNK), :]
    uc = uc_ref[...]
    ub = uc.astype(BF16)

    s_ref[...] = _dot(ub, e_ref[...])

    a_re = are_ref[0]
    a_im = aim_ref[0]

    def scan(k, carry):
        h_re, h_im = carry
        hp_ref[pl.ds(k, 1), :] = jnp.concatenate([h_re, h_im], axis=1)
        row = s_ref[pl.ds(k, 1), :]
        n_re = a_re * h_re - a_im * h_im + row[:, :SLAB_STATE]
        n_im = a_re * h_im + a_im * h_re + row[:, SLAB_STATE:]
        return n_re, n_im

    zero_state = jnp.zeros((1, SLAB_STATE), F32)
    h_re, h_im = lax.fori_loop(0, n_chunks, scan, (zero_state, zero_state), unroll=8)
    hre_ref[0, 0] = h_re
    him_ref[0, 0] = h_im

    yc_ref[...] = _dot(hp_ref[...].astype(BF16), f_ref[...]) + uc * d_ref[0]
    for tb in range(n_blk):
        yc_ref[:, tb * MXU_TILE:(tb + 1) * MXU_TILE] += _dot(ub[:, :(tb + 1) * MXU_TILE],
                                                             r_ref[(n_blk - 1 - tb) * MXU_TILE:, :])
    y = jax.nn.gelu(yc_ref[...])
    for t in range(S5_CHUNK):
        y_ref[0, pl.ds(t, n_chunks, stride=S5_CHUNK), :] = y[:, t * LANES:(t + 1) * LANES]


def _s5_prompt(u, ops):
    nb, t_len, _ = u.shape
    n_chunks = t_len // S5_CHUNK
    slab = lambda shape: pl.BlockSpec((1,) + shape, lambda j, b: (j,) + (0,) * len(shape))
    full = lambda shape: pl.BlockSpec(shape, lambda j, b: (0,) * len(shape))
    y, hre, him = pl.pallas_call(
        _s5_prompt_kernel,
        grid=(N_SLABS, nb),
        in_specs=[
            pl.BlockSpec((1, t_len, LANES), lambda j, b: (b, 0, j)),
            slab((CHUNK_COLS, 2 * S5_GROUP)), slab((CHUNK_COLS, 2 * S5_STATE)), slab((2 * S5_STATE, CHUNK_COLS)),
            full((2 * S5_GROUP, MXU_TILE)), full((2 * S5_STATE, 2 * SLAB_STATE)), full((2 * SLAB_STATE, 2 * S5_STATE)),
            slab((1, SLAB_STATE)), slab((1, SLAB_STATE)), slab((1, CHUNK_COLS)),
        ],
        out_specs=[
            pl.BlockSpec((1, t_len, LANES), lambda j, b: (b, 0, j)),
            pl.BlockSpec((1, 1, 1, SLAB_STATE), lambda j, b: (j, b, 0, 0)),
            pl.BlockSpec((1, 1, 1, SLAB_STATE), lambda j, b: (j, b, 0, 0)),
        ],
        out_shape=[
            jax.ShapeDtypeStruct((nb, t_len, S5_WIDTH), F32),
            jax.ShapeDtypeStruct((N_SLABS, nb, 1, SLAB_STATE), F32),
            jax.ShapeDtypeStruct((N_SLABS, nb, 1, SLAB_STATE), F32),
        ],
        scratch_shapes=[
            pltpu.VMEM((CHUNK_COLS, MXU_TILE), BF16),
            pltpu.VMEM((CHUNK_COLS, 2 * SLAB_STATE), BF16),
            pltpu.VMEM((2 * SLAB_STATE, CHUNK_COLS), BF16),
            pltpu.VMEM((n_chunks, CHUNK_COLS), F32),
            pltpu.VMEM((n_chunks, 2 * SLAB_STATE), F32),
            pltpu.VMEM((n_chunks, 2 * SLAB_STATE), F32),
            pltpu.VMEM((n_chunks, CHUNK_COLS), F32),
        ],
        compiler_params=_cparams(("arbitrary", "arbitrary")),
        name="s5_prompt",
    )(u, ops["wr"], ops["we"], ops["wf"], ops["tile_r"], ops["tile_e"], ops["tile_f"], ops["a_re"], ops["a_im"],
      ops["d_chunk"])

    def to_state(h):
        return h.reshape(N_SLABS, nb, GROUPS_PER_SLAB, S5_STATE).transpose(1, 0, 2, 3).reshape(nb, S5_GROUPS, S5_STATE)

    return y, to_state(hre), to_state(him)


def _split_bf16(x):
    hi = x.astype(BF16)
    return hi, (x - hi.astype(F32)).astype(BF16)


def _s5_sample_kernel(u_ref, h0re_ref, h0im_ref, bd_hi_ref, bd_lo_ref, cd_ref, lre_ref, lim_ref, d_ref,
                      y_ref, hre_ref, him_ref):
    u = u_ref[...]
    u_hi, u_lo = _split_bf16(u)
    bu = _dot(u_hi, bd_hi_ref[0]) + (_dot(u_hi, bd_lo_ref[0]) + _dot(u_lo, bd_hi_ref[0]))
    l_re = lre_ref[0]
    l_im = lim_ref[0]
    h0_re = h0re_ref[...]
    h0_im = h0im_ref[...]
    h_re = l_re * h0_re - l_im * h0_im + bu[:, :SLAB_STATE]
    h_im = l_re * h0_im + l_im * h0_re + bu[:, SLAB_STATE:]
    hre_ref[...] = h_re
    him_ref[...] = h_im
    h = jnp.concatenate([h_re, h_im], axis=1).astype(BF16)
    y_ref[...] = jax.nn.gelu(_dot(h, cd_ref[0]) + u * d_ref[0])


def _s5_sample(u, h0_re, h0_im, ops):
    nb = u.shape[0]
    n_state = S5_GROUPS * S5_STATE
    slab = lambda shape: pl.BlockSpec((1,) + shape, lambda j: (j,) + (0,) * len(shape))
    col = lambda width: pl.BlockSpec((nb, width), lambda j: (0, j))
    y, hre, him = pl.pallas_call(
        _s5_sample_kernel,
        grid=(N_SLABS,),
        in_specs=[
            col(LANES), col(SLAB_STATE), col(SLAB_STATE),
            slab((LANES, 2 * SLAB_STATE)), slab((LANES, 2 * SLAB_STATE)), slab((2 * SLAB_STATE, LANES)),
            slab((1, SLAB_STATE)), slab((1, SLAB_STATE)), slab((1, LANES)),
        ],
        out_specs=[col(LANES), col(SLAB_STATE), col(SLAB_STATE)],
        out_shape=[
            jax.ShapeDtypeStruct((nb, S5_WIDTH), F32),
            jax.ShapeDtypeStruct((nb, n_state), F32),
            jax.ShapeDtypeStruct((nb, n_state), F32),
        ],
        compiler_params=_cparams(("parallel",)),
        name="s5_sample",
    )(u, h0_re.reshape(nb, n_state), h0_im.reshape(nb, n_state), ops["bd_hi"], ops["bd_lo"], ops["cd"],
      ops["l_re"], ops["l_im"], ops["d_slab"])
    return y, hre.reshape(nb, S5_GROUPS, S5_STATE), him.reshape(nb, S5_GROUPS, S5_STATE)


def _mixout_kernel(x_ref, attn_ref, y_ref, gt_ref, wglu_ref, bglu_ref, gs_ref, woa_ref, woy_ref, o_ref):
    y = y_ref[...]
    z = _dot(y.astype(BF16), wglu_ref[...]) + bglu_ref[...]
    yn = _rms(y * jax.nn.sigmoid(z), gs_ref[...]).astype(BF16)
    mix = _dot(attn_ref[...], woa_ref[...]) + _dot(yn, woy_ref[...])
    o_ref[...] = x_ref[...] + gt_ref[0] * mix


def _mixout(x, attn, y, mod, tiles_per_b, p, *, tm):
    n = x.shape[0]
    r = mod.shape[1]
    full = lambda shape: pl.BlockSpec(shape, lambda i: (0,) * len(shape))
    return pl.pallas_call(
        _mixout_kernel,
        grid=(n // tm,),
        in_specs=[
            pl.BlockSpec((tm, D_MODEL), lambda i: (i, 0)),
            pl.BlockSpec((tm, MLA_WIDTH), lambda i: (i, 0)),
            pl.BlockSpec((tm, S5_WIDTH), lambda i: (i, 0)),
            pl.BlockSpec((1, r, D_MODEL), lambda i: (i // tiles_per_b, 0, 5)),
            full((S5_WIDTH, S5_WIDTH)), full((1, S5_WIDTH)), full((1, S5_WIDTH)),
            full((MLA_WIDTH, D_MODEL)), full((S5_WIDTH, D_MODEL)),
        ],
        out_specs=pl.BlockSpec((tm, D_MODEL), lambda i: (i, 0)),
        out_shape=jax.ShapeDtypeStruct((n, D_MODEL), F32),
        compiler_params=_cparams(("parallel",)),
        name="mixer_out",
    )(x, attn, y, mod, p["w_glu"], p["b_glu"], p["norm_ssm_out"], p["w_out_attn"], p["w_out_ssm"])


def _rope_tables(pos):
    half = QK_ROPE // 2
    inv_freq = ROPE_THETA ** (-jnp.arange(half, dtype=F32) / half)
    ang = pos.astype(F32)[:, None] * inv_freq[None, :]
    return jnp.cos(ang).T, jnp.sin(ang).T


def _group_diag(x, g_axis, new_axis):
    x = jnp.expand_dims(x, new_axis)
    shape = [1] * x.ndim
    shape[g_axis if g_axis < new_axis else g_axis + 1] = GROUPS_PER_SLAB
    shape[new_axis] = GROUPS_PER_SLAB
    return x * jnp.eye(GROUPS_PER_SLAB, dtype=x.dtype).reshape(shape)


def _by_slab(x, g_axis):
    return x.reshape(x.shape[:g_axis] + (N_SLABS, GROUPS_PER_SLAB) + x.shape[g_axis + 1:])


def _replicate_over_groups(n_outer, n_inner):
    eo = jnp.eye(n_outer, dtype=F32)[:, None, :, None, None]
    ei = jnp.eye(n_inner, dtype=F32)[None, :, None, None, :]
    ones = jnp.ones((1, 1, 1, GROUPS_PER_SLAB, 1), F32)
    return (eo * ei * ones).reshape(n_outer * n_inner, n_outer * GROUPS_PER_SLAB * n_inner)


def _s5_operators(a_re, a_im, log_dt, b_re, b_im, c_re, c_im, d_skip):
    hp = lax.Precision.HIGHEST
    a_re, a_im, b_re, b_im, c_re, c_im = (v.astype(F32) for v in (a_re, a_im, b_re, b_im, c_re, c_im))
    dt = jnp.exp(log_dt.astype(F32))[:, None]
    z_re = a_re * dt
    z_im = a_im * dt

    def lam_pow(n):
        mag = jnp.exp(z_re[None] * n[:, None, None])
        ang = z_im[None] * n[:, None, None]
        return mag * jnp.cos(ang), mag * jnp.sin(ang)

    steps = jnp.arange(S5_CHUNK + 1, dtype=F32)
    pw_re, pw_im = lam_pow(steps)
    lb_re, lb_im = pw_re[1], pw_im[1]
    den = a_re * a_re + a_im * a_im
    q_re = ((lb_re - 1.0) * a_re + lb_im * a_im) / den
    q_im = (lb_im * a_re - (lb_re - 1.0) * a_im) / den
    bb_re = q_re[:, :, None] * b_re - q_im[:, :, None] * b_im
    bb_im = q_re[:, :, None] * b_im + q_im[:, :, None] * b_re

    cp_re = c_re[None] * pw_re[:S5_CHUNK, :, None, :] - c_im[None] * pw_im[:S5_CHUNK, :, None, :]
    cp_im = c_re[None] * pw_im[:S5_CHUNK, :, None, :] + c_im[None] * pw_re[:S5_CHUNK, :, None, :]
    m = jnp.einsum("ngcq,gqd->ngcd", jnp.concatenate([cp_re, -cp_im], axis=3),
                   jnp.concatenate([bb_re, bb_im], axis=1), precision=hp)
    m_ext = jnp.concatenate([m, jnp.zeros_like(m[:1])], axis=0)
    n_blk = S5_CHUNK // 2
    d_i = (n_blk - 1 - jnp.arange(n_blk))[:, None, None]
    s_i = jnp.arange(2)[None, :, None]
    t_i = jnp.arange(2)[None, None, :]
    lag = 2 * d_i + t_i - s_i
    mg = _by_slab(m_ext[lag], 3)
    wr = mg.transpose(3, 0, 1, 4, 6, 2, 5).reshape(N_SLABS, CHUNK_COLS, 2 * S5_GROUP)

    pr_re, pr_im = lam_pow(S5_CHUNK - 1 - steps[:S5_CHUNK])
    w_re = pr_re[:, :, :, None] * bb_re[None] - pr_im[:, :, :, None] * bb_im[None]
    w_im = pr_re[:, :, :, None] * bb_im[None] + pr_im[:, :, :, None] * bb_re[None]

    def e_half(v):
        return _by_slab(v, 1).transpose(1, 0, 2, 4, 3).reshape(N_SLABS, CHUNK_COLS, S5_STATE)

    we = jnp.concatenate([e_half(w_re), e_half(w_im)], axis=2)

    g_re = c_re[None] * pw_re[1:, :, None, :] - c_im[None] * pw_im[1:, :, None, :]
    g_im = c_re[None] * pw_im[1:, :, None, :] + c_im[None] * pw_re[1:, :, None, :]

    def f_half(v):
        return _by_slab(v, 1).transpose(1, 4, 0, 2, 3).reshape(N_SLABS, S5_STATE, CHUNK_COLS)

    wf = jnp.concatenate([f_half(g_re), f_half(-g_im)], axis=1)

    def slab_vec(v):
        return v.reshape(N_SLABS, 1, SLAB_STATE)

    d_slab = d_skip.astype(F32).reshape(N_SLABS, 1, LANES)
    tile_e = _replicate_over_groups(2, S5_STATE)

    def bd_half(v):
        return _group_diag(_by_slab(v, 0).transpose(0, 1, 3, 2), 1, 3).reshape(N_SLABS, LANES, SLAB_STATE)

    def cd_half(v):
        return _group_diag(_by_slab(v, 0).transpose(0, 1, 3, 2), 1, 3).reshape(N_SLABS, SLAB_STATE, LANES)

    bd = jnp.concatenate([bd_half(bb_re), bd_half(bb_im)], axis=2)
    cd = jnp.concatenate([cd_half(c_re), cd_half(-c_im)], axis=1)
    bd_hi = bd.astype(BF16)
    bd_lo = (bd - bd_hi.astype(F32)).astype(BF16)
    return {
        "wr": wr.astype(BF16), "we": we.astype(BF16), "wf": wf.astype(BF16),
        "tile_r": _replicate_over_groups(2, S5_GROUP).astype(BF16),
        "tile_e": tile_e.astype(BF16), "tile_f": tile_e.T.astype(BF16),
        "a_re": slab_vec(pw_re[S5_CHUNK]), "a_im": slab_vec(pw_im[S5_CHUNK]),
        "d_chunk": jnp.tile(d_slab, (1, 1, S5_CHUNK)), "d_slab": d_slab,
        "bd_hi": bd_hi, "bd_lo": bd_lo, "cd": cd.astype(BF16),
        "l_re": slab_vec(lb_re), "l_im": slab_vec(lb_im),
    }


def _layer_params(w_in, w_uq, w_uk, w_uv, w_glu, w_out, norm_mix, norm_q, norm_kv, norm_attn_out, norm_ssm_out,
                  b_glu):
    c0, c1, c2 = Q_LORA, Q_LORA + KV_LORA, Q_LORA + KV_LORA + QK_ROPE
    w_uq_h = w_uq.reshape(Q_LORA, MLA_HEADS, QK_NOPE + QK_ROPE)
    return {
        "w_q": w_in[:, :c0].astype(BF16),
        "w_kv": w_in[:, c0:c1].astype(BF16),
        "w_kr_t": w_in[:, c1:c2].T.astype(BF16),
        "w_u": w_in[:, c2:].astype(BF16),
        "w_uq_nope": w_uq_h[:, :, :QK_NOPE].reshape(Q_LORA, MLA_HEADS * QK_NOPE).astype(BF16),
        "w_uq_rope_t": w_uq_h[:, :, QK_NOPE:].reshape(Q_LORA, MLA_HEADS * QK_ROPE).T.astype(BF16),
        "w_uk_h": w_uk.transpose(1, 0, 2).astype(BF16),
        "w_uv_h": w_uv.transpose(1, 2, 0).astype(BF16),
        "w_glu": w_glu.astype(BF16),
        "w_out_attn": w_out[:MLA_WIDTH].astype(BF16),
        "w_out_ssm": w_out[MLA_WIDTH:].astype(BF16),
        "norm_mix": norm_mix.reshape(1, D_MODEL),
        "norm_q": norm_q.reshape(1, Q_LORA),
        "norm_kv": norm_kv.reshape(1, KV_LORA),
        "norm_attn_out": norm_attn_out.reshape(1, MLA_WIDTH),
        "norm_attn_out_col": norm_attn_out.reshape(MLA_WIDTH, 1),
        "norm_ssm_out": norm_ssm_out.reshape(1, S5_WIDTH),
        "b_glu": b_glu.reshape(1, S5_WIDTH),
    }


def kernel(x_prompt, x_sample, c_prompt, c_sample, cache_ckv, cache_krope, state_s5_re, state_s5_im, page_table, w_ada, b_ada, norm_ffn1, ffn1_w1, ffn1_w3, ffn1_w2, norm_mix, w_in, norm_q, w_uq, norm_kv, w_uk, w_uv, s5_a_re, s5_a_im, s5_log_dt, s5_b_re, s5_b_im, s5_c_re, s5_c_im, s5_d, w_glu, b_glu, norm_attn_out, norm_ssm_out, w_out, norm_ffn2, ffn2_w1, ffn2_w3, ffn2_w2, norm_final):
    bp, seq, _ = x_prompt.shape
    bs = x_sample.shape[0]
    depth = w_ada.shape[0]
    assert depth == 1 and x_sample.shape[1] == 1
    n_pages = page_table.shape[1]
    past_len = n_pages * PAGE
    l = 0

    pad = (-(bs + bp)) % 8
    c_all = jnp.concatenate([c_sample, c_prompt, jnp.zeros((pad, D_MODEL), F32)], axis=0)
    mod = _ada(c_all, w_ada[l], b_ada[l])
    mod_s = mod[:bs].reshape(1, bs, ADA_CHUNKS * D_MODEL)
    mod_p = mod[bs:bs + bp].reshape(bp, 1, ADA_CHUNKS * D_MODEL)

    p = _layer_params(w_in[l], w_uq[l], w_uk[l], w_uv[l], w_glu[l], w_out[l], norm_mix[l], norm_q[l], norm_kv[l],
                      norm_attn_out[l], norm_ssm_out[l], b_glu[l])
    ops = _s5_operators(s5_a_re[l], s5_a_im[l], s5_log_dt[l], s5_b_re[l], s5_b_im[l], s5_c_re[l], s5_c_im[l],
                        s5_d[l])
    cos_p, sin_p = _rope_tables(jnp.arange(seq))
    cos_s, sin_s = _rope_tables(jnp.full((bs,), past_len))

    xs = x_sample.reshape(bs, D_MODEL)
    xs, *f1 = _ffn_cast(xs, mod_s, 0, norm_ffn1[l], ffn1_w1[l], ffn1_w3[l], ffn1_w2[l])
    q_lat_ts, q_rope_ts, ckv_s, _, _, krope_t_s, _, u_s = _mixin(xs, mod_s, 1, bs, p, cos_s, sin_s, tm=bs, tq=bs)
    krope_s = krope_t_s[0].T
    q_lat_s = q_lat_ts.reshape(KV_LORA, MLA_HEADS, bs).transpose(2, 1, 0)
    q_rope_s = q_rope_ts.reshape(QK_ROPE, MLA_HEADS, bs).transpose(2, 1, 0)
    o_lat = _sample_attn(page_table, q_lat_s, q_rope_s,
                         ckv_s.reshape(bs, 1, KV_LORA), krope_s.reshape(bs, 1, QK_ROPE), cache_ckv[l:l + 1],
                         cache_krope[l:l + 1].transpose(0, 1, 3, 2))
    attn_s = _sample_attn_out(o_lat.transpose(1, 0, 2), p["w_uv_h"], p["norm_attn_out"])
    y_s, hre_s, him_s = _s5_sample(u_s.reshape(bs, S5_WIDTH), state_s5_re[l], state_s5_im[l], ops)
    xs = _mixout(xs, attn_s, y_s, mod_s, 1, p, tm=bs)
    y_sample, *f2 = _ffn_cast(xs, mod_s, 6, norm_ffn2[l], ffn2_w1[l], ffn2_w3[l], ffn2_w2[l], norm_final)
    y_sample = y_sample.reshape(bs, 1, D_MODEL)

    tm_p = TILES.ffn_rows
    xp = x_prompt.reshape(bp * seq, D_MODEL)
    xp = _ffn(xp, mod_p, 0, seq // tm_p, norm_ffn1[l], *f1, tm=tm_p)
    tm_mix = TILES.mix_rows
    tq_p = TILES.attn_q
    q_lat_t, q_rope_t, ckv_p, kc_p, kct_p, krope_t_p, kr_p, u_p = _mixin(xp, mod_p, seq // tm_mix, seq, p, cos_p,
                                                                         sin_p, tm=tm_mix, tq=tq_p)
    attn_p = _prompt_attn(q_lat_t, q_rope_t, kc_p, kr_p, kct_p, p["w_uv_h"], p["norm_attn_out_col"], tq=tq_p,
                          tk=TILES.attn_k)
    y_p, hre_p, him_p = _s5_prompt(u_p, ops)
    xp = _mixout(xp, attn_p.reshape(bp * seq, MLA_WIDTH), y_p.reshape(bp * seq, S5_WIDTH), mod_p, seq // tm_mix, p,
                 tm=tm_mix)
    y_prompt = _ffn(xp, mod_p, 6, seq // tm_p, norm_ffn2[l], *f2, norm_final, tm=tm_p).reshape(bp, seq, D_MODEL)

    return (y_prompt, y_sample,
            ckv_p[None], krope_t_p.transpose(0, 2, 1)[None],
            ckv_s.reshape(1, bs, 1, KV_LORA), krope_s.reshape(1, bs, 1, QK_ROPE),
            hre_p[None], him_p[None], hre_s[None], him_s[None])
```

```python
import functools
from typing import NamedTuple

import jax
import jax.numpy as jnp
from jax import lax
from jax.experimental import pallas as pl
from jax.experimental.pallas import tpu as pltpu

F32 = jnp.float32
BF16 = jnp.bfloat16

D_MODEL = 2048
D_FF = 5632
MLA_HEADS = 8
QK_NOPE = 128
QK_ROPE = 64
V_HEAD = 128
Q_LORA = 512
KV_LORA = 256
MLA_WIDTH = 1024
S5_WIDTH = 1024
S5_GROUP = 16
S5_GROUPS = 64
S5_STATE = 64
ADA_CHUNKS = 9
PAGE = 128
ROPE_THETA = 10000.0
SOFTMAX_SCALE = (QK_NOPE + QK_ROPE) ** -0.5
EPS = 1e-6

LANES = 128
S5_CHUNK = 16
GROUPS_PER_SLAB = LANES // S5_GROUP
N_SLABS = S5_WIDTH // LANES
SLAB_STATE = GROUPS_PER_SLAB * S5_STATE
CHUNK_COLS = S5_CHUNK * LANES
MXU_TILE = 2 * LANES
VMEM_LIMIT = 56 * 1024 * 1024


class _Tiles(NamedTuple):
    ada_cols: int = 1024
    ffn_rows: int = 512
    ffn_cols: int = 512
    mix_rows: int = 512
    attn_q: int = 512
    attn_k: int = 512
    pages: int = 32
    streams: int = 2
    slots: int = 3


TILES = _Tiles()


def _cparams(sem):
    return pltpu.CompilerParams(dimension_semantics=sem, vmem_limit_bytes=VMEM_LIMIT)


def _rms(x, g):
    return x * lax.rsqrt(jnp.mean(x * x, axis=-1, keepdims=True) + EPS) * g


def _rms_rows(x, g):
    return x * lax.rsqrt(jnp.mean(x * x, axis=0, keepdims=True) + EPS) * g


def _dot(a, b):
    return jnp.dot(a, b, preferred_element_type=F32)


def _dot_nt(a, b):
    return lax.dot_general(a, b, (((1,), (1,)), ((), ())), preferred_element_type=F32)


def _ada_kernel(c_ref, w_ref, b_ref, o_ref):
    c = c_ref[...]
    a = (c * jax.nn.sigmoid(c)).astype(BF16)
    o_ref[...] = _dot(a, w_ref[...].astype(BF16)) + b_ref[...]


def _ada(c, w_ada, b_ada):
    rows = c.shape[0]
    n = w_ada.shape[1]
    tn = TILES.ada_cols
    return pl.pallas_call(
        _ada_kernel,
        grid=(n // tn,),
        in_specs=[
            pl.BlockSpec((rows, D_MODEL), lambda j: (0, 0)),
            pl.BlockSpec((D_MODEL, tn), lambda j: (0, j)),
            pl.BlockSpec((1, tn), lambda j: (0, j)),
        ],
        out_specs=pl.BlockSpec((rows, tn), lambda j: (0, j)),
        out_shape=jax.ShapeDtypeStruct((rows, n), F32),
        compiler_params=_cparams(("arbitrary",)),
        name="ada_modulation",
    )(c, w_ada, b_ada.reshape(1, n))


def _ffn_kernel(x_ref, sh_ref, sc_ref, gt_ref, g_ref, w1_ref, w3_ref, w2_ref, w2_last_ref, *rest, final_norm):
    if final_norm:
        gf_ref, o_ref, h_ref, act_ref = rest
    else:
        o_ref, h_ref, act_ref = rest
    j = pl.program_id(1)
    last = pl.num_programs(1) - 1

    def gate():
        h = h_ref[...]
        a = _dot(h, w1_ref[...])
        b = _dot(h, w3_ref[...])
        act_ref[...] = (a * jax.nn.sigmoid(a) * b).astype(BF16)

    def down():
        return _dot(act_ref[...], w2_ref[...])

    @pl.when(j == 0)
    def _():
        h = _rms(x_ref[...], g_ref[...]) * (1.0 + sc_ref[0]) + sh_ref[0]
        h_ref[...] = h.astype(BF16)
        gate()

    @pl.when(j == 1)
    def _():
        o_ref[...] = down()
        gate()

    @pl.when((j > 1) & (j < last))
    def _():
        o_ref[...] += down()
        gate()

    @pl.when(j == last)
    def _():
        prev = down()
        gate()
        y = x_ref[...] + 0.5 * gt_ref[0] * (o_ref[...] + prev + _dot(act_ref[...], w2_last_ref[...]))
        if final_norm:
            y = _rms(y, gf_ref[...])
        o_ref[...] = y


def _ffn_cast_kernel(x_ref, sh_ref, sc_ref, gt_ref, g_ref, w1_ref, w3_ref, w2_ref, *rest, final_norm):
    if final_norm:
        gf_ref, o_ref, w1b_ref, w3b_ref, w2b_ref, h_ref = rest
    else:
        o_ref, w1b_ref, w3b_ref, w2b_ref, h_ref = rest
    j = pl.program_id(0)

    @pl.when(j == 0)
    def _():
        h = _rms(x_ref[...], g_ref[...]) * (1.0 + sc_ref[0]) + sh_ref[0]
        h_ref[...] = h.astype(BF16)
        o_ref[...] = jnp.zeros_like(o_ref)

    w1b = w1_ref[...].astype(BF16)
    w3b = w3_ref[...].astype(BF16)
    w2b = w2_ref[...].astype(BF16)
    w1b_ref[...] = w1b
    w3b_ref[...] = w3b
    w2b_ref[...] = w2b
    h = h_ref[...]
    a = _dot(h, w1b)
    act = (a * jax.nn.sigmoid(a) * _dot(h, w3b)).astype(BF16)
    o_ref[...] += _dot(act, w2b)

    @pl.when(j == pl.num_programs(0) - 1)
    def _():
        y = x_ref[...] + 0.5 * gt_ref[0] * o_ref[...]
        if final_norm:
            y = _rms(y, gf_ref[...])
        o_ref[...] = y


def _ffn_cast(x, mod, chunk0, g, w1, w3, w2, gf=None, *, tf=TILES.ffn_cols):
    rows = x.shape[0]
    const = lambda shape: pl.BlockSpec(shape, lambda j: (0,) * len(shape))
    mod_spec = lambda k: pl.BlockSpec((1, rows, D_MODEL), lambda j: (0, 0, k))
    up = pl.BlockSpec((D_MODEL, tf), lambda j: (0, j))
    down = pl.BlockSpec((tf, D_MODEL), lambda j: (j, 0))
    in_specs = [const((rows, D_MODEL)), mod_spec(chunk0), mod_spec(chunk0 + 1), mod_spec(chunk0 + 2),
                const((1, D_MODEL)), up, up, down]
    args = [x, mod, mod, mod, g.reshape(1, D_MODEL), w1, w3, w2]
    if gf is not None:
        in_specs.append(const((1, D_MODEL)))
        args.append(gf.reshape(1, D_MODEL))
    return pl.pallas_call(
        functools.partial(_ffn_cast_kernel, final_norm=gf is not None),
        grid=(D_FF // tf,),
        in_specs=in_specs,
        out_specs=[const((rows, D_MODEL)), up, up, down],
        out_shape=[jax.ShapeDtypeStruct((rows, D_MODEL), F32), jax.ShapeDtypeStruct(w1.shape, BF16),
                   jax.ShapeDtypeStruct(w3.shape, BF16), jax.ShapeDtypeStruct(w2.shape, BF16)],
        scratch_shapes=[pltpu.VMEM((rows, D_MODEL), BF16)],
        compiler_params=_cparams(("arbitrary",)),
        name="sample_ffn_cast",
    )(*args)


def _ffn(x, mod, chunk0, tiles_per_b, g, w1, w3, w2, gf=None, *, tm, tf=TILES.ffn_cols):
    n = x.shape[0]
    r = mod.shape[1]
    n_ff = D_FF // tf
    assert n_ff >= 3

    def mod_spec(k):
        return pl.BlockSpec((1, r, D_MODEL), lambda i, j: (i // tiles_per_b, 0, k))

    in_specs = [
        pl.BlockSpec((tm, D_MODEL), lambda i, j: (i, 0)),
        mod_spec(chunk0), mod_spec(chunk0 + 1), mod_spec(chunk0 + 2),
        pl.BlockSpec((1, D_MODEL), lambda i, j: (0, 0)),
        pl.BlockSpec((D_MODEL, tf), lambda i, j: (0, j)),
        pl.BlockSpec((D_MODEL, tf), lambda i, j: (0, j)),
        pl.BlockSpec((tf, D_MODEL), lambda i, j: (jnp.maximum(j - 1, 0), 0)),
        pl.BlockSpec((tf, D_MODEL), lambda i, j: (n_ff - 1, 0)),
    ]
    args = [x, mod, mod, mod, g.reshape(1, D_MODEL), w1, w3, w2, w2]
    if gf is not None:
        in_specs.append(pl.BlockSpec((1, D_MODEL), lambda i, j: (0, 0)))
        args.append(gf.reshape(1, D_MODEL))
    return pl.pallas_call(
        functools.partial(_ffn_kernel, final_norm=gf is not None),
        grid=(n // tm, n_ff),
        in_specs=in_specs,
        out_specs=pl.BlockSpec((tm, D_MODEL), lambda i, j: (i, 0)),
        out_shape=jax.ShapeDtypeStruct((n, D_MODEL), F32),
        scratch_shapes=[pltpu.VMEM((tm, D_MODEL), BF16), pltpu.VMEM((tm, tf), BF16)],
        compiler_params=_cparams(("parallel", "arbitrary")),
        name="macaron_ffn",
    )(*args)


def _rope_rows(x, cos, sin):
    half = QK_ROPE // 2
    x1 = x[:half]
    x2 = x[half:]
    return x1 * cos - x2 * sin, x1 * sin + x2 * cos


def _mixin_kernel(x_ref, sh_ref, sc_ref, g_ref, wq_ref, wkv_ref, wkr_ref, wu_ref, gq_ref, wqn_ref, wqr_ref,
                  wuk_ref, gkv_ref, cos_ref, sin_ref,
                  qlat_ref, qrope_ref, ckv_ref, kc_ref, kct_ref, krope_ref, kr_ref, u_ref, *, tq):
    half = QK_ROPE // 2
    n_q = x_ref.shape[0] // tq
    h = (_rms(x_ref[...], g_ref[...]) * (1.0 + sc_ref[0]) + sh_ref[0]).astype(BF16)
    cos = cos_ref[...]
    sin = sin_ref[...]

    u_ref[0] = _dot(h, wu_ref[...])

    ckv = _rms(_dot(h, wkv_ref[...]), gkv_ref[...])
    ckv_ref[0] = ckv
    kc_ref[0] = ckv.astype(BF16)
    kct_ref[0] = ckv.T.astype(BF16)

    k1, k2 = _rope_rows(_dot_nt(wkr_ref[...], h), cos, sin)
    kr_t = jnp.concatenate([k1, k2], axis=0)
    krope_ref[0] = kr_t
    kr_ref[0] = kr_t.T.astype(BF16)

    qn = _rms(_dot(h, wq_ref[...]), gq_ref[...]).astype(BF16)
    q_nope = (_dot(qn, wqn_ref[...]) * SOFTMAX_SCALE).astype(BF16)
    q_rope_t = _dot_nt(wqr_ref[...], qn) * SOFTMAX_SCALE
    for hd in range(MLA_HEADS):
        ql_t = _dot_nt(wuk_ref[hd], q_nope[:, hd * QK_NOPE:(hd + 1) * QK_NOPE]).astype(BF16)
        r1, r2 = _rope_rows(q_rope_t[hd * QK_ROPE:(hd + 1) * QK_ROPE], cos, sin)
        r1 = r1.astype(BF16)
        r2 = r2.astype(BF16)
        for qq in range(n_q):
            src = slice(qq * tq, (qq + 1) * tq)
            dst = slice(hd * tq, (hd + 1) * tq)
            qlat_ref[0, qq, :, dst] = ql_t[:, src]
            qrope_ref[0, qq, :half, dst] = r1[:, src]
            qrope_ref[0, qq, half:, dst] = r2[:, src]


def _mixin(x, mod, tiles_per_b, t_len, p, cos_t, sin_t, *, tm, tq):
    n = x.shape[0]
    nb = n // t_len
    r = mod.shape[1]
    half = QK_ROPE // 2
    full = lambda shape: pl.BlockSpec(shape, lambda i: (0,) * len(shape))
    rows = lambda width: pl.BlockSpec((1, tm, width), lambda i: (i // tiles_per_b, i % tiles_per_b, 0))
    cols = lambda height: pl.BlockSpec((1, height, tm), lambda i: (i // tiles_per_b, 0, i % tiles_per_b))
    head_cols = lambda height: pl.BlockSpec((1, tm // tq, height, MLA_HEADS * tq),
                                            lambda i: (i // tiles_per_b, i % tiles_per_b, 0, 0))
    in_specs = [
        pl.BlockSpec((tm, D_MODEL), lambda i: (i, 0)),
        pl.BlockSpec((1, r, D_MODEL), lambda i: (i // tiles_per_b, 0, 3)),
        pl.BlockSpec((1, r, D_MODEL), lambda i: (i // tiles_per_b, 0, 4)),
        full((1, D_MODEL)),
        full((D_MODEL, Q_LORA)), full((D_MODEL, KV_LORA)), full((QK_ROPE, D_MODEL)), full((D_MODEL, S5_WIDTH)),
        full((1, Q_LORA)), full((Q_LORA, MLA_HEADS * QK_NOPE)), full((MLA_HEADS * QK_ROPE, Q_LORA)),
        full((MLA_HEADS, KV_LORA, QK_NOPE)), full((1, KV_LORA)),
        pl.BlockSpec((half, tm), lambda i: (0, i % tiles_per_b)),
        pl.BlockSpec((half, tm), lambda i: (0, i % tiles_per_b)),
    ]
    out_specs = [head_cols(KV_LORA), head_cols(QK_ROPE), rows(KV_LORA), rows(KV_LORA), cols(KV_LORA),
                 cols(QK_ROPE), rows(QK_ROPE), rows(S5_WIDTH)]
    out_shape = [
        jax.ShapeDtypeStruct((nb, t_len // tq, KV_LORA, MLA_HEADS * tq), BF16),
        jax.ShapeDtypeStruct((nb, t_len // tq, QK_ROPE, MLA_HEADS * tq), BF16),
        jax.ShapeDtypeStruct((nb, t_len, KV_LORA), F32),
        jax.ShapeDtypeStruct((nb, t_len, KV_LORA), BF16),
        jax.ShapeDtypeStruct((nb, KV_LORA, t_len), BF16),
        jax.ShapeDtypeStruct((nb, QK_ROPE, t_len), F32),
        jax.ShapeDtypeStruct((nb, t_len, QK_ROPE), BF16),
        jax.ShapeDtypeStruct((nb, t_len, S5_WIDTH), F32),
    ]
    return pl.pallas_call(
        functools.partial(_mixin_kernel, tq=tq),
        grid=(n // tm,),
        in_specs=in_specs,
        out_specs=out_specs,
        out_shape=out_shape,
        compiler_params=_cparams(("parallel",)),
        name="mixer_in",
    )(x, mod, mod, p["norm_mix"], p["w_q"], p["w_kv"], p["w_kr_t"], p["w_u"], p["norm_q"], p["w_uq_nope"],
      p["w_uq_rope_t"], p["w_uk_h"], p["norm_kv"], cos_t, sin_t)


def _prompt_attn_kernel(ql_ref, qr_ref, kc_ref, kr_ref, kct_ref, wuv_ref, g_ref, o_ref, *scratch, tq, tk):
    qi = pl.program_id(1)
    n_half = len(scratch) // 3
    m_ref, l_ref, acc_ref = scratch[:n_half], scratch[n_half:2 * n_half], scratch[2 * n_half:]
    width = m_ref[0].shape[1]
    heads_per_half = width // tq
    for hf in range(n_half):
        m_ref[hf][...] = jnp.full_like(m_ref[hf], -jnp.inf)
        l_ref[hf][...] = jnp.zeros_like(l_ref[hf])
        acc_ref[hf][...] = jnp.zeros_like(acc_ref[hf])

    def step(ki, diag):
        start = pl.multiple_of(ki * tk, tk)
        kc = kc_ref[0, pl.ds(start, tk), :]
        kr = kr_ref[0, pl.ds(start, tk), :]
        kct = kct_ref[0, :, pl.ds(start, tk)]
        masked = diag is not None
        if masked:
            k_local = lax.broadcasted_iota(jnp.int32, (tk, width), 0) + diag * tk
            t_local = lax.broadcasted_iota(jnp.int32, (tk, width), 1) & (tq - 1)
            keep = k_local <= t_local
        for hf in range(n_half):
            lanes = slice(hf * width, (hf + 1) * width)
            s = _dot(kc, ql_ref[0, 0, :, lanes]) + _dot(kr, qr_ref[0, 0, :, lanes])
            if masked:
                s = jnp.where(keep, s, -jnp.inf)
            m_old = m_ref[hf][...]
            m_new = jnp.maximum(m_old, jnp.max(s, axis=0, keepdims=True))
            alpha = jnp.exp(m_old - m_new)
            pexp = jnp.exp(s - m_new)
            l_ref[hf][...] = alpha * l_ref[hf][...] + jnp.sum(pexp, axis=0, keepdims=True)
            acc_ref[hf][...] = alpha * acc_ref[hf][...] + _dot(kct, pexp.astype(BF16))
            m_ref[hf][...] = m_new

    def body(ki, carry):
        step(ki, None)
        return carry

    blocks_per_tile = tq // tk
    lax.fori_loop(0, qi * blocks_per_tile, body, 0)
    for d in range(blocks_per_tile):
        step(qi * blocks_per_tile + d, d)
    parts = []
    for hd in range(MLA_HEADS):
        hf = hd // heads_per_half
        lanes = slice((hd % heads_per_half) * tq, (hd % heads_per_half + 1) * tq)
        o_t = acc_ref[hf][:, lanes] / l_ref[hf][:, lanes]
        parts.append(_dot(wuv_ref[hd], o_t.astype(BF16)))
    attn_t = _rms_rows(jnp.concatenate(parts, axis=0), g_ref[...])
    o_ref[0] = attn_t.T.astype(BF16)


def _prompt_attn(q_lat_t, q_rope_t, kc, kr, kc_t, w_uv_h, g_col, *, tq, tk, n_half=1):
    nb, n_q, _, _ = q_lat_t.shape
    t_len = n_q * tq
    width = MLA_HEADS * tq // n_half
    assert tq % tk == 0
    return pl.pallas_call(
        functools.partial(_prompt_attn_kernel, tq=tq, tk=tk),
        grid=(nb, n_q),
        in_specs=[
            pl.BlockSpec((1, 1, KV_LORA, MLA_HEADS * tq), lambda b, i: (b, i, 0, 0)),
            pl.BlockSpec((1, 1, QK_ROPE, MLA_HEADS * tq), lambda b, i: (b, i, 0, 0)),
            pl.BlockSpec((1, t_len, KV_LORA), lambda b, i: (b, 0, 0)),
            pl.BlockSpec((1, t_len, QK_ROPE), lambda b, i: (b, 0, 0)),
            pl.BlockSpec((1, KV_LORA, t_len), lambda b, i: (b, 0, 0)),
            pl.BlockSpec((MLA_HEADS, V_HEAD, KV_LORA), lambda b, i: (0, 0, 0)),
            pl.BlockSpec((MLA_WIDTH, 1), lambda b, i: (0, 0)),
        ],
        out_specs=pl.BlockSpec((1, tq, MLA_WIDTH), lambda b, i: (b, i, 0)),
        out_shape=jax.ShapeDtypeStruct((nb, t_len, MLA_WIDTH), BF16),
        scratch_shapes=([pltpu.VMEM((1, width), F32)] * (2 * n_half) + [pltpu.VMEM((KV_LORA, width), F32)] * n_half),
        compiler_params=_cparams(("parallel", "parallel")),
        name="prompt_attention",
    )(q_lat_t, q_rope_t, kc, kr, kc_t, w_uv_h, g_col)


def _sample_attn_kernel(pt_ref, ql_ref, qr_ref, kcn_ref, krn_ref, ckv_hbm, kr_hbm, o_ref, kbuf, rbuf, sem_k, sem_r,
                        *, pages, streams):
    nb, n_pages = pt_ref.shape
    n_chunks = n_pages // pages
    n_groups = nb // streams
    total = n_groups * n_chunks
    n_slots = kbuf.shape[0]
    ahead = n_slots - 1

    def chunk_copies(g, slot):
        grp = g // n_chunks
        c = g % n_chunks
        copies = []
        for st in range(streams):
            b = grp * streams + st
            for i in range(pages):
                page = pt_ref[b, c * pages + i]
                copies.append(pltpu.make_async_copy(
                    ckv_hbm.at[0, page], kbuf.at[slot, st, pl.ds(i * PAGE, PAGE), :], sem_k.at[slot, st]))
                copies.append(pltpu.make_async_copy(
                    kr_hbm.at[0, page], rbuf.at[slot, st, i], sem_r.at[slot, st]))
        return copies

    def start_all(copies):
        for n, cp in enumerate(copies):
            cp.start(priority=n % 2)

    grp = pl.program_id(0)

    @pl.when(grp == 0)
    def _():
        for g0 in range(ahead):
            start_all(chunk_copies(g0, g0))

    qs = [(ql_ref[st], qr_ref[st]) for st in range(streams)]

    def chunk_body(c, carry):
        g = grp * n_chunks + c
        slot = g % n_slots

        @pl.when(g + ahead < total)
        def _():
            start_all(chunk_copies(g + ahead, (g + ahead) % n_slots))

        for cp in chunk_copies(g, slot):
            cp.wait()
        out = []
        for st in range(streams):
            m_old, l_old, acc_old = carry[st]
            ql, qr = qs[st]
            kc = kbuf[slot, st].astype(BF16)
            kr_t = jnp.concatenate([rbuf[slot, st, i].astype(BF16) for i in range(pages)], axis=1)
            s = _dot_nt(ql, kc) + _dot(qr, kr_t)
            m_new = jnp.maximum(m_old, jnp.max(s, axis=1, keepdims=True))
            alpha = jnp.exp(m_old - m_new)
            pexp = jnp.exp(s - m_new)
            l_new = alpha * l_old + jnp.sum(pexp, axis=1, keepdims=True)
            acc_new = alpha * acc_old + _dot(pexp.astype(BF16), kc)
            out.append((m_new, l_new, acc_new))
        return tuple(out)

    init = tuple((jnp.full((MLA_HEADS, 1), -jnp.inf, F32), jnp.zeros((MLA_HEADS, 1), F32),
                  jnp.zeros((MLA_HEADS, KV_LORA), F32)) for _ in range(streams))
    final = lax.fori_loop(0, n_chunks, chunk_body, init)
    for st in range(streams):
        m_old, l_old, acc_old = final[st]
        ql, qr = qs[st]
        kcn = kcn_ref[st].astype(BF16).astype(F32)
        krn = krn_ref[st].astype(BF16).astype(F32)
        s_n = (jnp.sum(ql.astype(F32) * kcn, axis=1, keepdims=True)
               + jnp.sum(qr.astype(F32) * krn, axis=1, keepdims=True))
        m_f = jnp.maximum(m_old, s_n)
        a_f = jnp.exp(m_old - m_f)
        p_n = jnp.exp(s_n - m_f)
        l_f = a_f * l_old + p_n
        acc_f = a_f * acc_old + p_n.astype(BF16).astype(F32) * kcn
        o_ref[st] = acc_f / l_f


def _sample_attn(page_table, q_lat, q_rope, ckv_new, krope_new, cache_ckv, cache_krope_t, *, pages=TILES.pages,
                 streams=TILES.streams, slots=TILES.slots):
    nb, n_pages = page_table.shape
    assert n_pages % pages == 0 and nb % streams == 0 and (nb // streams) * (n_pages // pages) >= slots
    per_group = lambda rows, width: pl.BlockSpec((streams, rows, width), lambda g, pt: (g, 0, 0))
    grid_spec = pltpu.PrefetchScalarGridSpec(
        num_scalar_prefetch=1,
        grid=(nb // streams,),
        in_specs=[
            per_group(MLA_HEADS, KV_LORA), per_group(MLA_HEADS, QK_ROPE),
            per_group(1, KV_LORA), per_group(1, QK_ROPE),
            pl.BlockSpec(memory_space=pl.ANY), pl.BlockSpec(memory_space=pl.ANY),
        ],
        out_specs=per_group(MLA_HEADS, KV_LORA),
        scratch_shapes=[
            pltpu.VMEM((slots, streams, pages * PAGE, KV_LORA), F32),
            pltpu.VMEM((slots, streams, pages, QK_ROPE, PAGE), F32),
            pltpu.SemaphoreType.DMA((slots, streams)),
            pltpu.SemaphoreType.DMA((slots, streams)),
        ],
    )
    return pl.pallas_call(
        functools.partial(_sample_attn_kernel, pages=pages, streams=streams),
        grid_spec=grid_spec,
        out_shape=jax.ShapeDtypeStruct((nb, MLA_HEADS, KV_LORA), F32),
        compiler_params=_cparams(("arbitrary",)),
        name="sample_paged_attention",
    )(page_table, q_lat, q_rope, ckv_new, krope_new, cache_ckv, cache_krope_t)


def _sample_attn_out_kernel(o_ref, wuv_ref, g_ref, out_ref):
    parts = [_dot_nt(o_ref[hd].astype(BF16), wuv_ref[hd]) for hd in range(MLA_HEADS)]
    out_ref[...] = _rms(jnp.concatenate(parts, axis=1), g_ref[...]).astype(BF16)


def _sample_attn_out(o_lat_h, w_uv_h, g):
    nb = o_lat_h.shape[1]
    return pl.pallas_call(
        _sample_attn_out_kernel,
        out_shape=jax.ShapeDtypeStruct((nb, MLA_WIDTH), BF16),
        name="sample_attention_out",
    )(o_lat_h, w_uv_h, g)


def _same_group(shape, row_shift, col_shift):
    rows = lax.broadcasted_iota(jnp.int32, shape, 0)
    cols = lax.broadcasted_iota(jnp.int32, shape, 1)
    mask = GROUPS_PER_SLAB - 1
    return ((rows >> row_shift) & mask) == ((cols >> col_shift) & mask)


def _s5_prompt_kernel(u_ref, wr_ref, we_ref, wf_ref, tr_ref, te_ref, tf_ref, are_ref, aim_ref, d_ref,
                      y_ref, hre_ref, him_ref, r_ref, e_ref, f_ref, uc_ref, s_ref, hp_ref, yc_ref):
    n_chunks = uc_ref.shape[0]
    n_blk = CHUNK_COLS // MXU_TILE
    ch_bits = S5_GROUP.bit_length() - 1
    st_bits = S5_STATE.bit_length() - 1

    @pl.when(pl.program_id(1) == 0)
    def _():
        zero = jnp.zeros((), F32)
        r_ref[...] = jnp.where(_same_group(r_ref.shape, ch_bits, ch_bits),
                               _dot(wr_ref[0], tr_ref[...]), zero).astype(BF16)
        e_ref[...] = jnp.where(_same_group(e_ref.shape, ch_bits, st_bits),
                               _dot(we_ref[0], te_ref[...]), zero).astype(BF16)
        f_ref[...] = jnp.where(_same_group(f_ref.shape, st_bits, ch_bits),
                               _dot(tf_ref[...], wf_ref[0]), zero).astype(BF16)

    for s in range(S5_CHUNK):
        uc_ref[:, s * LANES:(s + 1) * LANES] = u_ref[0, pl.ds(s, n_chunks, stride=S5_CHUNK), :]
    uc = uc_ref[...]
    ub = uc.astype(BF16)

    s_ref[...] = _dot(ub, e_ref[...])

    a_re = are_ref[0]
    a_im = aim_ref[0]

    def scan(k, carry):
        h_re, h_im = carry
        hp_ref[pl.ds(k, 1), :] = jnp.concatenate([h_re, h_im], axis=1)
        row = s_ref[pl.ds(k, 1), :]
        n_re = a_re * h_re - a_im * h_im + row[:, :SLAB_STATE]
        n_im = a_re * h_im + a_im * h_re + row[:, SLAB_STATE:]
        return n_re, n_im

    zero_state = jnp.zeros((1, SLAB_STATE), F32)
    h_re, h_im = lax.fori_loop(0, n_chunks, scan, (zero_state, zero_state), unroll=8)
    hre_ref[0, 0] = h_re
    him_ref[0, 0] = h_im

    yc_ref[...] = _dot(hp_ref[...].astype(BF16), f_ref[...]) + uc * d_ref[0]
    for tb in range(n_blk):
        yc_ref[:, tb * MXU_TILE:(tb + 1) * MXU_TILE] += _dot(ub[:, :(tb + 1) * MXU_TILE],
                                                             r_ref[(n_blk - 1 - tb) * MXU_TILE:, :])
    y = jax.nn.gelu(yc_ref[...])
    for t in range(S5_CHUNK):
        y_ref[0, pl.ds(t, n_chunks, stride=S5_CHUNK), :] = y[:, t * LANES:(t + 1) * LANES]


def _s5_prompt(u, ops):
    nb, t_len, _ = u.shape
    n_chunks = t_len // S5_CHUNK
    slab = lambda shape: pl.BlockSpec((1,) + shape, lambda j, b: (j,) + (0,) * len(shape))
    full = lambda shape: pl.BlockSpec(shape, lambda j, b: (0,) * len(shape))
    y, hre, him = pl.pallas_call(
        _s5_prompt_kernel,
        grid=(N_SLABS, nb),
        in_specs=[
            pl.BlockSpec((1, t_len, LANES), lambda j, b: (b, 0, j)),
            slab((CHUNK_COLS, 2 * S5_GROUP)), slab((CHUNK_COLS, 2 * S5_STATE)), slab((2 * S5_STATE, CHUNK_COLS)),
            full((2 * S5_GROUP, MXU_TILE)), full((2 * S5_STATE, 2 * SLAB_STATE)), full((2 * SLAB_STATE, 2 * S5_STATE)),
            slab((1, SLAB_STATE)), slab((1, SLAB_STATE)), slab((1, CHUNK_COLS)),
        ],
        out_specs=[
            pl.BlockSpec((1, t_len, LANES), lambda j, b: (b, 0, j)),
            pl.BlockSpec((1, 1, 1, SLAB_STATE), lambda j, b: (j, b, 0, 0)),
            pl.BlockSpec((1, 1, 1, SLAB_STATE), lambda j, b: (j, b, 0, 0)),
        ],
        out_shape=[
            jax.ShapeDtypeStruct((nb, t_len, S5_WIDTH), F32),
            jax.ShapeDtypeStruct((N_SLABS, nb, 1, SLAB_STATE), F32),
            jax.ShapeDtypeStruct((N_SLABS, nb, 1, SLAB_STATE), F32),
        ],
        scratch_shapes=[
            pltpu.VMEM((CHUNK_COLS, MXU_TILE), BF16),
            pltpu.VMEM((CHUNK_COLS, 2 * SLAB_STATE), BF16),
            pltpu.VMEM((2 * SLAB_STATE, CHUNK_COLS), BF16),
            pltpu.VMEM((n_chunks, CHUNK_COLS), F32),
            pltpu.VMEM((n_chunks, 2 * SLAB_STATE), F32),
            pltpu.VMEM((n_chunks, 2 * SLAB_STATE), F32),
            pltpu.VMEM((n_chunks, CHUNK_COLS), F32),
        ],
        compiler_params=_cparams(("arbitrary", "arbitrary")),
        name="s5_prompt",
    )(u, ops["wr"], ops["we"], ops["wf"], ops["tile_r"], ops["tile_e"], ops["tile_f"], ops["a_re"], ops["a_im"],
      ops["d_chunk"])

    def to_state(h):
        return h.reshape(N_SLABS, nb, GROUPS_PER_SLAB, S5_STATE).transpose(1, 0, 2, 3).reshape(nb, S5_GROUPS, S5_STATE)

    return y, to_state(hre), to_state(him)


def _split_bf16(x):
    hi = x.astype(BF16)
    return hi, (x - hi.astype(F32)).astype(BF16)


def _s5_sample_kernel(u_ref, h0re_ref, h0im_ref, bd_hi_ref, bd_lo_ref, cd_ref, lre_ref, lim_ref, d_ref,
                      y_ref, hre_ref, him_ref):
    u = u_ref[...]
    u_hi, u_lo = _split_bf16(u)
    bu = _dot(u_hi, bd_hi_ref[0]) + (_dot(u_hi, bd_lo_ref[0]) + _dot(u_lo, bd_hi_ref[0]))
    l_re = lre_ref[0]
    l_im = lim_ref[0]
    h0_re = h0re_ref[...]
    h0_im = h0im_ref[...]
    h_re = l_re * h0_re - l_im * h0_im + bu[:, :SLAB_STATE]
    h_im = l_re * h0_im + l_im * h0_re + bu[:, SLAB_STATE:]
    hre_ref[...] = h_re
    him_ref[...] = h_im
    h = jnp.concatenate([h_re, h_im], axis=1).astype(BF16)
    y_ref[...] = jax.nn.gelu(_dot(h, cd_ref[0]) + u * d_ref[0])


def _s5_sample(u, h0_re, h0_im, ops):
    nb = u.shape[0]
    n_state = S5_GROUPS * S5_STATE
    slab = lambda shape: pl.BlockSpec((1,) + shape, lambda j: (j,) + (0,) * len(shape))
    col = lambda width: pl.BlockSpec((nb, width), lambda j: (0, j))
    y, hre, him = pl.pallas_call(
        _s5_sample_kernel,
        grid=(N_SLABS,),
        in_specs=[
            col(LANES), col(SLAB_STATE), col(SLAB_STATE),
            slab((LANES, 2 * SLAB_STATE)), slab((LANES, 2 * SLAB_STATE)), slab((2 * SLAB_STATE, LANES)),
            slab((1, SLAB_STATE)), slab((1, SLAB_STATE)), slab((1, LANES)),
        ],
        out_specs=[col(LANES), col(SLAB_STATE), col(SLAB_STATE)],
        out_shape=[
            jax.ShapeDtypeStruct((nb, S5_WIDTH), F32),
            jax.ShapeDtypeStruct((nb, n_state), F32),
            jax.ShapeDtypeStruct((nb, n_state), F32),
        ],
        compiler_params=_cparams(("parallel",)),
        name="s5_sample",
    )(u, h0_re.reshape(nb, n_state), h0_im.reshape(nb, n_state), ops["bd_hi"], ops["bd_lo"], ops["cd"],
      ops["l_re"], ops["l_im"], ops["d_slab"])
    return y, hre.reshape(nb, S5_GROUPS, S5_STATE), him.reshape(nb, S5_GROUPS, S5_STATE)


def _mixout_kernel(x_ref, attn_ref, y_ref, gt_ref, wglu_ref, bglu_ref, gs_ref, woa_ref, woy_ref, o_ref):
    y = y_ref[...]
    z = _dot(y.astype(BF16), wglu_ref[...]) + bglu_ref[...]
    yn = _rms(y * jax.nn.sigmoid(z), gs_ref[...]).astype(BF16)
    mix = _dot(attn_ref[...], woa_ref[...]) + _dot(yn, woy_ref[...])
    o_ref[...] = x_ref[...] + gt_ref[0] * mix


def _mixout(x, attn, y, mod, tiles_per_b, p, *, tm):
    n = x.shape[0]
    r = mod.shape[1]
    full = lambda shape: pl.BlockSpec(shape, lambda i: (0,) * len(shape))
    return pl.pallas_call(
        _mixout_kernel,
        grid=(n // tm,),
        in_specs=[
            pl.BlockSpec((tm, D_MODEL), lambda i: (i, 0)),
            pl.BlockSpec((tm, MLA_WIDTH), lambda i: (i, 0)),
            pl.BlockSpec((tm, S5_WIDTH), lambda i: (i, 0)),
            pl.BlockSpec((1, r, D_MODEL), lambda i: (i // tiles_per_b, 0, 5)),
            full((S5_WIDTH, S5_WIDTH)), full((1, S5_WIDTH)), full((1, S5_WIDTH)),
            full((MLA_WIDTH, D_MODEL)), full((S5_WIDTH, D_MODEL)),
        ],
        out_specs=pl.BlockSpec((tm, D_MODEL), lambda i: (i, 0)),
        out_shape=jax.ShapeDtypeStruct((n, D_MODEL), F32),
        compiler_params=_cparams(("parallel",)),
        name="mixer_out",
    )(x, attn, y, mod, p["w_glu"], p["b_glu"], p["norm_ssm_out"], p["w_out_attn"], p["w_out_ssm"])


def _rope_tables(pos):
    half = QK_ROPE // 2
    inv_freq = ROPE_THETA ** (-jnp.arange(half, dtype=F32) / half)
    ang = pos.astype(F32)[:, None] * inv_freq[None, :]
    return jnp.cos(ang).T, jnp.sin(ang).T


def _group_diag(x, g_axis, new_axis):
    x = jnp.expand_dims(x, new_axis)
    shape = [1] * x.ndim
    shape[g_axis if g_axis < new_axis else g_axis + 1] = GROUPS_PER_SLAB
    shape[new_axis] = GROUPS_PER_SLAB
    return x * jnp.eye(GROUPS_PER_SLAB, dtype=x.dtype).reshape(shape)


def _by_slab(x, g_axis):
    return x.reshape(x.shape[:g_axis] + (N_SLABS, GROUPS_PER_SLAB) + x.shape[g_axis + 1:])


def _replicate_over_groups(n_outer, n_inner):
    eo = jnp.eye(n_outer, dtype=F32)[:, None, :, None, None]
    ei = jnp.eye(n_inner, dtype=F32)[None, :, None, None, :]
    ones = jnp.ones((1, 1, 1, GROUPS_PER_SLAB, 1), F32)
    return (eo * ei * ones).reshape(n_outer * n_inner, n_outer * GROUPS_PER_SLAB * n_inner)


def _s5_operators(a_re, a_im, log_dt, b_re, b_im, c_re, c_im, d_skip):
    hp = lax.Precision.HIGHEST
    a_re, a_im, b_re, b_im, c_re, c_im = (v.astype(F32) for v in (a_re, a_im, b_re, b_im, c_re, c_im))
    dt = jnp.exp(log_dt.astype(F32))[:, None]
    z_re = a_re * dt
    z_im = a_im * dt

    def lam_pow(n):
        mag = jnp.exp(z_re[None] * n[:, None, None])
        ang = z_im[None] * n[:, None, None]
        return mag * jnp.cos(ang), mag * jnp.sin(ang)

    steps = jnp.arange(S5_CHUNK + 1, dtype=F32)
    pw_re, pw_im = lam_pow(steps)
    lb_re, lb_im = pw_re[1], pw_im[1]
    den = a_re * a_re + a_im * a_im
    q_re = ((lb_re - 1.0) * a_re + lb_im * a_im) / den
    q_im = (lb_im * a_re - (lb_re - 1.0) * a_im) / den
    bb_re = q_re[:, :, None] * b_re - q_im[:, :, None] * b_im
    bb_im = q_re[:, :, None] * b_im + q_im[:, :, None] * b_re

    cp_re = c_re[None] * pw_re[:S5_CHUNK, :, None, :] - c_im[None] * pw_im[:S5_CHUNK, :, None, :]
    cp_im = c_re[None] * pw_im[:S5_CHUNK, :, None, :] + c_im[None] * pw_re[:S5_CHUNK, :, None, :]
    m = jnp.einsum("ngcq,gqd->ngcd", jnp.concatenate([cp_re, -cp_im], axis=3),
                   jnp.concatenate([bb_re, bb_im], axis=1), precision=hp)
    m_ext = jnp.concatenate([m, jnp.zeros_like(m[:1])], axis=0)
    n_blk = S5_CHUNK // 2
    d_i = (n_blk - 1 - jnp.arange(n_blk))[:, None, None]
    s_i = jnp.arange(2)[None, :, None]
    t_i = jnp.arange(2)[None, None, :]
    lag = 2 * d_i + t_i - s_i
    mg = _by_slab(m_ext[lag], 3)
    wr = mg.transpose(3, 0, 1, 4, 6, 2, 5).reshape(N_SLABS, CHUNK_COLS, 2 * S5_GROUP)

    pr_re, pr_im = lam_pow(S5_CHUNK - 1 - steps[:S5_CHUNK])
    w_re = pr_re[:, :, :, None] * bb_re[None] - pr_im[:, :, :, None] * bb_im[None]
    w_im = pr_re[:, :, :, None] * bb_im[None] + pr_im[:, :, :, None] * bb_re[None]

    def e_half(v):
        return _by_slab(v, 1).transpose(1, 0, 2, 4, 3).reshape(N_SLABS, CHUNK_COLS, S5_STATE)

    we = jnp.concatenate([e_half(w_re), e_half(w_im)], axis=2)

    g_re = c_re[None] * pw_re[1:, :, None, :] - c_im[None] * pw_im[1:, :, None, :]
    g_im = c_re[None] * pw_im[1:, :, None, :] + c_im[None] * pw_re[1:, :, None, :]

    def f_half(v):
        return _by_slab(v, 1).transpose(1, 4, 0, 2, 3).reshape(N_SLABS, S5_STATE, CHUNK_COLS)

    wf = jnp.concatenate([f_half(g_re), f_half(-g_im)], axis=1)

    def slab_vec(v):
        return v.reshape(N_SLABS, 1, SLAB_STATE)

    d_slab = d_skip.astype(F32).reshape(N_SLABS, 1, LANES)
    tile_e = _replicate_over_groups(2, S5_STATE)

    def bd_half(v):
        return _group_diag(_by_slab(v, 0).transpose(0, 1, 3, 2), 1, 3).reshape(N_SLABS, LANES, SLAB_STATE)

    def cd_half(v):
        return _group_diag(_by_slab(v, 0).transpose(0, 1, 3, 2), 1, 3).reshape(N_SLABS, SLAB_STATE, LANES)

    bd = jnp.concatenate([bd_half(bb_re), bd_half(bb_im)], axis=2)
    cd = jnp.concatenate([cd_half(c_re), cd_half(-c_im)], axis=1)
    bd_hi = bd.astype(BF16)
    bd_lo = (bd - bd_hi.astype(F32)).astype(BF16)
    return {
        "wr": wr.astype(BF16), "we": we.astype(BF16), "wf": wf.astype(BF16),
        "tile_r": _replicate_over_groups(2, S5_GROUP).astype(BF16),
        "tile_e": tile_e.astype(BF16), "tile_f": tile_e.T.astype(BF16),
        "a_re": slab_vec(pw_re[S5_CHUNK]), "a_im": slab_vec(pw_im[S5_CHUNK]),
        "d_chunk": jnp.tile(d_slab, (1, 1, S5_CHUNK)), "d_slab": d_slab,
        "bd_hi": bd_hi, "bd_lo": bd_lo, "cd": cd.astype(BF16),
        "l_re": slab_vec(lb_re), "l_im": slab_vec(lb_im),
    }


def _layer_params(w_in, w_uq, w_uk, w_uv, w_glu, w_out, norm_mix, norm_q, norm_kv, norm_attn_out, norm_ssm_out,
                  b_glu):
    c0, c1, c2 = Q_LORA, Q_LORA + KV_LORA, Q_LORA + KV_LORA + QK_ROPE
    w_uq_h = w_uq.reshape(Q_LORA, MLA_HEADS, QK_NOPE + QK_ROPE)
    return {
        "w_q": w_in[:, :c0].astype(BF16),
        "w_kv": w_in[:, c0:c1].astype(BF16),
        "w_kr_t": w_in[:, c1:c2].T.astype(BF16),
        "w_u": w_in[:, c2:].astype(BF16),
        "w_uq_nope": w_uq_h[:, :, :QK_NOPE].reshape(Q_LORA, MLA_HEADS * QK_NOPE).astype(BF16),
        "w_uq_rope_t": w_uq_h[:, :, QK_NOPE:].reshape(Q_LORA, MLA_HEADS * QK_ROPE).T.astype(BF16),
        "w_uk_h": w_uk.transpose(1, 0, 2).astype(BF16),
        "w_uv_h": w_uv.transpose(1, 2, 0).astype(BF16),
        "w_glu": w_glu.astype(BF16),
        "w_out_attn": w_out[:MLA_WIDTH].astype(BF16),
        "w_out_ssm": w_out[MLA_WIDTH:].astype(BF16),
        "norm_mix": norm_mix.reshape(1, D_MODEL),
        "norm_q": norm_q.reshape(1, Q_LORA),
        "norm_kv": norm_kv.reshape(1, KV_LORA),
        "norm_attn_out": norm_attn_out.reshape(1, MLA_WIDTH),
        "norm_attn_out_col": norm_attn_out.reshape(MLA_WIDTH, 1),
        "norm_ssm_out": norm_ssm_out.reshape(1, S5_WIDTH),
        "b_glu": b_glu.reshape(1, S5_WIDTH),
    }


def kernel(x_prompt, x_sample, c_prompt, c_sample, cache_ckv, cache_krope, state_s5_re, state_s5_im, page_table, w_ada, b_ada, norm_ffn1, ffn1_w1, ffn1_w3, ffn1_w2, norm_mix, w_in, norm_q, w_uq, norm_kv, w_uk, w_uv, s5_a_re, s5_a_im, s5_log_dt, s5_b_re, s5_b_im, s5_c_re, s5_c_im, s5_d, w_glu, b_glu, norm_attn_out, norm_ssm_out, w_out, norm_ffn2, ffn2_w1, ffn2_w3, ffn2_w2, norm_final):
    bp, seq, _ = x_prompt.shape
    bs = x_sample.shape[0]
    depth = w_ada.shape[0]
    assert depth == 1 and x_sample.shape[1] == 1
    n_pages = page_table.shape[1]
    past_len = n_pages * PAGE
    l = 0

    pad = (-(bs + bp)) % 8
    c_all = jnp.concatenate([c_sample, c_prompt, jnp.zeros((pad, D_MODEL), F32)], axis=0)
    mod = _ada(c_all, w_ada[l], b_ada[l])
    mod_s = mod[:bs].reshape(1, bs, ADA_CHUNKS * D_MODEL)
    mod_p = mod[bs:bs + bp].reshape(bp, 1, ADA_CHUNKS * D_MODEL)

    p = _layer_params(w_in[l], w_uq[l], w_uk[l], w_uv[l], w_glu[l], w_out[l], norm_mix[l], norm_q[l], norm_kv[l],
                      norm_attn_out[l], norm_ssm_out[l], b_glu[l])
    ops = _s5_operators(s5_a_re[l], s5_a_im[l], s5_log_dt[l], s5_b_re[l], s5_b_im[l], s5_c_re[l], s5_c_im[l],
                        s5_d[l])
    cos_p, sin_p = _rope_tables(jnp.arange(seq))
    cos_s, sin_s = _rope_tables(jnp.full((bs,), past_len))

    xs = x_sample.reshape(bs, D_MODEL)
    xs, *f1 = _ffn_cast(xs, mod_s, 0, norm_ffn1[l], ffn1_w1[l], ffn1_w3[l], ffn1_w2[l])
    q_lat_ts, q_rope_ts, ckv_s, _, _, krope_t_s, _, u_s = _mixin(xs, mod_s, 1, bs, p, cos_s, sin_s, tm=bs, tq=bs)
    krope_s = krope_t_s[0].T
    q_lat_s = q_lat_ts.reshape(KV_LORA, MLA_HEADS, bs).transpose(2, 1, 0)
    q_rope_s = q_rope_ts.reshape(QK_ROPE, MLA_HEADS, bs).transpose(2, 1, 0)
    o_lat = _sample_attn(page_table, q_lat_s, q_rope_s,
                         ckv_s.reshape(bs, 1, KV_LORA), krope_s.reshape(bs, 1, QK_ROPE), cache_ckv[l:l + 1],
                         cache_krope[l:l + 1].transpose(0, 1, 3, 2))
    attn_s = _sample_attn_out(o_lat.transpose(1, 0, 2), p["w_uv_h"], p["norm_attn_out"])
    y_s, hre_s, him_s = _s5_sample(u_s.reshape(bs, S5_WIDTH), state_s5_re[l], state_s5_im[l], ops)
    xs = _mixout(xs, attn_s, y_s, mod_s, 1, p, tm=bs)
    y_sample, *f2 = _ffn_cast(xs, mod_s, 6, norm_ffn2[l], ffn2_w1[l], ffn2_w3[l], ffn2_w2[l], norm_final)
    y_sample = y_sample.reshape(bs, 1, D_MODEL)

    tm_p = TILES.ffn_rows
    xp = x_prompt.reshape(bp * seq, D_MODEL)
    xp = _ffn(xp, mod_p, 0, seq // tm_p, norm_ffn1[l], *f1, tm=tm_p)
    tm_mix = TILES.mix_rows
    tq_p = TILES.attn_q
    q_lat_t, q_rope_t, ckv_p, kc_p, kct_p, krope_t_p, kr_p, u_p = _mixin(xp, mod_p, seq // tm_mix, seq, p, cos_p,
                                                                         sin_p, tm=tm_mix, tq=tq_p)
    attn_p = _prompt_attn(q_lat_t, q_rope_t, kc_p, kr_p, kct_p, p["w_uv_h"], p["norm_attn_out_col"], tq=tq_p,
                          tk=TILES.attn_k)
    y_p, hre_p, him_p = _s5_prompt(u_p, ops)
    xp = _mixout(xp, attn_p.reshape(bp * seq, MLA_WIDTH), y_p.reshape(bp * seq, S5_WIDTH), mod_p, seq // tm_mix, p,
                 tm=tm_mix)
    y_prompt = _ffn(xp, mod_p, 6, seq // tm_p, norm_ffn2[l], *f2, norm_final, tm=tm_p).reshape(bp, seq, D_MODEL)

    return (y_prompt, y_sample,
            ckv_p[None], krope_t_p.transpose(0, 2, 1)[None],
            ckv_s.reshape(1, bs, 1, KV_LORA), krope_s.reshape(1, bs, 1, QK_ROPE),
            hre_p[None], him_p[None], hre_s[None], him_s[None])
```
